```python
import jax, jax.numpy as jnp
from jax import lax
import numpy as np

D_MODEL = 1024
BATCH = 2
SEQ = 16384
DEPTH = 4

GRID_W = 64
CTX_LEN = 256
HEAD_DIM = 64
ATTN_SCALE = HEAD_DIM ** -0.5
CONV_A_CH = D_MODEL // 2
CONV_A_WIDTH = 31
CONV_B_CH = D_MODEL // 2
CONV_B_WIDTH = 3
CONV_SPLITS = [CONV_A_CH, 2 * CONV_A_CH, 2 * CONV_A_CH + CONV_B_CH, 2 * CONV_A_CH + 2 * CONV_B_CH]
CONV_IN_COLS = 2 * CONV_A_CH + 3 * CONV_B_CH
CONV_OUT_ROWS = CONV_A_CH + CONV_B_CH
GQA_Q_HEADS = (D_MODEL // 2) // HEAD_DIM
GQA_KV_HEADS = GQA_Q_HEADS // 4
NA_HEADS = (D_MODEL // 2) // HEAD_DIM
NA_WIN_ROWS = 8
NA_WIN_COLS = 16
Q_BLOCK = 128
ROPE_THETA = 10000.0
ROPE_AXIS_DIM = HEAD_DIM // 2
Q_COLS_C = GQA_Q_HEADS * HEAD_DIM
KV_COLS_C = GQA_KV_HEADS * HEAD_DIM
COLS_D = NA_HEADS * HEAD_DIM
ATTN_SPLITS = [Q_COLS_C, Q_COLS_C + KV_COLS_C, Q_COLS_C + 2 * KV_COLS_C,
               Q_COLS_C + 2 * KV_COLS_C + COLS_D, Q_COLS_C + 2 * KV_COLS_C + 2 * COLS_D]
ATTN_IN_COLS = Q_COLS_C + 2 * KV_COLS_C + 3 * COLS_D
ATTN_OUT_ROWS = Q_COLS_C + COLS_D
N_GROUPS = 4
EXPERTS_PER_GROUP = 8
N_EXPERTS = N_GROUPS * EXPERTS_PER_GROUP
TOP_K_INNER = 2
EXPERT_HIDDEN = D_MODEL // 2
DISPATCH_BLOCK = 128
DEEPNORM_ALPHA = (2 * DEPTH) ** 0.25
DEEPNORM_BETA = (8 * DEPTH) ** -0.25
N_EVEN = (DEPTH + 1) // 2
N_ODD = DEPTH // 2
LN_EPS = 1e-5
RMS_EPS = 1e-6

kernel_name = 'hybrid_dit_conv_attn_hmoe'


def layer_norm(x, g, b):
    xf = x.astype(jnp.float32)
    mu = jnp.mean(xf, axis=-1, keepdims=True)
    var = jnp.mean(jnp.square(xf - mu), axis=-1, keepdims=True)
    return ((xf - mu) * lax.rsqrt(var + LN_EPS) * g + b).astype(x.dtype)


def rms_norm(x, g):
    xf = x.astype(jnp.float32)
    return (xf * lax.rsqrt(jnp.mean(xf * xf, axis=-1, keepdims=True) + RMS_EPS) * g).astype(x.dtype)


def ada_split(cond, w, b):
    return jnp.split(jax.nn.silu(cond) @ w + b, 6, axis=-1)


def depthwise_conv(h, w):
    k = w.shape[0]
    return lax.conv_general_dilated(h, w[:, None, :].astype(h.dtype), (1,), [(k // 2, k // 2)],
                                    dimension_numbers=('NWC', 'WIO', 'NWC'),
                                    feature_group_count=h.shape[-1])


def split_heads(z, n):
    b, l, _ = z.shape
    return z.reshape(b, l, n, HEAD_DIM).transpose(0, 2, 1, 3)


def merge_heads(o):
    b, n, l, d = o.shape
    return o.transpose(0, 2, 1, 3).reshape(b, l, n * d)


def axial_rope_tables(n_tokens):
    t = jnp.arange(n_tokens)
    row = (t // GRID_W).astype(jnp.float32)
    col = (t % GRID_W).astype(jnp.float32)
    inv = ROPE_THETA ** (-jnp.arange(0, ROPE_AXIS_DIM, 2, dtype=jnp.float32) / ROPE_AXIS_DIM)
    ar = row[:, None] * inv
    ac = col[:, None] * inv
    return jnp.cos(ar), jnp.sin(ar), jnp.cos(ac), jnp.sin(ac)


def rotate(x, cos, sin):
    x1, x2 = jnp.split(x, 2, axis=-1)
    return jnp.concatenate([x1 * cos - x2 * sin, x1 * sin + x2 * cos], axis=-1)


def apply_axial_rope(x, tables):
    cr, sr, cc, sc = [t.astype(x.dtype) for t in tables]
    xr, xc = jnp.split(x, 2, axis=-1)
    return jnp.concatenate([rotate(xr, cr, sr), rotate(xc, cc, sc)], axis=-1)


def softmax_attention(q, k, v):
    s = jnp.einsum('bkgqd,bksd->bkgqs', q, k, preferred_element_type=jnp.float32) * ATTN_SCALE
    p = jax.nn.softmax(s, axis=-1).astype(v.dtype)
    return jnp.einsum('bkgqs,bksd->bkgqd', p, v)


def gqa_latent(q, k_all, v_all):
    b, hk, g, s, d = q.shape
    nb = s // Q_BLOCK
    qb = jnp.moveaxis(q.reshape(b, hk, g, nb, Q_BLOCK, d), 3, 0)
    o = lax.map(lambda qi: softmax_attention(qi, k_all, v_all), qb)
    return jnp.moveaxis(o, 0, 3).reshape(b, hk * g, s, d)


def neighbourhood_latent(q, k, v, k_ctx, v_ctx, rel_bias):
    b, h, s, d = q.shape
    rows = s // GRID_W
    kr = min(NA_WIN_ROWS, rows)
    n_ctx = k_ctx.shape[2]
    qg = jnp.moveaxis(q.reshape(b, h, rows, GRID_W, d), 2, 0)
    kg = k.reshape(b, h, rows, GRID_W, d)
    vg = v.reshape(b, h, rows, GRID_W, d)
    qcol = jnp.arange(GRID_W)
    cstart = jnp.clip(qcol - NA_WIN_COLS // 2, 0, GRID_W - NA_WIN_COLS)
    in_win = (qcol[None, :] >= cstart[:, None]) & (qcol[None, :] < cstart[:, None] + NA_WIN_COLS)
    col_idx = jnp.clip(qcol[None, :] - qcol[:, None] + NA_WIN_COLS - 1, 0, 2 * NA_WIN_COLS - 2)
    bias_cols = rel_bias[:, :, col_idx]

    def one_row(args):
        qr, r = args
        rs = jnp.clip(r - kr // 2, 0, rows - kr)
        kb = lax.dynamic_slice_in_dim(kg, rs, kr, axis=2)
        vb = lax.dynamic_slice_in_dim(vg, rs, kr, axis=2)
        s_lat = jnp.einsum('bhqd,bhrkd->bhqrk', qr, kb, preferred_element_type=jnp.float32) * ATTN_SCALE
        row_idx = rs + jnp.arange(kr) - r + NA_WIN_ROWS - 1
        bias = jnp.take(bias_cols, row_idx, axis=1).transpose(0, 2, 1, 3).astype(jnp.float32)
        s_lat = jnp.where(in_win[:, None, :], s_lat + bias, -jnp.inf).reshape(b, h, GRID_W, kr * GRID_W)
        s_ctx = jnp.einsum('bhqd,bhcd->bhqc', qr, k_ctx, preferred_element_type=jnp.float32) * ATTN_SCALE
        p = jax.nn.softmax(jnp.concatenate([s_ctx, s_lat], axis=-1), axis=-1).astype(v.dtype)
        p_ctx = p[..., :n_ctx]
        p_lat = p[..., n_ctx:].reshape(b, h, GRID_W, kr, GRID_W)
        return (jnp.einsum('bhqc,bhcd->bhqd', p_ctx, v_ctx)
                + jnp.einsum('bhqrk,bhrkd->bhqd', p_lat, vb))

    o = lax.map(one_row, (qg, jnp.arange(rows)))
    return jnp.moveaxis(o, 0, 2).reshape(b, h, s, d)


def conv_mixers(h, w_in, w_out, a_dw_w, a_dw_b, a_ln_g, a_ln_b, b_dw_w):
    z = h @ w_in
    a_val, a_gate, b_gate, c_gate, b_val = jnp.split(z, CONV_SPLITS, axis=-1)
    ya = a_val * jax.nn.sigmoid(a_gate)
    ya = depthwise_conv(ya, a_dw_w) + a_dw_b
    ya = jax.nn.silu(layer_norm(ya, a_ln_g, a_ln_b))
    yb = b_gate * depthwise_conv(c_gate * b_val, b_dw_w)
    return jnp.concatenate([ya, yb], axis=-1) @ w_out


def attn_mixers(h_lat, h_ctx, w_in, w_out, q_g, k_g, rel_bias, with_ctx_out):
    b, s, _ = h_lat.shape
    n_ctx = h_ctx.shape[1]
    group = GQA_Q_HEADS // GQA_KV_HEADS
    ql_c, kl_c, vl_c, ql_d, kl_d, vl_d = jnp.split(h_lat @ w_in, ATTN_SPLITS, axis=-1)
    qc_c, kc_c, vc_c, qc_d, kc_d, vc_d = jnp.split(h_ctx @ w_in, ATTN_SPLITS, axis=-1)
    tables = axial_rope_tables(s)
    q_lat = apply_axial_rope(rms_norm(split_heads(ql_c, GQA_Q_HEADS), q_g), tables)
    q_lat = q_lat.reshape(b, GQA_KV_HEADS, group, s, HEAD_DIM)
    k_lat = apply_axial_rope(rms_norm(split_heads(kl_c, GQA_KV_HEADS), k_g), tables)
    v_lat = split_heads(vl_c, GQA_KV_HEADS)
    k_ctx = rms_norm(split_heads(kc_c, GQA_KV_HEADS), k_g)
    v_ctx = split_heads(vc_c, GQA_KV_HEADS)
    o_c = gqa_latent(q_lat, jnp.concatenate([k_ctx, k_lat], axis=2), jnp.concatenate([v_ctx, v_lat], axis=2))
    kd_ctx = split_heads(kc_d, NA_HEADS)
    vd_ctx = split_heads(vc_d, NA_HEADS)
    o_d = neighbourhood_latent(split_heads(ql_d, NA_HEADS), split_heads(kl_d, NA_HEADS),
                               split_heads(vl_d, NA_HEADS), kd_ctx, vd_ctx, rel_bias)
    y_lat = jnp.concatenate([merge_heads(o_c), merge_heads(o_d)], axis=-1) @ w_out
    if not with_ctx_out:
        return y_lat, None
    q_ctx = rms_norm(split_heads(qc_c, GQA_Q_HEADS), q_g).reshape(b, GQA_KV_HEADS, group, n_ctx, HEAD_DIM)
    oc_c = softmax_attention(q_ctx, k_ctx, v_ctx).reshape(b, GQA_Q_HEADS, n_ctx, HEAD_DIM)
    oc_d = softmax_attention(split_heads(qc_d, NA_HEADS)[:, :, None], kd_ctx, vd_ctx)[:, :, 0]
    y_ctx = jnp.concatenate([merge_heads(oc_c), merge_heads(oc_d)], axis=-1) @ w_out
    return y_lat, y_ctx


def grouped_experts(x, expert, gate, w_gate, w_up, w_down):
    n, d = x.shape
    k = expert.shape[1]
    m = n * k
    flat_e = expert.reshape(m)
    flat_tok = jnp.arange(m) // k
    flat_w = gate.reshape(m)
    order = jnp.argsort(flat_e)
    se = flat_e[order]
    counts = jnp.bincount(flat_e, length=N_EXPERTS)
    padded = (counts + DISPATCH_BLOCK - 1) // DISPATCH_BLOCK * DISPATCH_BLOCK
    pad_end = jnp.cumsum(padded)
    pad_start = pad_end - padded
    start = jnp.cumsum(counts) - counts
    dest = pad_start[se] + jnp.arange(m) - start[se]
    n_blocks = -(-m // DISPATCH_BLOCK) + N_EXPERTS
    cap = n_blocks * DISPATCH_BLOCK
    slot_tok = jnp.full((cap,), n, jnp.int32).at[dest].set(flat_tok[order])
    slot_w = jnp.zeros((cap,), x.dtype).at[dest].set(flat_w[order])
    xs = jnp.concatenate([x, jnp.zeros((1, d), x.dtype)], axis=0)[slot_tok].reshape(n_blocks, DISPATCH_BLOCK, d)
    block_e = jnp.minimum(jnp.searchsorted(pad_end, jnp.arange(n_blocks) * DISPATCH_BLOCK, side='right'),
                          N_EXPERTS - 1)

    def run(args):
        xb, e = args
        return (jax.nn.silu(xb @ w_gate[e]) * (xb @ w_up[e])) @ w_down[e]

    ys = lax.map(run, (xs, block_e)).reshape(cap, d)
    return jnp.zeros((n + 1, d), x.dtype).at[slot_tok].add(ys * slot_w[:, None])[:n]


def hier_moe(tokens, w1, b1, w2, b2, w_gate, w_up, w_down):
    n = tokens.shape[0]
    p1 = jax.nn.softmax((tokens @ w1 + b1).astype(jnp.float32), axis=-1)
    p_grp, grp = lax.top_k(p1, 1)
    lg2 = (tokens @ w2 + b2).astype(jnp.float32).reshape(n, N_GROUPS, EXPERTS_PER_GROUP)
    lg2 = jnp.take_along_axis(lg2, grp[:, :, None], axis=1)[:, 0]
    top_val, top_idx = lax.top_k(lg2, TOP_K_INNER)
    gate = p_grp * jax.nn.softmax(top_val, axis=-1)
    expert = grp * EXPERTS_PER_GROUP + top_idx
    return grouped_experts(tokens, expert, gate.astype(tokens.dtype), w_gate, w_up, w_down)


def setup_inputs(seed: int = 0) -> dict:
    key = jax.random.key(seed)
    keys = jax.random.split(key, 32)
    counter = [0]
    D = D_MODEL

    def nrm(shape, scale):
        kk = keys[counter[0]]
        counter[0] += 1
        return jax.random.normal(kk, shape, jnp.float32) * scale

    return {
        'x': nrm((BATCH, SEQ, D), 1.0),
        'c': nrm((BATCH, D), 1.0),
        'ctx': nrm((BATCH, CTX_LEN, D), 1.0),
        'c_ctx': nrm((D,), 1.0),
        'ada_w': nrm((DEPTH, D, 6 * D), 0.5 * D ** -0.5),
        'ada_b': nrm((DEPTH, 6 * D), 0.02),
        'ln1_g': 1.0 + nrm((DEPTH, D), 0.02),
        'ln1_b': nrm((DEPTH, D), 0.02),
        'ln2_g': 1.0 + nrm((DEPTH, D), 0.02),
        'ln2_b': nrm((DEPTH, D), 0.02),
        'conv_w_in': nrm((N_EVEN, D, CONV_IN_COLS), D ** -0.5),
        'conv_w_out': nrm((N_EVEN, CONV_OUT_ROWS, D), DEEPNORM_BETA * CONV_OUT_ROWS ** -0.5),
        'conv_a_dw_w': nrm((N_EVEN, CONV_A_WIDTH, CONV_A_CH), CONV_A_WIDTH ** -0.5),
        'conv_a_dw_b': nrm((N_EVEN, CONV_A_CH), 0.02),
        'conv_a_ln_g': 1.0 + nrm((N_EVEN, CONV_A_CH), 0.02),
        'conv_a_ln_b': nrm((N_EVEN, CONV_A_CH), 0.02),
        'conv_b_dw_w': nrm((N_EVEN, CONV_B_WIDTH, CONV_B_CH), CONV_B_WIDTH ** -0.5),
        'attn_w_in': nrm((N_ODD, D, ATTN_IN_COLS), D ** -0.5),
        'attn_w_out': nrm((N_ODD, ATTN_OUT_ROWS, D), DEEPNORM_BETA * ATTN_OUT_ROWS ** -0.5),
        'q_norm_g': 1.0 + nrm((N_ODD, HEAD_DIM), 0.02),
        'k_norm_g': 1.0 + nrm((N_ODD, HEAD_DIM), 0.02),
        'na_rel_bias': nrm((N_ODD, NA_HEADS, 2 * NA_WIN_ROWS - 1, 2 * NA_WIN_COLS - 1), 0.05),
        'router_w1': nrm((DEPTH, D, N_GROUPS), D ** -0.5),
        'router_b1': nrm((DEPTH, N_GROUPS), 0.01),
        'router_w2': nrm((DEPTH, D, N_EXPERTS), D ** -0.5),
        'router_b2': nrm((DEPTH, N_EXPERTS), 0.01),
        'expert_w_gate': nrm((DEPTH, N_EXPERTS, D, EXPERT_HIDDEN), D ** -0.5),
        'expert_w_up': nrm((DEPTH, N_EXPERTS, D, EXPERT_HIDDEN), D ** -0.5),
        'expert_w_down': nrm((DEPTH, N_EXPERTS, EXPERT_HIDDEN, D), DEEPNORM_BETA * EXPERT_HIDDEN ** -0.5),
    }


def reference(x, c, ctx, c_ctx, ada_w, ada_b, ln1_g, ln1_b, ln2_g, ln2_b,
              conv_w_in, conv_w_out, conv_a_dw_w, conv_a_dw_b, conv_a_ln_g, conv_a_ln_b, conv_b_dw_w,
              attn_w_in, attn_w_out, q_norm_g, k_norm_g, na_rel_bias,
              router_w1, router_b1, router_w2, router_b2, expert_w_gate, expert_w_up, expert_w_down):
    d = x.shape[-1]
    x_lat, x_ctx = x, ctx
    for i in range(DEPTH):
        last = i == DEPTH - 1
        odd = i % 2 == 1
        j = i // 2
        sh1, sc1, g1, sh2, sc2, g2 = [m[:, None, :] for m in ada_split(c, ada_w[i], ada_b[i])]
        csh1, csc1, cg1, csh2, csc2, cg2 = ada_split(c_ctx, ada_w[i], ada_b[i])
        moe_w = (router_w1[i], router_b1[i], router_w2[i], router_b2[i],
                 expert_w_gate[i], expert_w_up[i], expert_w_down[i])
        h_lat = x_lat * (1.0 + sc1) + sh1
        h_ctx = x_ctx * (1.0 + csc1) + csh1 if (odd or not last) else None
        if odd:
            y_lat, y_ctx = attn_mixers(h_lat, h_ctx, attn_w_in[j], attn_w_out[j], q_norm_g[j], k_norm_g[j],
                                       na_rel_bias[j], not last)
        else:
            conv_w = (conv_w_in[j], conv_w_out[j], conv_a_dw_w[j], conv_a_dw_b[j],
                      conv_a_ln_g[j], conv_a_ln_b[j], conv_b_dw_w[j])
            y_lat = conv_mixers(h_lat, *conv_w)
            y_ctx = None if last else conv_mixers(h_ctx, *conv_w)
        x_lat = layer_norm(DEEPNORM_ALPHA * x_lat + g1 * y_lat, ln1_g[i], ln1_b[i])
        h_lat = x_lat * (1.0 + sc2) + sh2
        if last:
            y = hier_moe(h_lat.reshape(-1, d), *moe_w).reshape(x_lat.shape)
            x_lat = layer_norm(DEEPNORM_ALPHA * x_lat + g2 * y, ln2_g[i], ln2_b[i])
        else:
            x_ctx = layer_norm(DEEPNORM_ALPHA * x_ctx + cg1 * y_ctx, ln1_g[i], ln1_b[i])
            h_ctx = x_ctx * (1.0 + csc2) + csh2
            n_lat = h_lat.shape[0] * h_lat.shape[1]
            y = hier_moe(jnp.concatenate([h_lat.reshape(-1, d), h_ctx.reshape(-1, d)], axis=0), *moe_w)
            x_lat = layer_norm(DEEPNORM_ALPHA * x_lat + g2 * y[:n_lat].reshape(x_lat.shape), ln2_g[i], ln2_b[i])
            x_ctx = layer_norm(DEEPNORM_ALPHA * x_ctx + cg2 * y[n_lat:].reshape(x_ctx.shape), ln2_g[i], ln2_b[i])
    return x_lat
```

```python
import functools
import math

import jax
import jax.numpy as jnp
from jax import lax
from jax.experimental import pallas as pl
from jax.experimental.pallas import tpu as pltpu

F32 = jnp.float32
BF16 = jnp.bfloat16
HIGHEST = lax.Precision.HIGHEST

GRID_W = 64
HEAD_DIM = 64
ATTN_SCALE = HEAD_DIM ** -0.5
LOG2E = math.log2(math.e)
CONV_A_WIDTH = 31
CONV_B_WIDTH = 3
CONV_HALO = 16
NA_WIN_ROWS = 8
NA_WIN_COLS = 16
NA_Q_ROWS = 4
NA_BAND_ROWS = 12
ROPE_THETA = 10000.0
N_GROUPS = 4
EXPERTS_PER_GROUP = 8
N_EXPERTS = N_GROUPS * EXPERTS_PER_GROUP
LN_EPS = 1e-5
RMS_EPS = 1e-6
NEG_BIG = -1e30

VMEM_LIMIT_BYTES = 48 * 1024 * 1024
LANES = 128
ROUTER_COLS = LANES
EXPERT_BLOCK = 256
ROW_COPY_CHUNK = 1024
RANK_BLOCK = 512


def _cparams(*sem):
    return pltpu.CompilerParams(dimension_semantics=sem, vmem_limit_bytes=VMEM_LIMIT_BYTES)


def _layer_norm(x, g, b):
    mu = jnp.mean(x, axis=-1, keepdims=True)
    xc = x - mu
    var = jnp.mean(xc * xc, axis=-1, keepdims=True)
    return xc * lax.rsqrt(var + LN_EPS) * g + b


def _ada_kernel(c_ref, w_ref, b_ref, o_ref):
    c = c_ref[...]
    s = c * jax.nn.sigmoid(c)
    o_ref[0] = jnp.dot(s, w_ref[0], preferred_element_type=F32, precision=HIGHEST) + b_ref[0]


def _ada(cond8, ada_w, ada_b):
    n_layers, d, n = ada_w.shape
    tn = 1536
    return pl.pallas_call(
        _ada_kernel,
        grid=(n_layers, n // tn),
        in_specs=[
            pl.BlockSpec((8, d), lambda l, j: (0, 0)),
            pl.BlockSpec((1, d, tn), lambda l, j: (l, 0, j)),
            pl.BlockSpec((1, 1, tn), lambda l, j: (l, 0, j)),
        ],
        out_specs=pl.BlockSpec((1, 8, tn), lambda l, j: (l, 0, j)),
        out_shape=jax.ShapeDtypeStruct((n_layers, 8, n), F32),
        compiler_params=_cparams("parallel", "parallel"),
        name="ada_mod",
    )(cond8, ada_w, ada_b.reshape(n_layers, 1, n))


def _mod_matmul_kernel(x_ref, sc_ref, sh_ref, w_ref, o_ref):
    h = x_ref[0] * (1.0 + sc_ref[0]) + sh_ref[0]
    o_ref[0] = jnp.dot(h.astype(BF16), w_ref[...], preferred_element_type=F32)


def _mod_matmul(x, sc, sh, w):
    b, s, d = x.shape
    n = w.shape[1]
    tm = min(s, 512)
    return pl.pallas_call(
        _mod_matmul_kernel,
        grid=(b, s // tm),
        in_specs=[
            pl.BlockSpec((1, tm, d), lambda bi, i: (bi, i, 0)),
            pl.BlockSpec((1, 1, d), lambda bi, i: (bi, 0, 0)),
            pl.BlockSpec((1, 1, d), lambda bi, i: (bi, 0, 0)),
            pl.BlockSpec((d, n), lambda bi, i: (0, 0)),
        ],
        out_specs=pl.BlockSpec((1, tm, n), lambda bi, i: (bi, i, 0)),
        out_shape=jax.ShapeDtypeStruct((b, s, n), F32),
        compiler_params=_cparams("parallel", "parallel"),
        name="mod_matmul",
    )(x, sc, sh, w)


def _conv_kernel(z_ref, zp_ref, zn_ref, aw_ref, ab_ref, ag_ref, abb_ref, bw_ref,
                 o_ref, ea_ref, eb_ref, *, tile, ch):
    i = pl.program_id(1)
    n = pl.num_programs(1)
    halo = CONV_HALO

    def glu(zz):
        return zz[:, 0:ch] * jax.nn.sigmoid(zz[:, ch:2 * ch])

    def cv(zz):
        return zz[:, 3 * ch:4 * ch] * zz[:, 4 * ch:5 * ch]

    pmask = (i > 0).astype(F32)
    nmask = (i < n - 1).astype(F32)
    zp = zp_ref[0]
    zn = zn_ref[0]
    ea_ref[0:halo, :] = glu(zp) * pmask
    eb_ref[0:halo, :] = cv(zp) * pmask
    ea_ref[halo + tile:2 * halo + tile, :] = glu(zn) * nmask
    eb_ref[halo + tile:2 * halo + tile, :] = cv(zn) * nmask
    ea_ref[halo:halo + tile, :] = glu(z_ref[0])
    eb_ref[halo:halo + tile, :] = cv(z_ref[0])

    rows = 32
    pad_a = CONV_A_WIDTH // 2
    pad_b = CONV_B_WIDTH // 2
    for r0 in range(0, tile, rows):
        acc = jnp.zeros((rows, ch), F32) + ab_ref[...]
        for k in range(CONV_A_WIDTH):
            acc = acc + aw_ref[k:k + 1, :] * ea_ref[pl.ds(halo + r0 + k - pad_a, rows), :]
        ya = _layer_norm(acc, ag_ref[...], abb_ref[...])
        ya = ya * jax.nn.sigmoid(ya)
        cb = jnp.zeros((rows, ch), F32)
        for k in range(CONV_B_WIDTH):
            cb = cb + bw_ref[k:k + 1, :] * eb_ref[pl.ds(halo + r0 + k - pad_b, rows), :]
        yb = z_ref[0, r0:r0 + rows, 2 * ch:3 * ch] * cb
        o_ref[0, r0:r0 + rows, 0:ch] = ya.astype(BF16)
        o_ref[0, r0:r0 + rows, ch:2 * ch] = yb.astype(BF16)


def _conv_mixers(z, a_dw_w, a_dw_b, a_ln_g, a_ln_b, b_dw_w):
    b, s, n = z.shape
    ch = n // 5
    tile = min(s, 256)
    hb = tile // CONV_HALO
    last_hb = s // CONV_HALO - 1
    kern = functools.partial(_conv_kernel, tile=tile, ch=ch)
    vec = lambda v: v.reshape(1, ch)
    return pl.pallas_call(
        kern,
        grid=(b, s // tile),
        in_specs=[
            pl.BlockSpec((1, tile, n), lambda bi, i: (bi, i, 0)),
            pl.BlockSpec((1, CONV_HALO, n), lambda bi, i: (bi, jnp.maximum(i * hb - 1, 0), 0)),
            pl.BlockSpec((1, CONV_HALO, n), lambda bi, i: (bi, jnp.minimum((i + 1) * hb, last_hb), 0)),
            pl.BlockSpec((CONV_A_WIDTH, ch), lambda bi, i: (0, 0)),
            pl.BlockSpec((1, ch), lambda bi, i: (0, 0)),
            pl.BlockSpec((1, ch), lambda bi, i: (0, 0)),
            pl.BlockSpec((1, ch), lambda bi, i: (0, 0)),
            pl.BlockSpec((CONV_B_WIDTH, ch), lambda bi, i: (0, 0)),
        ],
        out_specs=pl.BlockSpec((1, tile, 2 * ch), lambda bi, i: (bi, i, 0)),
        out_shape=jax.ShapeDtypeStruct((b, s, 2 * ch), BF16),
        scratch_shapes=[pltpu.VMEM((tile + 2 * CONV_HALO, ch), F32),
                        pltpu.VMEM((tile + 2 * CONV_HALO, ch), F32)],
        compiler_params=_cparams("parallel", "parallel"),
        name="conv_mixers",
    )(z, z, z, a_dw_w, vec(a_dw_b), vec(a_ln_g), vec(a_ln_b), b_dw_w)


def _proj_ln_kernel(*refs, heads, alpha):
    if heads:
        ot_ref, od_ref = refs[:2]
        refs = refs[2:]
    else:
        u_ref = refs[0]
        refs = refs[1:]
    (w_ref, x_ref, g_ref, lng_ref, lnb_ref, sc_ref, sh_ref, wr_ref, br_ref,
     xo_ref, h_ref, lg_ref) = refs
    if heads:
        n_heads = ot_ref.shape[1]
        y = None
        for h in range(n_heads):
            t = lax.dot_general(ot_ref[0, h], w_ref[h * HEAD_DIM:(h + 1) * HEAD_DIM, :],
                                (((0,), (0,)), ((), ())), preferred_element_type=F32)
            y = t if y is None else y + t
        off = n_heads * HEAD_DIM
        for h in range(od_ref.shape[1]):
            y = y + jnp.dot(od_ref[0, h], w_ref[off + h * HEAD_DIM:off + (h + 1) * HEAD_DIM, :],
                            preferred_element_type=F32)
    else:
        y = jnp.dot(u_ref[0], w_ref[...], preferred_element_type=F32)
    xn = _layer_norm(alpha * x_ref[0] + g_ref[0] * y, lng_ref[...], lnb_ref[...])
    xo_ref[0] = xn
    h2 = xn * (1.0 + sc_ref[0]) + sh_ref[0]
    h_ref[0] = h2
    lg_ref[0] = jnp.dot(h2, wr_ref[...], preferred_element_type=F32, precision=HIGHEST) + br_ref[...]


def _proj_ln(mix, w_out, x, g, ln_g, ln_b, sc2, sh2, wr, br, alpha):
    b, s, d = x.shape
    tm = min(s, 512)
    heads = isinstance(mix, tuple)
    if heads:
        ot, od = mix
        mix_specs = [
            pl.BlockSpec((1, ot.shape[1], HEAD_DIM, tm), lambda bi, i: (bi, 0, 0, i)),
            pl.BlockSpec((1, od.shape[1], tm, HEAD_DIM), lambda bi, i: (bi, 0, i, 0)),
        ]
        mix_args = [ot, od]
    else:
        mix_specs = [pl.BlockSpec((1, tm, mix.shape[2]), lambda bi, i: (bi, i, 0))]
        mix_args = [mix]
    row = lambda: pl.BlockSpec((1, tm, d), lambda bi, i: (bi, i, 0))
    per_b = lambda: pl.BlockSpec((1, 1, d), lambda bi, i: (bi, 0, 0))
    vec = lambda: pl.BlockSpec((1, d), lambda bi, i: (0, 0))
    kern = functools.partial(_proj_ln_kernel, heads=heads, alpha=alpha)
    return pl.pallas_call(
        kern,
        grid=(b, s // tm),
        in_specs=mix_specs + [
            pl.BlockSpec(w_out.shape, lambda bi, i: (0, 0)),
            row(), per_b(), vec(), vec(), per_b(), per_b(),
            pl.BlockSpec((d, ROUTER_COLS), lambda bi, i: (0, 0)),
            pl.BlockSpec((1, ROUTER_COLS), lambda bi, i: (0, 0)),
        ],
        out_specs=[row(), row(), pl.BlockSpec((1, tm, ROUTER_COLS), lambda bi, i: (bi, i, 0))],
        out_shape=[jax.ShapeDtypeStruct((b, s, d), F32),
                   jax.ShapeDtypeStruct((b, s, d), F32),
                   jax.ShapeDtypeStruct((b, s, ROUTER_COLS), F32)],
        compiler_params=_cparams("parallel", "parallel"),
        name="proj_res_ln",
    )(*mix_args, w_out, x, g, ln_g.reshape(1, d), ln_b.reshape(1, d), sc2, sh2, wr, br)


def _rank_kernel(e_ref, tri_ref, rank_ref, cnt_ref, carry_ref):
    @pl.when(pl.program_id(0) == 0)
    def _():
        carry_ref[...] = jnp.zeros_like(carry_ref)

    e = e_ref[0]
    blk = e.shape[1]
    ids = lax.broadcasted_iota(jnp.int32, (N_EXPERTS, blk), 0)
    oh = jnp.where(ids == e, 1.0, 0.0).astype(F32)
    before = jnp.dot(oh.astype(BF16), tri_ref[...], preferred_element_type=F32)
    carry = carry_ref[...]
    before = before + carry[:, 0:1]
    rank_ref[0] = jnp.sum(oh * before, axis=0, keepdims=True).astype(jnp.int32)
    carry = carry + jnp.sum(oh, axis=1, keepdims=True)
    carry_ref[...] = carry
    cnt_ref[...] = carry.astype(jnp.int32)


def _expert_ranks(flat_e):
    m = flat_e.shape[0]
    blk = RANK_BLOCK
    nblk = m // blk
    tri = (lax.broadcasted_iota(jnp.int32, (blk, blk), 0)
           < lax.broadcasted_iota(jnp.int32, (blk, blk), 1)).astype(BF16)
    rank, cnt = pl.pallas_call(
        _rank_kernel,
        grid=(nblk,),
        in_specs=[pl.BlockSpec((1, 1, blk), lambda i: (i, 0, 0)),
                  pl.BlockSpec((blk, blk), lambda i: (0, 0))],
        out_specs=[pl.BlockSpec((1, 1, blk), lambda i: (i, 0, 0)),
                   pl.BlockSpec((N_EXPERTS, LANES), lambda i: (0, 0))],
        out_shape=[jax.ShapeDtypeStruct((nblk, 1, blk), jnp.int32),
                   jax.ShapeDtypeStruct((N_EXPERTS, LANES), jnp.int32)],
        scratch_shapes=[pltpu.VMEM((N_EXPERTS, LANES), F32)],
        compiler_params=_cparams("arbitrary"),
        name="expert_rank",
    )(flat_e.reshape(nblk, 1, blk), tri)
    return rank.reshape(m), cnt[:, 0]


def _row_copy_kernel(*refs, chunk, aliased):
    if aliased:
        sidx_ref, didx_ref, src_ref, _, dst_ref, sem = refs
    else:
        sidx_ref, didx_ref, src_ref, dst_ref, sem = refs

    def issue(i, c):
        pltpu.make_async_copy(src_ref.at[sidx_ref[0, 0, i]], dst_ref.at[didx_ref[0, 0, i]], sem).start()
        return c

    lax.fori_loop(0, chunk, issue, 0, unroll=8)
    pltpu.make_async_copy(src_ref.at[pl.ds(0, chunk)], dst_ref.at[pl.ds(0, chunk)], sem).wait()


def _row_copy(src, sidx, didx, n_dst, init=None):
    m = sidx.shape[0]
    chunk = ROW_COPY_CHUNK
    nchunk = m // chunk
    aliased = init is not None
    idx_spec = lambda: pl.BlockSpec((1, 1, chunk), lambda i: (i, 0, 0), memory_space=pltpu.SMEM)
    any_spec = lambda: pl.BlockSpec(memory_space=pl.ANY)
    args = [sidx.reshape(nchunk, 1, chunk), didx.reshape(nchunk, 1, chunk), src]
    in_specs = [idx_spec(), idx_spec(), any_spec()]
    if aliased:
        args.append(init)
        in_specs.append(any_spec())
    return pl.pallas_call(
        functools.partial(_row_copy_kernel, chunk=chunk, aliased=aliased),
        grid=(nchunk,),
        in_specs=in_specs,
        out_specs=any_spec(),
        out_shape=jax.ShapeDtypeStruct((n_dst,) + src.shape[1:], src.dtype),
        scratch_shapes=[pltpu.SemaphoreType.DMA(())],
        input_output_aliases={3: 0} if aliased else {},
        compiler_params=_cparams("arbitrary"),
        name="row_copy",
    )(*args)


def _experts_kernel(be_ref, nu_ref, x_ref, wg_ref, wu_ref, wd_ref, o_ref):
    i = pl.program_id(0)

    @pl.when(i < nu_ref[0])
    def _():
        x = x_ref[...].astype(BF16)
        g = jnp.dot(x, wg_ref[0], preferred_element_type=F32)
        u = jnp.dot(x, wu_ref[0], preferred_element_type=F32)
        a = (g * jax.nn.sigmoid(g) * u).astype(BF16)
        o_ref[...] = jnp.dot(a, wd_ref[0], preferred_element_type=F32)

    @pl.when(i >= nu_ref[0])
    def _():
        o_ref[...] = jnp.zeros_like(o_ref)


def _experts(xs, block_e, n_used, w_gate, w_up, w_down):
    cap, d = xs.shape
    hid = w_gate.shape[2]
    tb = EXPERT_BLOCK
    nb = cap // tb
    grid_spec = pltpu.PrefetchScalarGridSpec(
        num_scalar_prefetch=2,
        grid=(nb,),
        in_specs=[
            pl.BlockSpec((tb, d), lambda i, be, nu: (i, 0)),
            pl.BlockSpec((1, d, hid), lambda i, be, nu: (be[i], 0, 0)),
            pl.BlockSpec((1, d, hid), lambda i, be, nu: (be[i], 0, 0)),
            pl.BlockSpec((1, hid, d), lambda i, be, nu: (be[i], 0, 0)),
        ],
        out_specs=pl.BlockSpec((tb, d), lambda i, be, nu: (i, 0)),
    )
    return pl.pallas_call(
        _experts_kernel,
        grid_spec=grid_spec,
        out_shape=jax.ShapeDtypeStruct((cap, d), F32),
        compiler_params=_cparams("arbitrary"),
        name="experts",
    )(block_e, n_used, xs, w_gate, w_up, w_down)


def _moe_ln_kernel(x_ref, y0_ref, y1_ref, gt_ref, g_ref, lng_ref, lnb_ref, o_ref, *, alpha):
    gt = gt_ref[...]
    y = gt[:, 0:1] * y0_ref[...] + gt[:, 1:2] * y1_ref[...]
    o_ref[...] = _layer_norm(alpha * x_ref[...] + g_ref[0] * y, lng_ref[...], lnb_ref[...])


def _moe_ln(x, y01, gates, tok_off, g, ln_g, ln_b, alpha):
    b, s, d = x.shape
    n_tok = gates.shape[0]
    tm = min(s, 512)
    nt = s // tm
    off = tok_off // tm
    off1 = n_tok // tm + off
    vec = lambda: pl.BlockSpec((1, d), lambda bi, i: (0, 0))
    out = pl.pallas_call(
        functools.partial(_moe_ln_kernel, alpha=alpha),
        grid=(b, nt),
        in_specs=[
            pl.BlockSpec((tm, d), lambda bi, i: (bi * nt + i, 0)),
            pl.BlockSpec((tm, d), lambda bi, i: (off + bi * nt + i, 0)),
            pl.BlockSpec((tm, d), lambda bi, i: (off1 + bi * nt + i, 0)),
            pl.BlockSpec((tm, 2), lambda bi, i: (off + bi * nt + i, 0)),
            pl.BlockSpec((1, 1, d), lambda bi, i: (bi, 0, 0)),
            vec(), vec(),
        ],
        out_specs=pl.BlockSpec((tm, d), lambda bi, i: (bi * nt + i, 0)),
        out_shape=jax.ShapeDtypeStruct((b * s, d), F32),
        compiler_params=_cparams("parallel", "parallel"),
        name="moe_res_ln",
    )(x.reshape(b * s, d), y01, y01, gates, g, ln_g.reshape(1, d), ln_b.reshape(1, d))
    return out.reshape(b, s, d)


def _route(logits):
    n = logits.shape[0]
    l1 = logits[:, :N_GROUPS]
    grp = jnp.argmax(l1, axis=-1).astype(jnp.int32)
    p_grp = 1.0 / jnp.sum(jnp.exp(l1 - jnp.max(l1, axis=-1, keepdims=True)), axis=-1)
    lg2 = logits[:, N_GROUPS:N_GROUPS + N_EXPERTS].reshape(n, N_GROUPS, EXPERTS_PER_GROUP)
    gsel = grp[:, None] == jnp.arange(N_GROUPS, dtype=jnp.int32)[None, :]
    sel = jnp.sum(jnp.where(gsel[:, :, None], lg2, 0.0), axis=1)
    i1 = jnp.argmax(sel, axis=-1).astype(jnp.int32)
    v1 = jnp.max(sel, axis=-1)
    col = jnp.arange(EXPERTS_PER_GROUP, dtype=jnp.int32)[None, :]
    sel2 = jnp.where(col == i1[:, None], -jnp.inf, sel)
    i2 = jnp.argmax(sel2, axis=-1).astype(jnp.int32)
    v2 = jnp.max(sel2, axis=-1)
    e2 = jnp.exp(v2 - v1)
    w1 = 1.0 / (1.0 + e2)
    gates = p_grp[:, None] * jnp.stack([w1, e2 * w1], axis=-1)
    experts = grp[:, None] * EXPERTS_PER_GROUP + jnp.stack([i1, i2], axis=-1)
    return experts, gates.astype(F32)


def _moe(h_all, logits, w_gate, w_up, w_down):
    n, d = h_all.shape
    m = 2 * n
    tb = EXPERT_BLOCK
    experts, gates = _route(logits)
    flat_e = experts.reshape(m)
    rank, counts = _expert_ranks(flat_e)
    padded = (counts + tb - 1) // tb * tb
    pad_end = jnp.cumsum(padded)
    pad_start = pad_end - padded
    onehot = flat_e[:, None] == jnp.arange(N_EXPERTS, dtype=jnp.int32)[None, :]
    dest = jnp.sum(jnp.where(onehot, pad_start[None, :], 0), axis=-1).astype(jnp.int32) + rank
    nb = -(-m // tb) + N_EXPERTS
    cap = nb * tb
    blk_start = jnp.arange(nb, dtype=jnp.int32) * tb
    block_e = jnp.minimum(jnp.sum(pad_end[None, :] <= blk_start[:, None], axis=-1), N_EXPERTS - 1)
    n_used = (pad_end[-1] // tb).reshape(1)

    tiles = d // LANES
    mm = jnp.arange(m, dtype=jnp.int32)
    xs3 = _row_copy(h_all.reshape(n, tiles, LANES), mm // 2, dest, cap,
                    init=jnp.zeros((cap, tiles, LANES), F32))
    ys = _experts(xs3.reshape(cap, d), block_e.astype(jnp.int32), n_used.astype(jnp.int32),
                  w_gate, w_up, w_down)
    y01 = _row_copy(ys.reshape(cap, tiles, LANES), dest, (mm % 2) * n + mm // 2, m)
    return y01.reshape(m, d), gates


def _prep_kernel(z_ref, cos_ref, sin_ref, gq_ref, gk_ref, grp_ref,
                 qt_ref, kc_ref, vt_ref, qd_ref, ktd_ref, vd_ref, *, hq, hkv, hd, ts, dchunk):
    cos = cos_ref[...]
    sin = sin_ref[...]
    lane = lax.broadcasted_iota(jnp.int32, (ts, LANES), 1)
    first_half = (lane % 32) < 16

    def norm_rope(x, g):
        ss = jnp.dot(x * x, grp_ref[...], preferred_element_type=F32, precision=HIGHEST)
        xn = x * lax.rsqrt(ss * (1.0 / HEAD_DIM) + RMS_EPS) * g
        partner = jnp.where(first_half, pltpu.roll(xn, LANES - 16, 1), pltpu.roll(xn, 16, 1))
        return xn * cos + partner * sin

    col = 0
    for s in range(hq // 2):
        x = norm_rope(z_ref[0, :, col:col + LANES], gq_ref[...]) * (ATTN_SCALE * LOG2E)
        t = x.T.astype(BF16)
        qt_ref[0, 2 * s] = t[0:HEAD_DIM]
        qt_ref[0, 2 * s + 1] = t[HEAD_DIM:]
        col += LANES
    for s in range(hkv // 2):
        x = norm_rope(z_ref[0, :, col:col + LANES], gk_ref[...]).astype(BF16)
        kc_ref[0, 2 * s, 0] = x[:, 0:HEAD_DIM]
        kc_ref[0, 2 * s + 1, 0] = x[:, HEAD_DIM:]
        col += LANES
    for s in range(hkv // 2):
        t = z_ref[0, :, col:col + LANES].T.astype(BF16)
        vt_ref[0, 2 * s, 0] = t[0:HEAD_DIM]
        vt_ref[0, 2 * s + 1, 0] = t[HEAD_DIM:]
        col += LANES
    for s in range(hd // 2):
        x = (z_ref[0, :, col:col + LANES] * ATTN_SCALE).astype(BF16)
        qd_ref[0, 2 * s] = x[:, 0:HEAD_DIM]
        qd_ref[0, 2 * s + 1] = x[:, HEAD_DIM:]
        col += LANES
    for s in range(hd // 2):
        t = z_ref[0, :, col:col + LANES].T.astype(BF16)
        for c in range(ts // dchunk):
            ktd_ref[0, 2 * s, c] = t[0:HEAD_DIM, c * dchunk:(c + 1) * dchunk]
            ktd_ref[0, 2 * s + 1, c] = t[HEAD_DIM:, c * dchunk:(c + 1) * dchunk]
        col += LANES
    for s in range(hd // 2):
        x = z_ref[0, :, col:col + LANES].astype(BF16)
        for c in range(ts // dchunk):
            vd_ref[0, 2 * s, c] = x[c * dchunk:(c + 1) * dchunk, 0:HEAD_DIM]
            vd_ref[0, 2 * s + 1, c] = x[c * dchunk:(c + 1) * dchunk, HEAD_DIM:]
        col += LANES


def _attn_prep(z, cos, sin, q_g, k_g, hq, hkv, hd):
    b, s, n = z.shape
    ts = min(s, 512)
    dchunk = 256
    nt = s // ts
    hdim = HEAD_DIM
    gq = jnp.tile(q_g, 2).reshape(1, LANES)
    gk = jnp.tile(k_g, 2).reshape(1, LANES)
    li = jnp.arange(LANES)
    grp = (li[:, None] // hdim == li[None, :] // hdim).astype(F32)
    kern = functools.partial(_prep_kernel, hq=hq, hkv=hkv, hd=hd, ts=ts, dchunk=dchunk)
    const = lambda shape: pl.BlockSpec(shape, lambda bi, i: (0,) * len(shape))
    return pl.pallas_call(
        kern,
        grid=(b, nt),
        in_specs=[
            pl.BlockSpec((1, ts, n), lambda bi, i: (bi, i, 0)),
            pl.BlockSpec((ts, LANES), lambda bi, i: (i, 0)),
            pl.BlockSpec((ts, LANES), lambda bi, i: (i, 0)),
            const((1, LANES)), const((1, LANES)), const((LANES, LANES)),
        ],
        out_specs=[
            pl.BlockSpec((1, hq, hdim, ts), lambda bi, i: (bi, 0, 0, i)),
            pl.BlockSpec((1, hkv, 1, ts, hdim), lambda bi, i: (bi, 0, i, 0, 0)),
            pl.BlockSpec((1, hkv, 1, hdim, ts), lambda bi, i: (bi, 0, i, 0, 0)),
            pl.BlockSpec((1, hd, ts, hdim), lambda bi, i: (bi, 0, i, 0)),
            pl.BlockSpec((1, hd, ts // dchunk, hdim, dchunk), lambda bi, i: (bi, 0, i, 0, 0)),
            pl.BlockSpec((1, hd, ts // dchunk, dchunk, hdim), lambda bi, i: (bi, 0, i, 0, 0)),
        ],
        out_shape=[
            jax.ShapeDtypeStruct((b, hq, hdim, s), BF16),
            jax.ShapeDtypeStruct((b, hkv, nt, ts, hdim), BF16),
            jax.ShapeDtypeStruct((b, hkv, nt, hdim, ts), BF16),
            jax.ShapeDtypeStruct((b, hd, s, hdim), BF16),
            jax.ShapeDtypeStruct((b, hd, s // dchunk, hdim, dchunk), BF16),
            jax.ShapeDtypeStruct((b, hd, s // dchunk, dchunk, hdim), BF16),
        ],
        compiler_params=_cparams("parallel", "parallel"),
        name="attn_prep",
    )(z, cos, sin, gq, gk, grp)


def _rope_tables(s):
    t = jnp.arange(s)
    row = (t // GRID_W).astype(F32)
    colp = (t % GRID_W).astype(F32)
    axis_dim = HEAD_DIM // 2
    inv = ROPE_THETA ** (-jnp.arange(0, axis_dim, 2, dtype=F32) / axis_dim)
    ar = row[:, None] * inv
    ac = colp[:, None] * inv
    cr, sr, cc, sc = jnp.cos(ar), jnp.sin(ar), jnp.cos(ac), jnp.sin(ac)
    cos = jnp.concatenate([cr, cr, cc, cc], axis=-1)
    sin = jnp.concatenate([-sr, sr, -sc, sc], axis=-1)
    return jnp.tile(cos, (1, 2)), jnp.tile(sin, (1, 2))


def _gqa_kernel(qt_ref, k_ref, vt_ref, kc_ref, vtc_ref, o_ref, *, group):
    n_chunks = k_ref.shape[2]
    for g in range(group):
        qt = qt_ref[0, g]
        st = jnp.dot(kc_ref[0, 0], qt, preferred_element_type=F32)
        m = jnp.max(st, axis=0, keepdims=True)
        p = jnp.exp2(st - m)
        l = jnp.sum(p, axis=0, keepdims=True)
        acc = jnp.dot(vtc_ref[0, 0], p.astype(BF16), preferred_element_type=F32)

        def body(j, carry):
            m, l, acc = carry
            st = jnp.dot(k_ref[0, 0, j], qt, preferred_element_type=F32)
            m_new = jnp.maximum(m, jnp.max(st, axis=0, keepdims=True))
            alpha = jnp.exp2(m - m_new)
            p = jnp.exp2(st - m_new)
            l = alpha * l + jnp.sum(p, axis=0, keepdims=True)
            acc = alpha * acc + jnp.dot(vt_ref[0, 0, j], p.astype(BF16), preferred_element_type=F32)
            return m_new, l, acc

        m, l, acc = lax.fori_loop(0, n_chunks, body, (m, l, acc))
        o_ref[0, g] = (acc / l).astype(BF16)


def _gqa(qt, k, vt, k_ctx, vt_ctx):
    b, hq, hdim, s = qt.shape
    hkv, n_chunks, tk = k.shape[1], k.shape[2], k.shape[3]
    n_ctx = k_ctx.shape[2]
    group = hq // hkv
    tq = min(s, 256)
    return pl.pallas_call(
        functools.partial(_gqa_kernel, group=group),
        grid=(b, hkv, s // tq),
        in_specs=[
            pl.BlockSpec((1, group, hdim, tq), lambda bi, h, i: (bi, h, 0, i)),
            pl.BlockSpec((1, 1, n_chunks, tk, hdim), lambda bi, h, i: (bi, h, 0, 0, 0)),
            pl.BlockSpec((1, 1, n_chunks, hdim, tk), lambda bi, h, i: (bi, h, 0, 0, 0)),
            pl.BlockSpec((1, 1, n_ctx, hdim), lambda bi, h, i: (bi, h, 0, 0)),
            pl.BlockSpec((1, 1, hdim, n_ctx), lambda bi, h, i: (bi, h, 0, 0)),
        ],
        out_specs=pl.BlockSpec((1, group, hdim, tq), lambda bi, h, i: (bi, h, 0, i)),
        out_shape=jax.ShapeDtypeStruct((b, hq, hdim, s), BF16),
        compiler_params=_cparams("parallel", "parallel", "parallel"),
        name="gqa_attention",
    )(qt, k, vt, k_ctx, vt_ctx)


def _ctx_gqa_kernel(qt_ref, k_ref, vt_ref, o_ref):
    st = jnp.dot(k_ref[0, 0], qt_ref[0, 0], preferred_element_type=F32)
    p = jnp.exp2(st - jnp.max(st, axis=0, keepdims=True))
    l = jnp.sum(p, axis=0, keepdims=True)
    acc = jnp.dot(vt_ref[0, 0], p.astype(BF16), preferred_element_type=F32)
    o_ref[0, 0] = (acc / l).astype(BF16)


def _ctx_gqa(qt, k, vt):
    b, hq, hdim, n = qt.shape
    group = hq // k.shape[1]
    return pl.pallas_call(
        _ctx_gqa_kernel,
        grid=(b, hq),
        in_specs=[
            pl.BlockSpec((1, 1, hdim, n), lambda bi, h: (bi, h, 0, 0)),
            pl.BlockSpec((1, 1, n, hdim), lambda bi, h: (bi, h // group, 0, 0)),
            pl.BlockSpec((1, 1, hdim, n), lambda bi, h: (bi, h // group, 0, 0)),
        ],
        out_specs=pl.BlockSpec((1, 1, hdim, n), lambda bi, h: (bi, h, 0, 0)),
        out_shape=jax.ShapeDtypeStruct((b, hq, hdim, n), BF16),
        compiler_params=_cparams("parallel", "parallel"),
        name="ctx_gqa",
    )(qt, k, vt)


def _ctx_mha_kernel(q_ref, kt_ref, v_ref, o_ref):
    s = jnp.dot(q_ref[0, 0], kt_ref[0, 0], preferred_element_type=F32)
    p = jnp.exp(s - jnp.max(s, axis=-1, keepdims=True))
    l = jnp.sum(p, axis=-1, keepdims=True)
    acc = jnp.dot(p.astype(BF16), v_ref[0, 0], preferred_element_type=F32)
    o_ref[0, 0] = (acc / l).astype(BF16)


def _ctx_mha(q, kt, v):
    b, h, n, hdim = q.shape
    return pl.pallas_call(
        _ctx_mha_kernel,
        grid=(b, h),
        in_specs=[
            pl.BlockSpec((1, 1, n, hdim), lambda bi, hi: (bi, hi, 0, 0)),
            pl.BlockSpec((1, 1, hdim, n), lambda bi, hi: (bi, hi, 0, 0)),
            pl.BlockSpec((1, 1, n, hdim), lambda bi, hi: (bi, hi, 0, 0)),
        ],
        out_specs=pl.BlockSpec((1, 1, n, hdim), lambda bi, hi: (bi, hi, 0, 0)),
        out_shape=jax.ShapeDtypeStruct((b, h, n, hdim), BF16),
        compiler_params=_cparams("parallel", "parallel"),
        name="ctx_mha",
    )(q, kt, v)


def _na_kernel(q_ref, kt_ref, v_ref, ktc_ref, vc_ref, bias_ref, o_ref, *, n_band):
    i = pl.program_id(2)
    nq = pl.num_programs(2)
    c0 = jnp.clip(i - 1, 0, nq - n_band)
    q = q_ref[0, 0]
    s_ctx = jnp.dot(q, ktc_ref[0, 0], preferred_element_type=F32)
    m = jnp.max(s_ctx, axis=-1, keepdims=True)
    s_lat = []
    for c in range(n_band):
        cw = kt_ref.shape[4]
        sl = jnp.dot(q, kt_ref[0, 0, c0 + c], preferred_element_type=F32)
        sl = sl + bias_ref[0, 0, :, c * cw:(c + 1) * cw]
        m = jnp.maximum(m, jnp.max(sl, axis=-1, keepdims=True))
        s_lat.append(sl)
    p = jnp.exp(s_ctx - m)
    l = jnp.sum(p, axis=-1, keepdims=True)
    acc = jnp.dot(p.astype(BF16), vc_ref[0, 0], preferred_element_type=F32)
    for c in range(n_band):
        p = jnp.exp(s_lat[c] - m)
        l = l + jnp.sum(p, axis=-1, keepdims=True)
        acc = acc + jnp.dot(p.astype(BF16), v_ref[0, 0, c0 + c], preferred_element_type=F32)
    o_ref[0, 0] = (acc / l).astype(BF16)


def _na_bias_table(rel_bias, rows):
    w = GRID_W
    qcol = jnp.arange(w)
    cstart = jnp.clip(qcol - NA_WIN_COLS // 2, 0, w - NA_WIN_COLS)
    in_win = (qcol[None, :] >= cstart[:, None]) & (qcol[None, :] < cstart[:, None] + NA_WIN_COLS)
    col_idx = jnp.clip(qcol[None, :] - qcol[:, None] + NA_WIN_COLS - 1, 0, 2 * NA_WIN_COLS - 2)
    a = jnp.arange(NA_Q_ROWS)[:, None]
    c = jnp.arange(NA_BAND_ROWS)[None, :]
    tables = []
    for r0, rs0 in ((0, 0), (NA_Q_ROWS, 0), (rows - NA_Q_ROWS, rows - NA_BAND_ROWS)):
        r = r0 + a
        kr = rs0 + c
        rs = jnp.clip(r - NA_WIN_ROWS // 2, 0, rows - NA_WIN_ROWS)
        valid = (kr >= rs) & (kr < rs + NA_WIN_ROWS)
        row_idx = jnp.clip(kr - r + NA_WIN_ROWS - 1, 0, 2 * NA_WIN_ROWS - 2)
        bias = rel_bias[:, row_idx[:, :, None, None], col_idx[None, None, :, :]]
        ok = valid[:, :, None, None] & in_win[None, None, :, :]
        bias = jnp.where(ok[None], bias, NEG_BIG)
        h = rel_bias.shape[0]
        tables.append(bias.transpose(0, 1, 3, 2, 4).reshape(h, NA_Q_ROWS * w, NA_BAND_ROWS * w))
    return jnp.stack(tables, axis=1).astype(F32)


def _na(q, kt, v, kt_ctx, v_ctx, bias):
    b, h, s, hdim = q.shape
    n_chunks, cw = kt.shape[2], kt.shape[4]
    n_ctx = v_ctx.shape[2]
    tq = NA_Q_ROWS * GRID_W
    nq = s // tq
    n_band = NA_BAND_ROWS * GRID_W // cw
    band_w = NA_BAND_ROWS * GRID_W
    return pl.pallas_call(
        functools.partial(_na_kernel, n_band=n_band),
        grid=(b, h, nq),
        in_specs=[
            pl.BlockSpec((1, 1, tq, hdim), lambda bi, hi, i: (bi, hi, i, 0)),
            pl.BlockSpec((1, 1, n_chunks, hdim, cw), lambda bi, hi, i: (bi, hi, 0, 0, 0)),
            pl.BlockSpec((1, 1, n_chunks, cw, hdim), lambda bi, hi, i: (bi, hi, 0, 0, 0)),
            pl.BlockSpec((1, 1, hdim, n_ctx), lambda bi, hi, i: (bi, hi, 0, 0)),
            pl.BlockSpec((1, 1, n_ctx, hdim), lambda bi, hi, i: (bi, hi, 0, 0)),
            pl.BlockSpec((1, 1, tq, band_w),
                         lambda bi, hi, i: (hi, jnp.where(i == 0, 0, jnp.where(i == nq - 1, 2, 1)), 0, 0)),
        ],
        out_specs=pl.BlockSpec((1, 1, tq, hdim), lambda bi, hi, i: (bi, hi, i, 0)),
        out_shape=jax.ShapeDtypeStruct((b, h, s, hdim), BF16),
        compiler_params=_cparams("parallel", "parallel", "parallel"),
        name="na_attention",
    )(q, kt, v, kt_ctx, v_ctx, bias)


def kernel(x, c, ctx, c_ctx, ada_w, ada_b, ln1_g, ln1_b, ln2_g, ln2_b, conv_w_in, conv_w_out, conv_a_dw_w, conv_a_dw_b, conv_a_ln_g, conv_a_ln_b, conv_b_dw_w, attn_w_in, attn_w_out, q_norm_g, k_norm_g, na_rel_bias, router_w1, router_b1, router_w2, router_b2, expert_w_gate, expert_w_up, expert_w_down):
    bsz, seq, d = x.shape
    n_ctx = ctx.shape[1]
    depth = ada_w.shape[0]
    alpha = (2 * depth) ** 0.25
    hq = (d // 2) // HEAD_DIM
    hkv = hq // 4
    hd = (d // 2) // HEAD_DIM

    cond8 = jnp.zeros((8, d), F32).at[:bsz].set(c).at[bsz].set(c_ctx)
    mods = _ada(cond8, ada_w, ada_b)

    def lat_mod(i, k):
        return mods[i, :bsz, k * d:(k + 1) * d].reshape(bsz, 1, d)

    def ctx_mod(i, k):
        return jnp.broadcast_to(mods[i, bsz, k * d:(k + 1) * d].reshape(1, 1, d), (bsz, 1, d))

    wr = jnp.zeros((depth, d, ROUTER_COLS), F32)
    wr = wr.at[:, :, :N_GROUPS].set(router_w1).at[:, :, N_GROUPS:N_GROUPS + N_EXPERTS].set(router_w2)
    br = jnp.zeros((depth, 1, ROUTER_COLS), F32)
    br = br.at[:, 0, :N_GROUPS].set(router_b1).at[:, 0, N_GROUPS:N_GROUPS + N_EXPERTS].set(router_b2)

    cos_lat, sin_lat = _rope_tables(seq)
    cos_ctx = jnp.ones((n_ctx, LANES), F32)
    sin_ctx = jnp.zeros((n_ctx, LANES), F32)

    x_lat, x_ctx = x, ctx
    for i in range(depth):
        last = i == depth - 1
        odd = i % 2 == 1
        j = i // 2
        use_ctx = odd or not last
        if odd:
            w_in = attn_w_in[j].astype(BF16)
            w_out = attn_w_out[j].astype(BF16)
            z_lat = _mod_matmul(x_lat, lat_mod(i, 1), lat_mod(i, 0), w_in)
            z_ctx = _mod_matmul(x_ctx, ctx_mod(i, 1), ctx_mod(i, 0), w_in)
            qt, kc, vtc, qd, ktd, vd = _attn_prep(z_lat, cos_lat, sin_lat, q_norm_g[j], k_norm_g[j], hq, hkv, hd)
            cqt, ckc, cvtc, cqd, cktd, cvd = _attn_prep(z_ctx, cos_ctx, sin_ctx, q_norm_g[j], k_norm_g[j],
                                                        hq, hkv, hd)
            ckc = ckc.reshape(bsz, hkv, n_ctx, HEAD_DIM)
            cvtc = cvtc.reshape(bsz, hkv, HEAD_DIM, n_ctx)
            cktd = cktd.reshape(bsz, hd, HEAD_DIM, n_ctx)
            cvd = cvd.reshape(bsz, hd, n_ctx, HEAD_DIM)
            ot_c = _gqa(qt, kc, vtc, ckc, cvtc)
            bias = _na_bias_table(na_rel_bias[j], seq // GRID_W)
            o_d = _na(qd, ktd, vd, cktd, cvd, bias)
            mix_lat = (ot_c, o_d)
            if not last:
                mix_ctx = (_ctx_gqa(cqt, ckc, cvtc), _ctx_mha(cqd, cktd, cvd))
        else:
            w_in = conv_w_in[j].astype(BF16)
            w_out = conv_w_out[j].astype(BF16)
            conv_w = (conv_a_dw_w[j], conv_a_dw_b[j], conv_a_ln_g[j], conv_a_ln_b[j], conv_b_dw_w[j])
            z_lat = _mod_matmul(x_lat, lat_mod(i, 1), lat_mod(i, 0), w_in)
            mix_lat = _conv_mixers(z_lat, *conv_w)
            if not last:
                z_ctx = _mod_matmul(x_ctx, ctx_mod(i, 1), ctx_mod(i, 0), w_in)
                mix_ctx = _conv_mixers(z_ctx, *conv_w)

        x_lat, h_lat, lg_lat = _proj_ln(mix_lat, w_out, x_lat, lat_mod(i, 2), ln1_g[i], ln1_b[i],
                                        lat_mod(i, 4), lat_mod(i, 3), wr[i], br[i], alpha)
        moe_w = (expert_w_gate[i].astype(BF16), expert_w_up[i].astype(BF16), expert_w_down[i].astype(BF16))
        n_lat = bsz * seq
        if last:
            y01, gates = _moe(h_lat.reshape(n_lat, d), lg_lat.reshape(n_lat, ROUTER_COLS), *moe_w)
            x_lat = _moe_ln(x_lat, y01, gates, 0, lat_mod(i, 5), ln2_g[i], ln2_b[i], alpha)
        else:
            x_ctx, h_ctx, lg_ctx = _proj_ln(mix_ctx, w_out, x_ctx, ctx_mod(i, 2), ln1_g[i], ln1_b[i],
                                            ctx_mod(i, 4), ctx_mod(i, 3), wr[i], br[i], alpha)
            h_all = jnp.concatenate([h_lat.reshape(n_lat, d), h_ctx.reshape(bsz * n_ctx, d)], axis=0)
            lg_all = jnp.concatenate([lg_lat.reshape(n_lat, ROUTER_COLS),
                                      lg_ctx.reshape(bsz * n_ctx, ROUTER_COLS)], axis=0)
            y01, gates = _moe(h_all, lg_all, *moe_w)
            x_lat = _moe_ln(x_lat, y01, gates, 0, lat_mod(i, 5), ln2_g[i], ln2_b[i], alpha)
            x_ctx = _moe_ln(x_ctx, y01, gates, n_lat, ctx_mod(i, 5), ln2_g[i], ln2_b[i], alpha)
    return x_lat
```

```python
import functools
import math

import jax
import jax.numpy as jnp
import numpy as np
from jax import lax
from jax.experimental import pallas as pl
from jax.experimental.pallas import tpu as pltpu

F32 = jnp.float32
BF16 = jnp.bfloat16
HIGHEST = lax.Precision.HIGHEST

GRID_W = 64
HEAD_DIM = 64
ATTN_SCALE = HEAD_DIM ** -0.5
LOG2E = math.log2(math.e)
CONV_A_WIDTH = 31
CONV_B_WIDTH = 3
CONV_HALO = 16
NA_WIN_ROWS = 8
NA_WIN_COLS = 16
NA_Q_ROWS = 4
NA_BAND_ROWS = 12
ROPE_THETA = 10000.0
N_GROUPS = 4
EXPERTS_PER_GROUP = 8
N_EXPERTS = N_GROUPS * EXPERTS_PER_GROUP
LN_EPS = 1e-5
RMS_EPS = 1e-6
NEG_BIG = -1e30

VMEM_LIMIT_BYTES = 48 * 1024 * 1024
LANES = 128
ROUTER_COLS = LANES
EXPERT_BLOCK = 256
RANK_BLOCK = 512


def _cparams(*sem):
    return pltpu.CompilerParams(dimension_semantics=sem, vmem_limit_bytes=VMEM_LIMIT_BYTES)


def _layer_norm(x, g, b):
    mu = jnp.mean(x, axis=-1, keepdims=True)
    xc = x - mu
    var = jnp.mean(xc * xc, axis=-1, keepdims=True)
    return xc * lax.rsqrt(var + LN_EPS) * g + b


def _ada_kernel(c_ref, w_ref, b_ref, o_ref):
    c = c_ref[...]
    s = c * jax.nn.sigmoid(c)
    o_ref[0] = jnp.dot(s, w_ref[0], preferred_element_type=F32, precision=HIGHEST) + b_ref[0]


def _ada(cond8, ada_w, ada_b):
    n_layers, d, n = ada_w.shape
    tn = 1536
    return pl.pallas_call(
        _ada_kernel,
        grid=(n_layers, n // tn),
        in_specs=[
            pl.BlockSpec((8, d), lambda l, j: (0, 0)),
            pl.BlockSpec((1, d, tn), lambda l, j: (l, 0, j)),
            pl.BlockSpec((1, 1, tn), lambda l, j: (l, 0, j)),
        ],
        out_specs=pl.BlockSpec((1, 8, tn), lambda l, j: (l, 0, j)),
        out_shape=jax.ShapeDtypeStruct((n_layers, 8, n), F32),
        compiler_params=_cparams("parallel", "parallel"),
        name="ada_mod",
    )(cond8, ada_w, ada_b.reshape(n_layers, 1, n))


def _mod_matmul_kernel(x_ref, sc_ref, sh_ref, w_ref, o_ref):
    h = x_ref[0] * (1.0 + sc_ref[0]) + sh_ref[0]
    o_ref[0] = jnp.dot(h.astype(BF16), w_ref[...], preferred_element_type=F32)


def _mod_matmul(x, sc, sh, w):
    b, s, d = x.shape
    n = w.shape[1]
    tm = min(s, 512)
    return pl.pallas_call(
        _mod_matmul_kernel,
        grid=(b, s // tm),
        in_specs=[
            pl.BlockSpec((1, tm, d), lambda bi, i: (bi, i, 0)),
            pl.BlockSpec((1, 1, d), lambda bi, i: (bi, 0, 0)),
            pl.BlockSpec((1, 1, d), lambda bi, i: (bi, 0, 0)),
            pl.BlockSpec((d, n), lambda bi, i: (0, 0)),
        ],
        out_specs=pl.BlockSpec((1, tm, n), lambda bi, i: (bi, i, 0)),
        out_shape=jax.ShapeDtypeStruct((b, s, n), F32),
        compiler_params=_cparams("parallel", "parallel"),
        name="mod_matmul",
    )(x, sc, sh, w)


def _conv_kernel(z_ref, zp_ref, zn_ref, aw_ref, ab_ref, ag_ref, abb_ref, bw_ref,
                 o_ref, ea_ref, eb_ref, *, tile, ch):
    i = pl.program_id(1)
    n = pl.num_programs(1)
    halo = CONV_HALO

    def glu(zz):
        return zz[:, 0:ch] * jax.nn.sigmoid(zz[:, ch:2 * ch])

    def cv(zz):
        return zz[:, 3 * ch:4 * ch] * zz[:, 4 * ch:5 * ch]

    pmask = (i > 0).astype(F32)
    nmask = (i < n - 1).astype(F32)
    zp = zp_ref[0]
    zn = zn_ref[0]
    ea_ref[0:halo, :] = glu(zp) * pmask
    eb_ref[0:halo, :] = cv(zp) * pmask
    ea_ref[halo + tile:2 * halo + tile, :] = glu(zn) * nmask
    eb_ref[halo + tile:2 * halo + tile, :] = cv(zn) * nmask
    ea_ref[halo:halo + tile, :] = glu(z_ref[0])
    eb_ref[halo:halo + tile, :] = cv(z_ref[0])

    rows = 32
    pad_a = CONV_A_WIDTH // 2
    pad_b = CONV_B_WIDTH // 2
    for r0 in range(0, tile, rows):
        acc = jnp.zeros((rows, ch), F32) + ab_ref[...]
        for k in range(CONV_A_WIDTH):
            acc = acc + aw_ref[k:k + 1, :] * ea_ref[pl.ds(halo + r0 + k - pad_a, rows), :]
        ya = _layer_norm(acc, ag_ref[...], abb_ref[...])
        ya = ya * jax.nn.sigmoid(ya)
        cb = jnp.zeros((rows, ch), F32)
        for k in range(CONV_B_WIDTH):
            cb = cb + bw_ref[k:k + 1, :] * eb_ref[pl.ds(halo + r0 + k - pad_b, rows), :]
        yb = z_ref[0, r0:r0 + rows, 2 * ch:3 * ch] * cb
        o_ref[0, r0:r0 + rows, 0:ch] = ya.astype(BF16)
        o_ref[0, r0:r0 + rows, ch:2 * ch] = yb.astype(BF16)


def _conv_mixers(z, a_dw_w, a_dw_b, a_ln_g, a_ln_b, b_dw_w):
    b, s, n = z.shape
    ch = n // 5
    tile = min(s, 256)
    hb = tile // CONV_HALO
    last_hb = s // CONV_HALO - 1
    kern = functools.partial(_conv_kernel, tile=tile, ch=ch)
    vec = lambda v: v.reshape(1, ch)
    return pl.pallas_call(
        kern,
        grid=(b, s // tile),
        in_specs=[
            pl.BlockSpec((1, tile, n), lambda bi, i: (bi, i, 0)),
            pl.BlockSpec((1, CONV_HALO, n), lambda bi, i: (bi, jnp.maximum(i * hb - 1, 0), 0)),
            pl.BlockSpec((1, CONV_HALO, n), lambda bi, i: (bi, jnp.minimum((i + 1) * hb, last_hb), 0)),
            pl.BlockSpec((CONV_A_WIDTH, ch), lambda bi, i: (0, 0)),
            pl.BlockSpec((1, ch), lambda bi, i: (0, 0)),
            pl.BlockSpec((1, ch), lambda bi, i: (0, 0)),
            pl.BlockSpec((1, ch), lambda bi, i: (0, 0)),
            pl.BlockSpec((CONV_B_WIDTH, ch), lambda bi, i: (0, 0)),
        ],
        out_specs=pl.BlockSpec((1, tile, 2 * ch), lambda bi, i: (bi, i, 0)),
        out_shape=jax.ShapeDtypeStruct((b, s, 2 * ch), BF16),
        scratch_shapes=[pltpu.VMEM((tile + 2 * CONV_HALO, ch), F32),
                        pltpu.VMEM((tile + 2 * CONV_HALO, ch), F32)],
        compiler_params=_cparams("parallel", "parallel"),
        name="conv_mixers",
    )(z, z, z, a_dw_w, vec(a_dw_b), vec(a_ln_g), vec(a_ln_b), b_dw_w)


def _proj_ln_kernel(*refs, heads, alpha):
    if heads:
        ot_ref, od_ref = refs[:2]
        refs = refs[2:]
    else:
        u_ref = refs[0]
        refs = refs[1:]
    (w_ref, x_ref, g_ref, lng_ref, lnb_ref, sc_ref, sh_ref, wr_ref, br_ref,
     xo_ref, h_ref, lg_ref) = refs
    if heads:
        n_heads = ot_ref.shape[1]
        y = None
        for h in range(n_heads):
            t = lax.dot_general(ot_ref[0, h], w_ref[h * HEAD_DIM:(h + 1) * HEAD_DIM, :],
                                (((0,), (0,)), ((), ())), preferred_element_type=F32)
            y = t if y is None else y + t
        off = n_heads * HEAD_DIM
        for h in range(od_ref.shape[1]):
            y = y + jnp.dot(od_ref[0, h], w_ref[off + h * HEAD_DIM:off + (h + 1) * HEAD_DIM, :],
                            preferred_element_type=F32)
    else:
        y = jnp.dot(u_ref[0], w_ref[...], preferred_element_type=F32)
    xn = _layer_norm(alpha * x_ref[0] + g_ref[0] * y, lng_ref[...], lnb_ref[...])
    xo_ref[0] = xn
    h2 = xn * (1.0 + sc_ref[0]) + sh_ref[0]
    h_ref[0] = h2
    lg_ref[0] = jnp.dot(h2, wr_ref[...], preferred_element_type=F32, precision=HIGHEST) + br_ref[...]


def _proj_ln(mix, w_out, x, g, ln_g, ln_b, sc2, sh2, wr, br, alpha):
    b, s, d = x.shape
    tm = min(s, 512)
    heads = isinstance(mix, tuple)
    if heads:
        ot, od = mix
        mix_specs = [
            pl.BlockSpec((1, ot.shape[1], HEAD_DIM, tm), lambda bi, i: (bi, 0, 0, i)),
            pl.BlockSpec((1, od.shape[1], tm, HEAD_DIM), lambda bi, i: (bi, 0, i, 0)),
        ]
        mix_args = [ot, od]
    else:
        mix_specs = [pl.BlockSpec((1, tm, mix.shape[2]), lambda bi, i: (bi, i, 0))]
        mix_args = [mix]
    row = lambda: pl.BlockSpec((1, tm, d), lambda bi, i: (bi, i, 0))
    per_b = lambda: pl.BlockSpec((1, 1, d), lambda bi, i: (bi, 0, 0))
    vec = lambda: pl.BlockSpec((1, d), lambda bi, i: (0, 0))
    kern = functools.partial(_proj_ln_kernel, heads=heads, alpha=alpha)
    return pl.pallas_call(
        kern,
        grid=(b, s // tm),
        in_specs=mix_specs + [
            pl.BlockSpec(w_out.shape, lambda bi, i: (0, 0)),
            row(), per_b(), vec(), vec(), per_b(), per_b(),
            pl.BlockSpec((d, ROUTER_COLS), lambda bi, i: (0, 0)),
            pl.BlockSpec((1, ROUTER_COLS), lambda bi, i: (0, 0)),
        ],
        out_specs=[row(), row(), pl.BlockSpec((1, tm, ROUTER_COLS), lambda bi, i: (bi, i, 0))],
        out_shape=[jax.ShapeDtypeStruct((b, s, d), F32),
                   jax.ShapeDtypeStruct((b, s, d), F32),
                   jax.ShapeDtypeStruct((b, s, ROUTER_COLS), F32)],
        compiler_params=_cparams("parallel", "parallel"),
        name="proj_res_ln",
    )(*mix_args, w_out, x, g, ln_g.reshape(1, d), ln_b.reshape(1, d), sc2, sh2, wr, br)


def _rank_kernel(e_ref, tri_ref, rank_ref, cnt_ref, carry_ref):
    @pl.when(pl.program_id(0) == 0)
    def _():
        carry_ref[...] = jnp.zeros_like(carry_ref)

    e = e_ref[0]
    blk = e.shape[1]
    ids = lax.broadcasted_iota(jnp.int32, (N_EXPERTS, blk), 0)
    oh = jnp.where(ids == e, 1.0, 0.0).astype(F32)
    before = jnp.dot(oh.astype(BF16), tri_ref[...], preferred_element_type=F32)
    carry = carry_ref[...]
    before = before + carry[:, 0:1]
    rank_ref[0] = jnp.sum(oh * before, axis=0, keepdims=True).astype(jnp.int32)
    carry = carry + jnp.sum(oh, axis=1, keepdims=True)
    carry_ref[...] = carry
    cnt_ref[...] = carry.astype(jnp.int32)


def _expert_ranks(flat_e):
    m = flat_e.shape[0]
    blk = RANK_BLOCK
    nblk = m // blk
    tri = (lax.broadcasted_iota(jnp.int32, (blk, blk), 0)
           < lax.broadcasted_iota(jnp.int32, (blk, blk), 1)).astype(BF16)
    rank, cnt = pl.pallas_call(
        _rank_kernel,
        grid=(nblk,),
        in_specs=[pl.BlockSpec((1, 1, blk), lambda i: (i, 0, 0)),
                  pl.BlockSpec((blk, blk), lambda i: (0, 0))],
        out_specs=[pl.BlockSpec((1, 1, blk), lambda i: (i, 0, 0)),
                   pl.BlockSpec((N_EXPERTS, LANES), lambda i: (0, 0))],
        out_shape=[jax.ShapeDtypeStruct((nblk, 1, blk), jnp.int32),
                   jax.ShapeDtypeStruct((N_EXPERTS, LANES), jnp.int32)],
        scratch_shapes=[pltpu.VMEM((N_EXPERTS, LANES), F32)],
        compiler_params=_cparams("arbitrary"),
        name="expert_rank",
    )(flat_e.reshape(nblk, 1, blk), tri)
    return rank.reshape(m), cnt[:, 0]


def _to_row_tiles(ref, x, n_rows, tiles):
    for s in range(tiles):
        ref[pl.ds(s, n_rows, stride=tiles), :] = x[:, s * LANES:(s + 1) * LANES]


def _from_row_tiles(ref, n_rows, tiles):
    return jnp.concatenate([ref[pl.ds(s, n_rows, stride=tiles), :] for s in range(tiles)], axis=1)


def _dispatch_kernel(dest_ref, x_ref, init_ref, xs_ref, buf_ref, sem, *, tm, tiles):
    del init_ref
    _to_row_tiles(buf_ref, x_ref[...], tm, tiles)

    def issue(r, c):
        src = buf_ref.at[pl.ds(pl.multiple_of(r * tiles, tiles), tiles), :]
        for k in range(2):
            d = pl.multiple_of(dest_ref[0, 0, 2 * r + k] * tiles, tiles)
            pltpu.make_async_copy(src, xs_ref.at[pl.ds(d, tiles), :], sem).start()
        return c

    lax.fori_loop(0, tm, issue, 0, unroll=4)
    for _ in range(2):
        pltpu.make_async_copy(buf_ref, xs_ref.at[pl.ds(0, tm * tiles), :], sem).wait()


def _dispatch(h_all, dest, cap):
    n, d = h_all.shape
    tiles = d // LANES
    tm = 512
    nt = n // tm
    return pl.pallas_call(
        functools.partial(_dispatch_kernel, tm=tm, tiles=tiles),
        grid=(nt,),
        in_specs=[
            pl.BlockSpec((1, 1, 2 * tm), lambda i: (i, 0, 0), memory_space=pltpu.SMEM),
            pl.BlockSpec((tm, d), lambda i: (i, 0)),
            pl.BlockSpec(memory_space=pl.ANY),
        ],
        out_specs=pl.BlockSpec(memory_space=pl.ANY),
        out_shape=jax.ShapeDtypeStruct((cap * tiles, LANES), F32),
        scratch_shapes=[pltpu.VMEM((tm * tiles, LANES), F32), pltpu.SemaphoreType.DMA(())],
        input_output_aliases={2: 0},
        compiler_params=_cparams("arbitrary"),
        name="moe_dispatch",
    )(dest.reshape(nt, 1, 2 * tm), h_all, jnp.zeros((cap * tiles, LANES), F32))


def _experts_kernel(be_ref, nu_ref, x_ref, wg_ref, wu_ref, wd_ref, o_ref, wgb_ref, wub_ref, wdb_ref,
                    *, tb, tiles):
    i = pl.program_id(0)
    changed = jnp.logical_or(i == 0, be_ref[i] != be_ref[jnp.maximum(i - 1, 0)])

    @pl.when(changed)
    def _():
        wgb_ref[...] = wg_ref[0].astype(BF16)
        wub_ref[...] = wu_ref[0].astype(BF16)
        wdb_ref[...] = wd_ref[0].astype(BF16)

    @pl.when(i < nu_ref[0])
    def _():
        x = _from_row_tiles(x_ref, tb, tiles).astype(BF16)
        g = jnp.dot(x, wgb_ref[...], preferred_element_type=F32)
        u = jnp.dot(x, wub_ref[...], preferred_element_type=F32)
        a = (g * jax.nn.sigmoid(g) * u).astype(BF16)
        _to_row_tiles(o_ref, jnp.dot(a, wdb_ref[...], preferred_element_type=F32), tb, tiles)

    @pl.when(i >= nu_ref[0])
    def _():
        o_ref[...] = jnp.zeros_like(o_ref)


def _experts(xs, block_e, n_used, w_gate, w_up, w_down):
    d, hid = w_gate.shape[1], w_gate.shape[2]
    tiles = d // LANES
    tb = EXPERT_BLOCK
    nb = xs.shape[0] // (tb * tiles)
    grid_spec = pltpu.PrefetchScalarGridSpec(
        num_scalar_prefetch=2,
        grid=(nb,),
        in_specs=[
            pl.BlockSpec((tb * tiles, LANES), lambda i, be, nu: (i, 0)),
            pl.BlockSpec((1, d, hid), lambda i, be, nu: (be[i], 0, 0)),
            pl.BlockSpec((1, d, hid), lambda i, be, nu: (be[i], 0, 0)),
            pl.BlockSpec((1, hid, d), lambda i, be, nu: (be[i], 0, 0)),
        ],
        out_specs=pl.BlockSpec((tb * tiles, LANES), lambda i, be, nu: (i, 0)),
        scratch_shapes=[pltpu.VMEM((d, hid), BF16), pltpu.VMEM((d, hid), BF16), pltpu.VMEM((hid, d), BF16)],
    )
    return pl.pallas_call(
        functools.partial(_experts_kernel, tb=tb, tiles=tiles),
        grid_spec=grid_spec,
        out_shape=jax.ShapeDtypeStruct(xs.shape, F32),
        compiler_params=_cparams("arbitrary"),
        name="experts",
    )(block_e, n_used, xs, w_gate, w_up, w_down)


def _moe_ln_kernel(dest_ref, x_ref, gt_ref, g_ref, lng_ref, lnb_ref, ys_ref, o_ref,
                   y0_ref, y1_ref, sem0, sem1, *, tm, tiles, alpha):
    def issue(r, c):
        dst = pl.ds(pl.multiple_of(r * tiles, tiles), tiles)
        d0 = pl.multiple_of(dest_ref[0, 0, 2 * r] * tiles, tiles)
        d1 = pl.multiple_of(dest_ref[0, 0, 2 * r + 1] * tiles, tiles)
        pltpu.make_async_copy(ys_ref.at[pl.ds(d0, tiles), :], y0_ref.at[dst, :], sem0).start()
        pltpu.make_async_copy(ys_ref.at[pl.ds(d1, tiles), :], y1_ref.at[dst, :], sem1).start()
        return c

    lax.fori_loop(0, tm, issue, 0, unroll=4)
    pltpu.make_async_copy(ys_ref.at[pl.ds(0, tm * tiles), :], y0_ref, sem0).wait()
    pltpu.make_async_copy(ys_ref.at[pl.ds(0, tm * tiles), :], y1_ref, sem1).wait()
    gt = gt_ref[...]
    y = gt[:, 0:1] * _from_row_tiles(y0_ref, tm, tiles) + gt[:, 1:2] * _from_row_tiles(y1_ref, tm, tiles)
    o_ref[...] = _layer_norm(alpha * x_ref[...] + g_ref[0] * y, lng_ref[...], lnb_ref[...])


def _moe_ln(x, ys, dest, gates, tok_off, g, ln_g, ln_b, alpha):
    b, s, d = x.shape
    tiles = d // LANES
    n_tok = gates.shape[0]
    tm = min(s, 512)
    nt = s // tm
    off = tok_off // tm
    vec = lambda: pl.BlockSpec((1, d), lambda bi, i: (0, 0))
    out = pl.pallas_call(
        functools.partial(_moe_ln_kernel, tm=tm, tiles=tiles, alpha=alpha),
        grid=(b, nt),
        in_specs=[
            pl.BlockSpec((1, 1, 2 * tm), lambda bi, i: (off + bi * nt + i, 0, 0), memory_space=pltpu.SMEM),
            pl.BlockSpec((tm, d), lambda bi, i: (bi * nt + i, 0)),
            pl.BlockSpec((tm, 2), lambda bi, i: (off + bi * nt + i, 0)),
            pl.BlockSpec((1, 1, d), lambda bi, i: (bi, 0, 0)),
            vec(), vec(),
            pl.BlockSpec(memory_space=pl.ANY),
        ],
        out_specs=pl.BlockSpec((tm, d), lambda bi, i: (bi * nt + i, 0)),
        out_shape=jax.ShapeDtypeStruct((b * s, d), F32),
        scratch_shapes=[pltpu.VMEM((tm * tiles, LANES), F32), pltpu.VMEM((tm * tiles, LANES), F32),
                        pltpu.SemaphoreType.DMA(()), pltpu.SemaphoreType.DMA(())],
        compiler_params=_cparams("arbitrary", "arbitrary"),
        name="moe_res_ln",
    )(dest.reshape(n_tok // tm, 1, 2 * tm), x.reshape(b * s, d), gates, g,
      ln_g.reshape(1, d), ln_b.reshape(1, d), ys)
    return out.reshape(b, s, d)


def _route(logits):
    n = logits.shape[0]
    l1 = logits[:, :N_GROUPS]
    grp = jnp.argmax(l1, axis=-1).astype(jnp.int32)
    p_grp = 1.0 / jnp.sum(jnp.exp(l1 - jnp.max(l1, axis=-1, keepdims=True)), axis=-1)
    lg2 = logits[:, N_GROUPS:N_GROUPS + N_EXPERTS].reshape(n, N_GROUPS, EXPERTS_PER_GROUP)
    gsel = grp[:, None] == jnp.arange(N_GROUPS, dtype=jnp.int32)[None, :]
    sel = jnp.sum(jnp.where(gsel[:, :, None], lg2, 0.0), axis=1)
    i1 = jnp.argmax(sel, axis=-1).astype(jnp.int32)
    v1 = jnp.max(sel, axis=-1)
    col = jnp.arange(EXPERTS_PER_GROUP, dtype=jnp.int32)[None, :]
    sel2 = jnp.where(col == i1[:, None], -jnp.inf, sel)
    i2 = jnp.argmax(sel2, axis=-1).astype(jnp.int32)
    v2 = jnp.max(sel2, axis=-1)
    e2 = jnp.exp(v2 - v1)
    w1 = 1.0 / (1.0 + e2)
    gates = p_grp[:, None] * jnp.stack([w1, e2 * w1], axis=-1)
    experts = grp[:, None] * EXPERTS_PER_GROUP + jnp.stack([i1, i2], axis=-1)
    return experts, gates.astype(F32)


def _moe(h_all, logits, w_gate, w_up, w_down):
    n, d = h_all.shape
    m = 2 * n
    tb = EXPERT_BLOCK
    experts, gates = _route(logits)
    flat_e = experts.reshape(m)
    rank, counts = _expert_ranks(flat_e)
    padded = (counts + tb - 1) // tb * tb
    pad_end = jnp.cumsum(padded)
    pad_start = pad_end - padded
    onehot = flat_e[:, None] == jnp.arange(N_EXPERTS, dtype=jnp.int32)[None, :]
    dest = jnp.sum(jnp.where(onehot, pad_start[None, :], 0), axis=-1).astype(jnp.int32) + rank
    nb = -(-m // tb) + N_EXPERTS
    cap = nb * tb
    blk_start = jnp.arange(nb, dtype=jnp.int32) * tb
    block_e = jnp.minimum(jnp.sum(pad_end[None, :] <= blk_start[:, None], axis=-1), N_EXPERTS - 1)
    n_used = (pad_end[-1] // tb).reshape(1)

    xs = _dispatch(h_all, dest, cap)
    ys = _experts(xs, block_e.astype(jnp.int32), n_used.astype(jnp.int32), w_gate, w_up, w_down)
    return ys, dest, gates


def _prep_kernel(z_ref, cos_ref, sin_ref, gq_ref, gk_ref, grp_ref,
                 qt_ref, kc_ref, vt_ref, qd_ref, ktd_ref, vd_ref, *, hq, hkv, hd, ts, dchunk):
    cos = cos_ref[...]
    sin = sin_ref[...]
    lane = lax.broadcasted_iota(jnp.int32, (ts, LANES), 1)
    first_half = (lane % 32) < 16

    def norm_rope(x, g):
        ss = jnp.dot(x * x, grp_ref[...], preferred_element_type=F32, precision=HIGHEST)
        xn = x * lax.rsqrt(ss * (1.0 / HEAD_DIM) + RMS_EPS) * g
        partner = jnp.where(first_half, pltpu.roll(xn, LANES - 16, 1), pltpu.roll(xn, 16, 1))
        return xn * cos + partner * sin

    col = 0
    for s in range(hq // 2):
        x = norm_rope(z_ref[0, :, col:col + LANES], gq_ref[...]) * (ATTN_SCALE * LOG2E)
        t = x.T.astype(BF16)
        qt_ref[0, 2 * s] = t[0:HEAD_DIM]
        qt_ref[0, 2 * s + 1] = t[HEAD_DIM:]
        col += LANES
    for s in range(hkv // 2):
        x = norm_rope(z_ref[0, :, col:col + LANES], gk_ref[...]).astype(BF16)
        kc_ref[0, 2 * s, 0] = x[:, 0:HEAD_DIM]
        kc_ref[0, 2 * s + 1, 0] = x[:, HEAD_DIM:]
        col += LANES
    for s in range(hkv // 2):
        t = z_ref[0, :, col:col + LANES].T.astype(BF16)
        vt_ref[0, 2 * s, 0] = t[0:HEAD_DIM]
        vt_ref[0, 2 * s + 1, 0] = t[HEAD_DIM:]
        col += LANES
    for s in range(hd // 2):
        x = (z_ref[0, :, col:col + LANES] * ATTN_SCALE).astype(BF16)
        qd_ref[0, 2 * s] = x[:, 0:HEAD_DIM]
        qd_ref[0, 2 * s + 1] = x[:, HEAD_DIM:]
        col += LANES
    for s in range(hd // 2):
        t = z_ref[0, :, col:col + LANES].T.astype(BF16)
        for c in range(ts // dchunk):
            ktd_ref[0, 2 * s, c] = t[0:HEAD_DIM, c * dchunk:(c + 1) * dchunk]
            ktd_ref[0, 2 * s + 1, c] = t[HEAD_DIM:, c * dchunk:(c + 1) * dchunk]
        col += LANES
    for s in range(hd // 2):
        x = z_ref[0, :, col:col + LANES].astype(BF16)
        for c in range(ts // dchunk):
            vd_ref[0, 2 * s, c] = x[c * dchunk:(c + 1) * dchunk, 0:HEAD_DIM]
            vd_ref[0, 2 * s + 1, c] = x[c * dchunk:(c + 1) * dchunk, HEAD_DIM:]
        col += LANES


def _attn_prep(z, cos, sin, q_g, k_g, hq, hkv, hd):
    b, s, n = z.shape
    ts = min(s, 512)
    dchunk = 256
    nt = s // ts
    hdim = HEAD_DIM
    gq = jnp.tile(q_g, 2).reshape(1, LANES)
    gk = jnp.tile(k_g, 2).reshape(1, LANES)
    li = jnp.arange(LANES)
    grp = (li[:, None] // hdim == li[None, :] // hdim).astype(F32)
    kern = functools.partial(_prep_kernel, hq=hq, hkv=hkv, hd=hd, ts=ts, dchunk=dchunk)
    const = lambda shape: pl.BlockSpec(shape, lambda bi, i: (0,) * len(shape))
    return pl.pallas_call(
        kern,
        grid=(b, nt),
        in_specs=[
            pl.BlockSpec((1, ts, n), lambda bi, i: (bi, i, 0)),
            pl.BlockSpec((ts, LANES), lambda bi, i: (i, 0)),
            pl.BlockSpec((ts, LANES), lambda bi, i: (i, 0)),
            const((1, LANES)), const((1, LANES)), const((LANES, LANES)),
        ],
        out_specs=[
            pl.BlockSpec((1, hq, hdim, ts), lambda bi, i: (bi, 0, 0, i)),
            pl.BlockSpec((1, hkv, 1, ts, hdim), lambda bi, i: (bi, 0, i, 0, 0)),
            pl.BlockSpec((1, hkv, 1, hdim, ts), lambda bi, i: (bi, 0, i, 0, 0)),
            pl.BlockSpec((1, hd, ts, hdim), lambda bi, i: (bi, 0, i, 0)),
            pl.BlockSpec((1, hd, ts // dchunk, hdim, dchunk), lambda bi, i: (bi, 0, i, 0, 0)),
            pl.BlockSpec((1, hd, ts // dchunk, dchunk, hdim), lambda bi, i: (bi, 0, i, 0, 0)),
        ],
        out_shape=[
            jax.ShapeDtypeStruct((b, hq, hdim, s), BF16),
            jax.ShapeDtypeStruct((b, hkv, nt, ts, hdim), BF16),
            jax.ShapeDtypeStruct((b, hkv, nt, hdim, ts), BF16),
            jax.ShapeDtypeStruct((b, hd, s, hdim), BF16),
            jax.ShapeDtypeStruct((b, hd, s // dchunk, hdim, dchunk), BF16),
            jax.ShapeDtypeStruct((b, hd, s // dchunk, dchunk, hdim), BF16),
        ],
        compiler_params=_cparams("parallel", "parallel"),
        name="attn_prep",
    )(z, cos, sin, gq, gk, grp)


def _rope_tables(s):
    t = jnp.arange(s)
    row = (t // GRID_W).astype(F32)
    colp = (t % GRID_W).astype(F32)
    axis_dim = HEAD_DIM // 2
    inv = ROPE_THETA ** (-jnp.arange(0, axis_dim, 2, dtype=F32) / axis_dim)
    ar = row[:, None] * inv
    ac = colp[:, None] * inv
    cr, sr, cc, sc = jnp.cos(ar), jnp.sin(ar), jnp.cos(ac), jnp.sin(ac)
    cos = jnp.concatenate([cr, cr, cc, cc], axis=-1)
    sin = jnp.concatenate([-sr, sr, -sc, sc], axis=-1)
    return jnp.tile(cos, (1, 2)), jnp.tile(sin, (1, 2))


def _gqa_kernel(qt_ref, k_ref, vt_ref, kc_ref, vtc_ref, o_ref, acc_ref, s_ref, *, group):
    n_chunks = k_ref.shape[2]
    tq = qt_ref.shape[3]

    def q_all():
        return jnp.concatenate([qt_ref[0, g] for g in range(group)], axis=1)

    def scores(kb, slot):
        st = jnp.dot(kb, q_all(), preferred_element_type=F32)
        s_ref[slot] = st
        return jnp.max(st, axis=0, keepdims=True)

    def absorb(slot, vb, cmax, m, l):
        m_new = jnp.maximum(m, cmax)
        alpha = jnp.exp2(m - m_new)
        p = jnp.exp2(s_ref[slot] - m_new)
        l = alpha * l + jnp.sum(p, axis=0, keepdims=True)
        acc_ref[...] = alpha * acc_ref[...] + jnp.dot(vb, p.astype(BF16), preferred_element_type=F32)
        return m_new, l

    st = jnp.dot(kc_ref[0, 0], q_all(), preferred_element_type=F32)
    m = jnp.max(st, axis=0, keepdims=True)
    p = jnp.exp2(st - m)
    l = jnp.sum(p, axis=0, keepdims=True)
    acc_ref[...] = jnp.dot(vtc_ref[0, 0], p.astype(BF16), preferred_element_type=F32)

    cm = scores(k_ref[0, 0, 0], 0)

    def body(i, carry):
        m, l, cm0 = carry
        j = 2 * i
        cm1 = scores(k_ref[0, 0, j + 1], 1)
        m, l = absorb(0, vt_ref[0, 0, j], cm0, m, l)
        cm0 = scores(k_ref[0, 0, jnp.minimum(j + 2, n_chunks - 1)], 0)
        m, l = absorb(1, vt_ref[0, 0, j + 1], cm1, m, l)
        return m, l, cm0

    m, l, _ = lax.fori_loop(0, n_chunks // 2, body, (m, l, cm))
    out = acc_ref[...] / l
    for g in range(group):
        o_ref[0, g] = out[:, g * tq:(g + 1) * tq].astype(BF16)


def _gqa(qt, k, vt, k_ctx, vt_ctx):
    b, hq, hdim, s = qt.shape
    hkv, n_chunks, tk = k.shape[1], k.shape[2], k.shape[3]
    n_ctx = k_ctx.shape[2]
    group = hq // hkv
    tq = min(s, 256)
    return pl.pallas_call(
        functools.partial(_gqa_kernel, group=group),
        grid=(b, hkv, s // tq),
        in_specs=[
            pl.BlockSpec((1, group, hdim, tq), lambda bi, h, i: (bi, h, 0, i)),
            pl.BlockSpec((1, 1, n_chunks, tk, hdim), lambda bi, h, i: (bi, h, 0, 0, 0)),
            pl.BlockSpec((1, 1, n_chunks, hdim, tk), lambda bi, h, i: (bi, h, 0, 0, 0)),
            pl.BlockSpec((1, 1, n_ctx, hdim), lambda bi, h, i: (bi, h, 0, 0)),
            pl.BlockSpec((1, 1, hdim, n_ctx), lambda bi, h, i: (bi, h, 0, 0)),
        ],
        out_specs=pl.BlockSpec((1, group, hdim, tq), lambda bi, h, i: (bi, h, 0, i)),
        out_shape=jax.ShapeDtypeStruct((b, hq, hdim, s), BF16),
        scratch_shapes=[pltpu.VMEM((hdim, group * tq), F32), pltpu.VMEM((2, tk, group * tq), F32)],
        compiler_params=_cparams("parallel", "parallel", "parallel"),
        name="gqa_attention",
    )(qt, k, vt, k_ctx, vt_ctx)


def _ctx_gqa_kernel(qt_ref, k_ref, vt_ref, o_ref):
    st = jnp.dot(k_ref[0, 0], qt_ref[0, 0], preferred_element_type=F32)
    p = jnp.exp2(st - jnp.max(st, axis=0, keepdims=True))
    l = jnp.sum(p, axis=0, keepdims=True)
    acc = jnp.dot(vt_ref[0, 0], p.astype(BF16), preferred_element_type=F32)
    o_ref[0, 0] = (acc / l).astype(BF16)


def _ctx_gqa(qt, k, vt):
    b, hq, hdim, n = qt.shape
    group = hq // k.shape[1]
    return pl.pallas_call(
        _ctx_gqa_kernel,
        grid=(b, hq),
        in_specs=[
            pl.BlockSpec((1, 1, hdim, n), lambda bi, h: (bi, h, 0, 0)),
            pl.BlockSpec((1, 1, n, hdim), lambda bi, h: (bi, h // group, 0, 0)),
            pl.BlockSpec((1, 1, hdim, n), lambda bi, h: (bi, h // group, 0, 0)),
        ],
        out_specs=pl.BlockSpec((1, 1, hdim, n), lambda bi, h: (bi, h, 0, 0)),
        out_shape=jax.ShapeDtypeStruct((b, hq, hdim, n), BF16),
        compiler_params=_cparams("parallel", "parallel"),
        name="ctx_gqa",
    )(qt, k, vt)


def _ctx_mha_kernel(q_ref, kt_ref, v_ref, o_ref):
    s = jnp.dot(q_ref[0, 0], kt_ref[0, 0], preferred_element_type=F32)
    p = jnp.exp(s - jnp.max(s, axis=-1, keepdims=True))
    l = jnp.sum(p, axis=-1, keepdims=True)
    acc = jnp.dot(p.astype(BF16), v_ref[0, 0], preferred_element_type=F32)
    o_ref[0, 0] = (acc / l).astype(BF16)


def _ctx_mha(q, kt, v):
    b, h, n, hdim = q.shape
    return pl.pallas_call(
        _ctx_mha_kernel,
        grid=(b, h),
        in_specs=[
            pl.BlockSpec((1, 1, n, hdim), lambda bi, hi: (bi, hi, 0, 0)),
            pl.BlockSpec((1, 1, hdim, n), lambda bi, hi: (bi, hi, 0, 0)),
            pl.BlockSpec((1, 1, n, hdim), lambda bi, hi: (bi, hi, 0, 0)),
        ],
        out_specs=pl.BlockSpec((1, 1, n, hdim), lambda bi, hi: (bi, hi, 0, 0)),
        out_shape=jax.ShapeDtypeStruct((b, h, n, hdim), BF16),
        compiler_params=_cparams("parallel", "parallel"),
        name="ctx_mha",
    )(q, kt, v)


def _na_kernel(q_ref, kt_ref, v_ref, ktc_ref, vc_ref, bias_ref, o_ref, *, n_band):
    i = pl.program_id(2)
    nq = pl.num_programs(2)
    c0 = jnp.clip(i - 1, 0, nq - n_band)
    q = q_ref[0, 0]
    s_ctx = jnp.dot(q, ktc_ref[0, 0], preferred_element_type=F32)
    m = jnp.max(s_ctx, axis=-1, keepdims=True)
    s_lat = []
    for c in range(n_band):
        cw = kt_ref.shape[4]
        sl = jnp.dot(q, kt_ref[0, 0, c0 + c], preferred_element_type=F32)
        sl = sl + bias_ref[0, 0, :, c * cw:(c + 1) * cw]
        m = jnp.maximum(m, jnp.max(sl, axis=-1, keepdims=True))
        s_lat.append(sl)
    p = jnp.exp(s_ctx - m)
    l = jnp.sum(p, axis=-1, keepdims=True)
    acc = jnp.dot(p.astype(BF16), vc_ref[0, 0], preferred_element_type=F32)
    for c in range(n_band):
        p = jnp.exp(s_lat[c] - m)
        l = l + jnp.sum(p, axis=-1, keepdims=True)
        acc = acc + jnp.dot(p.astype(BF16), v_ref[0, 0, c0 + c], preferred_element_type=F32)
    o_ref[0, 0] = (acc / l).astype(BF16)


def _na_bias_table(rel_bias, rows):
    w = GRID_W
    h, n_rel_rows, n_rel_cols = rel_bias.shape
    qcol = np.arange(w)
    cstart = np.clip(qcol - NA_WIN_COLS // 2, 0, w - NA_WIN_COLS)
    in_win = (qcol[None, :] >= cstart[:, None]) & (qcol[None, :] < cstart[:, None] + NA_WIN_COLS)
    col_idx = np.clip(qcol[None, :] - qcol[:, None] + NA_WIN_COLS - 1, 0, n_rel_cols - 1)
    onehot = (col_idx.reshape(-1)[None, :] == np.arange(n_rel_cols)[:, None]).astype(np.float32)
    cols = jnp.einsum('hrj,jn->hrn', rel_bias, onehot, precision=HIGHEST).reshape(h, n_rel_rows, w, w)
    cols = jnp.where(in_win[None, None], cols, NEG_BIG)
    ext = jnp.concatenate([cols, jnp.full((h, 1, w, w), NEG_BIG, F32)], axis=1)
    a = np.arange(NA_Q_ROWS)[:, None]
    c = np.arange(NA_BAND_ROWS)[None, :]
    tables = []
    for r0, rs0 in ((0, 0), (NA_Q_ROWS, 0), (rows - NA_Q_ROWS, rows - NA_BAND_ROWS)):
        r = r0 + a
        kr = rs0 + c
        rs = np.clip(r - NA_WIN_ROWS // 2, 0, rows - NA_WIN_ROWS)
        valid = (kr >= rs) & (kr < rs + NA_WIN_ROWS)
        row_idx = np.where(valid, kr - r + NA_WIN_ROWS - 1, n_rel_rows)
        bias = ext[:, row_idx.reshape(-1)].reshape(h, NA_Q_ROWS, NA_BAND_ROWS, w, w)
        tables.append(bias.transpose(0, 1, 3, 2, 4).reshape(h, NA_Q_ROWS * w, NA_BAND_ROWS * w))
    return jnp.stack(tables, axis=1).astype(F32)


def _na(q, kt, v, kt_ctx, v_ctx, bias):
    b, h, s, hdim = q.shape
    n_chunks, cw = kt.shape[2], kt.shape[4]
    n_ctx = v_ctx.shape[2]
    tq = NA_Q_ROWS * GRID_W
    nq = s // tq
    n_band = NA_BAND_ROWS * GRID_W // cw
    band_w = NA_BAND_ROWS * GRID_W
    return pl.pallas_call(
        functools.partial(_na_kernel, n_band=n_band),
        grid=(b, h, nq),
        in_specs=[
            pl.BlockSpec((1, 1, tq, hdim), lambda bi, hi, i: (bi, hi, i, 0)),
            pl.BlockSpec((1, 1, n_chunks, hdim, cw), lambda bi, hi, i: (bi, hi, 0, 0, 0)),
            pl.BlockSpec((1, 1, n_chunks, cw, hdim), lambda bi, hi, i: (bi, hi, 0, 0, 0)),
            pl.BlockSpec((1, 1, hdim, n_ctx), lambda bi, hi, i: (bi, hi, 0, 0)),
            pl.BlockSpec((1, 1, n_ctx, hdim), lambda bi, hi, i: (bi, hi, 0, 0)),
            pl.BlockSpec((1, 1, tq, band_w),
                         lambda bi, hi, i: (hi, jnp.where(i == 0, 0, jnp.where(i == nq - 1, 2, 1)), 0, 0)),
        ],
        out_specs=pl.BlockSpec((1, 1, tq, hdim), lambda bi, hi, i: (bi, hi, i, 0)),
        out_shape=jax.ShapeDtypeStruct((b, h, s, hdim), BF16),
        compiler_params=_cparams("parallel", "parallel", "parallel"),
        name="na_attention",
    )(q, kt, v, kt_ctx, v_ctx, bias)


def kernel(x, c, ctx, c_ctx, ada_w, ada_b, ln1_g, ln1_b, ln2_g, ln2_b, conv_w_in, conv_w_out, conv_a_dw_w, conv_a_dw_b, conv_a_ln_g, conv_a_ln_b, conv_b_dw_w, attn_w_in, attn_w_out, q_norm_g, k_norm_g, na_rel_bias, router_w1, router_b1, router_w2, router_b2, expert_w_gate, expert_w_up, expert_w_down):
    bsz, seq, d = x.shape
    n_ctx = ctx.shape[1]
    depth = ada_w.shape[0]
    alpha = (2 * depth) ** 0.25
    hq = (d // 2) // HEAD_DIM
    hkv = hq // 4
    hd = (d // 2) // HEAD_DIM

    cond8 = jnp.zeros((8, d), F32).at[:bsz].set(c).at[bsz].set(c_ctx)
    mods = _ada(cond8, ada_w, ada_b)

    def lat_mod(i, k):
        return mods[i, :bsz, k * d:(k + 1) * d].reshape(bsz, 1, d)

    def ctx_mod(i, k):
        return jnp.broadcast_to(mods[i, bsz, k * d:(k + 1) * d].reshape(1, 1, d), (bsz, 1, d))

    wr = jnp.zeros((depth, d, ROUTER_COLS), F32)
    wr = wr.at[:, :, :N_GROUPS].set(router_w1).at[:, :, N_GROUPS:N_GROUPS + N_EXPERTS].set(router_w2)
    br = jnp.zeros((depth, 1, ROUTER_COLS), F32)
    br = br.at[:, 0, :N_GROUPS].set(router_b1).at[:, 0, N_GROUPS:N_GROUPS + N_EXPERTS].set(router_b2)

    cos_lat, sin_lat = _rope_tables(seq)
    cos_ctx = jnp.ones((n_ctx, LANES), F32)
    sin_ctx = jnp.zeros((n_ctx, LANES), F32)

    x_lat, x_ctx = x, ctx
    for i in range(depth):
        last = i == depth - 1
        odd = i % 2 == 1
        j = i // 2
        use_ctx = odd or not last
        if odd:
            w_in = attn_w_in[j].astype(BF16)
            w_out = attn_w_out[j].astype(BF16)
            z_lat = _mod_matmul(x_lat, lat_mod(i, 1), lat_mod(i, 0), w_in)
            z_ctx = _mod_matmul(x_ctx, ctx_mod(i, 1), ctx_mod(i, 0), w_in)
            qt, kc, vtc, qd, ktd, vd = _attn_prep(z_lat, cos_lat, sin_lat, q_norm_g[j], k_norm_g[j], hq, hkv, hd)
            cqt, ckc, cvtc, cqd, cktd, cvd = _attn_prep(z_ctx, cos_ctx, sin_ctx, q_norm_g[j], k_norm_g[j],
                                                        hq, hkv, hd)
            ckc = ckc.reshape(bsz, hkv, n_ctx, HEAD_DIM)
            cvtc = cvtc.reshape(bsz, hkv, HEAD_DIM, n_ctx)
            cktd = cktd.reshape(bsz, hd, HEAD_DIM, n_ctx)
            cvd = cvd.reshape(bsz, hd, n_ctx, HEAD_DIM)
            ot_c = _gqa(qt, kc, vtc, ckc, cvtc)
            bias = _na_bias_table(na_rel_bias[j], seq // GRID_W)
            o_d = _na(qd, ktd, vd, cktd, cvd, bias)
            mix_lat = (ot_c, o_d)
            if not last:
                mix_ctx = (_ctx_gqa(cqt, ckc, cvtc), _ctx_mha(cqd, cktd, cvd))
        else:
            w_in = conv_w_in[j].astype(BF16)
            w_out = conv_w_out[j].astype(BF16)
            conv_w = (conv_a_dw_w[j], conv_a_dw_b[j], conv_a_ln_g[j], conv_a_ln_b[j], conv_b_dw_w[j])
            z_lat = _mod_matmul(x_lat, lat_mod(i, 1), lat_mod(i, 0), w_in)
            mix_lat = _conv_mixers(z_lat, *conv_w)
            if not last:
                z_ctx = _mod_matmul(x_ctx, ctx_mod(i, 1), ctx_mod(i, 0), w_in)
                mix_ctx = _conv_mixers(z_ctx, *conv_w)

        x_lat, h_lat, lg_lat = _proj_ln(mix_lat, w_out, x_lat, lat_mod(i, 2), ln1_g[i], ln1_b[i],
                                        lat_mod(i, 4), lat_mod(i, 3), wr[i], br[i], alpha)
        moe_w = (expert_w_gate[i], expert_w_up[i], expert_w_down[i])
        n_lat = bsz * seq
        if last:
            ys, dest, gates = _moe(h_lat.reshape(n_lat, d), lg_lat.reshape(n_lat, ROUTER_COLS), *moe_w)
            x_lat = _moe_ln(x_lat, ys, dest, gates, 0, lat_mod(i, 5), ln2_g[i], ln2_b[i], alpha)
        else:
            x_ctx, h_ctx, lg_ctx = _proj_ln(mix_ctx, w_out, x_ctx, ctx_mod(i, 2), ln1_g[i], ln1_b[i],
                                            ctx_mod(i, 4), ctx_mod(i, 3), wr[i], br[i], alpha)
            h_all = jnp.concatenate([h_lat.reshape(n_lat, d), h_ctx.reshape(bsz * n_ctx, d)], axis=0)
            lg_all = jnp.concatenate([lg_lat.reshape(n_lat, ROUTER_COLS),
                                      lg_ctx.reshape(bsz * n_ctx, ROUTER_COLS)], axis=0)
            ys, dest, gates = _moe(h_all, lg_all, *moe_w)
            x_lat = _moe_ln(x_lat, ys, dest, gates, 0, lat_mod(i, 5), ln2_g[i], ln2_b[i], alpha)
            x_ctx = _moe_ln(x_ctx, ys, dest, gates, n_lat, ctx_mod(i, 5), ln2_g[i], ln2_b[i], alpha)
    return x_lat
```

```python
import functools
import math

import jax
import jax.numpy as jnp
import numpy as np
from jax import lax
from jax.experimental import pallas as pl
from jax.experimental.pallas import tpu as pltpu

F32 = jnp.float32
BF16 = jnp.bfloat16
HIGHEST = lax.Precision.HIGHEST

GRID_W = 64
HEAD_DIM = 64
V_ROWS = HEAD_DIM + 16
ATTN_SCALE = HEAD_DIM ** -0.5
LOG2E = math.log2(math.e)
CONV_A_WIDTH = 31
CONV_B_WIDTH = 3
CONV_HALO = 16
NA_WIN_ROWS = 8
NA_WIN_COLS = 16
NA_Q_ROWS = 4
NA_BAND_ROWS = 12
ROPE_THETA = 10000.0
N_GROUPS = 4
EXPERTS_PER_GROUP = 8
N_EXPERTS = N_GROUPS * EXPERTS_PER_GROUP
LN_EPS = 1e-5
RMS_EPS = 1e-6
NEG_BIG = -1e30

VMEM_LIMIT_BYTES = 48 * 1024 * 1024
LANES = 128
ROUTER_COLS = LANES
EXPERT_BLOCK = 256
RANK_BLOCK = 512


def _cparams(*sem):
    return pltpu.CompilerParams(dimension_semantics=sem, vmem_limit_bytes=VMEM_LIMIT_BYTES)


def _layer_norm(x, g, b):
    mu = jnp.mean(x, axis=-1, keepdims=True)
    xc = x - mu
    var = jnp.mean(xc * xc, axis=-1, keepdims=True)
    return xc * lax.rsqrt(var + LN_EPS) * g + b


def _ada_kernel(c_ref, w_ref, b_ref, o_ref):
    c = c_ref[...]
    s = c * jax.nn.sigmoid(c)
    o_ref[0] = jnp.dot(s, w_ref[0], preferred_element_type=F32, precision=HIGHEST) + b_ref[0]


def _ada(cond8, ada_w, ada_b):
    n_layers, d, n = ada_w.shape
    tn = 1536
    return pl.pallas_call(
        _ada_kernel,
        grid=(n_layers, n // tn),
        in_specs=[
            pl.BlockSpec((8, d), lambda l, j: (0, 0)),
            pl.BlockSpec((1, d, tn), lambda l, j: (l, 0, j)),
            pl.BlockSpec((1, 1, tn), lambda l, j: (l, 0, j)),
        ],
        out_specs=pl.BlockSpec((1, 8, tn), lambda l, j: (l, 0, j)),
        out_shape=jax.ShapeDtypeStruct((n_layers, 8, n), F32),
        compiler_params=_cparams("parallel", "parallel"),
        name="ada_mod",
    )(cond8, ada_w, ada_b.reshape(n_layers, 1, n))


def _mod_matmul_kernel(x_ref, sc_ref, sh_ref, w_ref, o_ref):
    h = x_ref[0] * (1.0 + sc_ref[0]) + sh_ref[0]
    o_ref[0] = jnp.dot(h.astype(BF16), w_ref[...], preferred_element_type=F32)


def _mod_matmul(x, sc, sh, w):
    b, s, d = x.shape
    n = w.shape[1]
    tm = min(s, 512)
    return pl.pallas_call(
        _mod_matmul_kernel,
        grid=(b, s // tm),
        in_specs=[
            pl.BlockSpec((1, tm, d), lambda bi, i: (bi, i, 0)),
            pl.BlockSpec((1, 1, d), lambda bi, i: (bi, 0, 0)),
            pl.BlockSpec((1, 1, d), lambda bi, i: (bi, 0, 0)),
            pl.BlockSpec((d, n), lambda bi, i: (0, 0)),
        ],
        out_specs=pl.BlockSpec((1, tm, n), lambda bi, i: (bi, i, 0)),
        out_shape=jax.ShapeDtypeStruct((b, s, n), F32),
        compiler_params=_cparams("parallel", "parallel"),
        name="mod_matmul",
    )(x, sc, sh, w)


def _conv_kernel(z_ref, zp_ref, zn_ref, aw_ref, ab_ref, ag_ref, abb_ref, bw_ref,
                 o_ref, ea_ref, eb_ref, sh_ref, *, tile, ch):
    i = pl.program_id(1)
    n = pl.num_programs(1)
    halo = CONV_HALO

    def glu(zz):
        return zz[:, 0:ch] * jax.nn.sigmoid(zz[:, ch:2 * ch])

    def cv(zz):
        return zz[:, 3 * ch:4 * ch] * zz[:, 4 * ch:5 * ch]

    pmask = (i > 0).astype(F32)
    nmask = (i < n - 1).astype(F32)
    zp = zp_ref[0]
    zn = zn_ref[0]
    ea_ref[0:halo, :] = glu(zp) * pmask
    eb_ref[0:halo, :] = cv(zp) * pmask
    ea_ref[halo + tile:2 * halo + tile, :] = glu(zn) * nmask
    eb_ref[halo + tile:2 * halo + tile, :] = cv(zn) * nmask
    ea_ref[halo:halo + tile, :] = glu(z_ref[0])
    eb_ref[halo:halo + tile, :] = cv(z_ref[0])

    rows = 32
    pad_a = CONV_A_WIDTH // 2
    pad_b = CONV_B_WIDTH // 2
    n_sh = sh_ref.shape[1]
    piece = 40
    for b in range(1, 8):
        for r in range(0, n_sh, piece):
            sh_ref[b - 1, r:r + piece, :] = ea_ref[pl.ds(r + b, piece), :]
    for r0 in range(0, tile, rows):
        acc = jnp.zeros((rows, ch), F32) + ab_ref[...]
        for k in range(CONV_A_WIDTH):
            a8, b = divmod(halo + k - pad_a, 8)
            if b == 0:
                tap = ea_ref[pl.ds(r0 + 8 * a8, rows), :]
            else:
                tap = sh_ref[b - 1, pl.ds(r0 + 8 * a8, rows), :]
            acc = acc + aw_ref[k:k + 1, :] * tap
        ya = _layer_norm(acc, ag_ref[...], abb_ref[...])
        ya = ya * jax.nn.sigmoid(ya)
        cb = jnp.zeros((rows, ch), F32)
        for k in range(CONV_B_WIDTH):
            cb = cb + bw_ref[k:k + 1, :] * eb_ref[pl.ds(halo + r0 + k - pad_b, rows), :]
        yb = z_ref[0, r0:r0 + rows, 2 * ch:3 * ch] * cb
        o_ref[0, r0:r0 + rows, 0:ch] = ya.astype(BF16)
        o_ref[0, r0:r0 + rows, ch:2 * ch] = yb.astype(BF16)


def _conv_mixers(z, a_dw_w, a_dw_b, a_ln_g, a_ln_b, b_dw_w):
    b, s, n = z.shape
    ch = n // 5
    tile = min(s, 256)
    hb = tile // CONV_HALO
    last_hb = s // CONV_HALO - 1
    kern = functools.partial(_conv_kernel, tile=tile, ch=ch)
    vec = lambda v: v.reshape(1, ch)
    return pl.pallas_call(
        kern,
        grid=(b, s // tile),
        in_specs=[
            pl.BlockSpec((1, tile, n), lambda bi, i: (bi, i, 0)),
            pl.BlockSpec((1, CONV_HALO, n), lambda bi, i: (bi, jnp.maximum(i * hb - 1, 0), 0)),
            pl.BlockSpec((1, CONV_HALO, n), lambda bi, i: (bi, jnp.minimum((i + 1) * hb, last_hb), 0)),
            pl.BlockSpec((CONV_A_WIDTH, ch), lambda bi, i: (0, 0)),
            pl.BlockSpec((1, ch), lambda bi, i: (0, 0)),
            pl.BlockSpec((1, ch), lambda bi, i: (0, 0)),
            pl.BlockSpec((1, ch), lambda bi, i: (0, 0)),
            pl.BlockSpec((CONV_B_WIDTH, ch), lambda bi, i: (0, 0)),
        ],
        out_specs=pl.BlockSpec((1, tile, 2 * ch), lambda bi, i: (bi, i, 0)),
        out_shape=jax.ShapeDtypeStruct((b, s, 2 * ch), BF16),
        scratch_shapes=[pltpu.VMEM((tile + 2 * CONV_HALO, ch), F32),
                        pltpu.VMEM((tile + 2 * CONV_HALO, ch), F32),
                        pltpu.VMEM((7, tile + 2 * CONV_HALO - 8, ch), F32)],
        compiler_params=_cparams("parallel", "parallel"),
        name="conv_mixers",
    )(z, z, z, a_dw_w, vec(a_dw_b), vec(a_ln_g), vec(a_ln_b), b_dw_w)


def _proj_ln_kernel(*refs, n_mix, n_alias, alpha):
    mix_refs = refs[:n_mix]
    (w_ref, x_ref, g_ref, lng_ref, lnb_ref, sc_ref, sh_ref, wrh_ref, wrl_ref, br_ref) = refs[n_mix:n_mix + 10]
    xo_ref, h_ref, lg_ref = refs[n_mix + 10 + n_alias:]
    if n_mix == 2:
        ut = jnp.concatenate([r[0].reshape(r.shape[1] * r.shape[2], r.shape[3]) for r in mix_refs], axis=0)
        y = lax.dot_general(ut, w_ref[...], (((0,), (0,)), ((), ())), preferred_element_type=F32)
    else:
        y = jnp.dot(mix_refs[0][0], w_ref[...], preferred_element_type=F32)
    xn = _layer_norm(alpha * x_ref[0] + g_ref[0] * y, lng_ref[...], lnb_ref[...])
    xo_ref[0] = xn
    h2 = xn * (1.0 + sc_ref[0]) + sh_ref[0]
    h_ref[...] = h2
    h_hi = h2.astype(BF16)
    h_lo = (h2 - h_hi.astype(F32)).astype(BF16)
    lg_ref[...] = (jnp.dot(h_hi, wrh_ref[...], preferred_element_type=F32)
                   + (jnp.dot(h_lo, wrh_ref[...], preferred_element_type=F32)
                      + jnp.dot(h_hi, wrl_ref[...], preferred_element_type=F32))
                   + br_ref[...])


def _proj_ln(mix, w_out, x, g, ln_g, ln_b, sc2, sh2, wr, br, alpha, n_tok, tok_off, bufs=None):
    b, s, d = x.shape
    tm = min(s, 512)
    nt = s // tm
    off = tok_off // tm
    if isinstance(mix, tuple):
        mix_specs = [pl.BlockSpec((1, o.shape[1], HEAD_DIM, tm), lambda bi, i: (bi, 0, 0, i)) for o in mix]
        mix_args = list(mix)
    else:
        mix_specs = [pl.BlockSpec((1, tm, mix.shape[2]), lambda bi, i: (bi, i, 0))]
        mix_args = [mix]
    row = lambda: pl.BlockSpec((1, tm, d), lambda bi, i: (bi, i, 0))
    per_b = lambda: pl.BlockSpec((1, 1, d), lambda bi, i: (bi, 0, 0))
    vec = lambda: pl.BlockSpec((1, d), lambda bi, i: (0, 0))
    alias_args = [] if bufs is None else list(bufs)
    n_in = len(mix_args) + 10
    wr_hi = wr.astype(BF16)
    wr_lo = (wr - wr_hi.astype(F32)).astype(BF16)
    kern = functools.partial(_proj_ln_kernel, n_mix=len(mix_args), n_alias=len(alias_args), alpha=alpha)
    return pl.pallas_call(
        kern,
        grid=(b, nt),
        in_specs=mix_specs + [
            pl.BlockSpec(w_out.shape, lambda bi, i: (0, 0)),
            row(), per_b(), vec(), vec(), per_b(), per_b(),
            pl.BlockSpec((d, ROUTER_COLS), lambda bi, i: (0, 0)),
            pl.BlockSpec((d, ROUTER_COLS), lambda bi, i: (0, 0)),
            pl.BlockSpec((1, ROUTER_COLS), lambda bi, i: (0, 0)),
        ] + [pl.BlockSpec(memory_space=pl.ANY) for _ in alias_args],
        out_specs=[row(),
                   pl.BlockSpec((tm, d), lambda bi, i: (off + bi * nt + i, 0)),
                   pl.BlockSpec((tm, ROUTER_COLS), lambda bi, i: (off + bi * nt + i, 0))],
        out_shape=[jax.ShapeDtypeStruct((b, s, d), F32),
                   jax.ShapeDtypeStruct((n_tok, d), F32),
                   jax.ShapeDtypeStruct((n_tok, ROUTER_COLS), F32)],
        input_output_aliases={n_in: 1, n_in + 1: 2} if alias_args else {},
        compiler_params=_cparams("parallel", "parallel"),
        name="proj_res_ln",
    )(*mix_args, w_out, x, g, ln_g.reshape(1, d), ln_b.reshape(1, d), sc2, sh2, wr_hi, wr_lo, br, *alias_args)


def _rank_kernel(e_ref, tri_ref, rank_ref, cnt_ref, carry_ref):
    @pl.when(pl.program_id(0) == 0)
    def _():
        carry_ref[...] = jnp.zeros_like(carry_ref)

    e = e_ref[0]
    blk = e.shape[1]
    ids = lax.broadcasted_iota(jnp.int32, (N_EXPERTS, blk), 0)
    oh = jnp.where(ids == e, 1.0, 0.0).astype(F32)
    before = jnp.dot(oh.astype(BF16), tri_ref[...], preferred_element_type=F32)
    carry = carry_ref[...]
    before = before + carry[:, 0:1]
    rank_ref[0] = jnp.sum(oh * before, axis=0, keepdims=True).astype(jnp.int32)
    carry = carry + jnp.sum(oh, axis=1, keepdims=True)
    carry_ref[...] = carry
    cnt_ref[...] = carry.astype(jnp.int32)


def _expert_ranks(flat_e):
    m = flat_e.shape[0]
    blk = RANK_BLOCK
    nblk = m // blk
    tri = (lax.broadcasted_iota(jnp.int32, (blk, blk), 0)
           < lax.broadcasted_iota(jnp.int32, (blk, blk), 1)).astype(BF16)
    rank, cnt = pl.pallas_call(
        _rank_kernel,
        grid=(nblk,),
        in_specs=[pl.BlockSpec((1, 1, blk), lambda i: (i, 0, 0)),
                  pl.BlockSpec((blk, blk), lambda i: (0, 0))],
        out_specs=[pl.BlockSpec((1, 1, blk), lambda i: (i, 0, 0)),
                   pl.BlockSpec((N_EXPERTS, LANES), lambda i: (0, 0))],
        out_shape=[jax.ShapeDtypeStruct((nblk, 1, blk), jnp.int32),
                   jax.ShapeDtypeStruct((N_EXPERTS, LANES), jnp.int32)],
        scratch_shapes=[pltpu.VMEM((N_EXPERTS, LANES), F32)],
        compiler_params=_cparams("arbitrary"),
        name="expert_rank",
    )(flat_e.reshape(nblk, 1, blk), tri)
    return rank.reshape(m), cnt[:, 0]


def _to_row_tiles(ref, x, n_rows, tiles):
    for s in range(tiles):
        ref[pl.ds(s, n_rows, stride=tiles), :] = x[:, s * LANES:(s + 1) * LANES]


def _from_row_tiles(ref, n_rows, tiles):
    return jnp.concatenate([ref[pl.ds(s, n_rows, stride=tiles), :] for s in range(tiles)], axis=1)


def _dispatch_kernel(dest_ref, x_ref, init_ref, xs_ref, buf_ref, sem, *, tm, tiles):
    del init_ref
    _to_row_tiles(buf_ref, x_ref[...], tm, tiles)

    def issue(r, c):
        src = buf_ref.at[pl.ds(pl.multiple_of(r * tiles, tiles), tiles), :]
        for k in range(2):
            d = pl.multiple_of(dest_ref[0, 0, 2 * r + k] * tiles, tiles)
            pltpu.make_async_copy(src, xs_ref.at[pl.ds(d, tiles), :], sem).start(priority=k)
        return c

    lax.fori_loop(0, tm, issue, 0, unroll=4)
    for _ in range(2):
        pltpu.make_async_copy(buf_ref, xs_ref.at[pl.ds(0, tm * tiles), :], sem).wait()


def _dispatch(h_all, dest, cap):
    n, d = h_all.shape
    tiles = d // LANES
    tm = 512
    nt = n // tm
    return pl.pallas_call(
        functools.partial(_dispatch_kernel, tm=tm, tiles=tiles),
        grid=(nt,),
        in_specs=[
            pl.BlockSpec((1, 1, 2 * tm), lambda i: (i, 0, 0), memory_space=pltpu.SMEM),
            pl.BlockSpec((tm, d), lambda i: (i, 0)),
            pl.BlockSpec(memory_space=pl.ANY),
        ],
        out_specs=pl.BlockSpec(memory_space=pl.ANY),
        out_shape=jax.ShapeDtypeStruct((cap * tiles, LANES), F32),
        scratch_shapes=[pltpu.VMEM((tm * tiles, LANES), F32), pltpu.SemaphoreType.DMA(())],
        input_output_aliases={2: 0},
        compiler_params=_cparams("arbitrary"),
        name="moe_dispatch",
    )(dest.reshape(nt, 1, 2 * tm), h_all, jnp.zeros((cap * tiles, LANES), F32))


def _experts_kernel(be_ref, nu_ref, x_ref, wg_ref, wu_ref, wd_ref, o_ref, wgb_ref, wub_ref, wdb_ref,
                    *, tb, tiles):
    i = pl.program_id(0)
    changed = jnp.logical_or(i == 0, be_ref[i] != be_ref[jnp.maximum(i - 1, 0)])

    @pl.when(changed)
    def _():
        wgb_ref[...] = wg_ref[0].astype(BF16)
        wub_ref[...] = wu_ref[0].astype(BF16)
        wdb_ref[...] = wd_ref[0].astype(BF16)

    @pl.when(i < nu_ref[0])
    def _():
        x = _from_row_tiles(x_ref, tb, tiles).astype(BF16)
        g = jnp.dot(x, wgb_ref[...], preferred_element_type=F32)
        u = jnp.dot(x, wub_ref[...], preferred_element_type=F32)
        a = (g * jax.nn.sigmoid(g) * u).astype(BF16)
        _to_row_tiles(o_ref, jnp.dot(a, wdb_ref[...], preferred_element_type=F32), tb, tiles)

    @pl.when(i >= nu_ref[0])
    def _():
        o_ref[...] = jnp.zeros_like(o_ref)


def _experts(xs, block_e, n_used, w_gate, w_up, w_down):
    d, hid = w_gate.shape[1], w_gate.shape[2]
    tiles = d // LANES
    tb = EXPERT_BLOCK
    nb = xs.shape[0] // (tb * tiles)
    grid_spec = pltpu.PrefetchScalarGridSpec(
        num_scalar_prefetch=2,
        grid=(nb,),
        in_specs=[
            pl.BlockSpec((tb * tiles, LANES), lambda i, be, nu: (i, 0)),
            pl.BlockSpec((1, d, hid), lambda i, be, nu: (be[i], 0, 0)),
            pl.BlockSpec((1, d, hid), lambda i, be, nu: (be[i], 0, 0)),
            pl.BlockSpec((1, hid, d), lambda i, be, nu: (be[i], 0, 0)),
        ],
        out_specs=pl.BlockSpec((tb * tiles, LANES), lambda i, be, nu: (i, 0)),
        scratch_shapes=[pltpu.VMEM((d, hid), BF16), pltpu.VMEM((d, hid), BF16), pltpu.VMEM((hid, d), BF16)],
    )
    return pl.pallas_call(
        functools.partial(_experts_kernel, tb=tb, tiles=tiles),
        grid_spec=grid_spec,
        out_shape=jax.ShapeDtypeStruct(xs.shape, F32),
        compiler_params=_cparams("arbitrary"),
        name="experts",
    )(block_e, n_used, xs, w_gate, w_up, w_down)


def _moe_ln_kernel(dest_ref, x_ref, gt_ref, g_ref, lng_ref, lnb_ref, ys_ref, o_ref,
                   y0_ref, y1_ref, sem0, sem1, *, tm, tiles, alpha):
    def issue(r, c):
        dst = pl.ds(pl.multiple_of(r * tiles, tiles), tiles)
        d0 = pl.multiple_of(dest_ref[0, 0, 2 * r] * tiles, tiles)
        d1 = pl.multiple_of(dest_ref[0, 0, 2 * r + 1] * tiles, tiles)
        pltpu.make_async_copy(ys_ref.at[pl.ds(d0, tiles), :], y0_ref.at[dst, :], sem0).start(priority=0)
        pltpu.make_async_copy(ys_ref.at[pl.ds(d1, tiles), :], y1_ref.at[dst, :], sem1).start(priority=1)
        return c

    lax.fori_loop(0, tm, issue, 0, unroll=4)
    pltpu.make_async_copy(ys_ref.at[pl.ds(0, tm * tiles), :], y0_ref, sem0).wait()
    pltpu.make_async_copy(ys_ref.at[pl.ds(0, tm * tiles), :], y1_ref, sem1).wait()
    gt = gt_ref[...]
    y = gt[:, 0:1] * _from_row_tiles(y0_ref, tm, tiles) + gt[:, 1:2] * _from_row_tiles(y1_ref, tm, tiles)
    o_ref[...] = _layer_norm(alpha * x_ref[...] + g_ref[0] * y, lng_ref[...], lnb_ref[...])


def _moe_ln(x, ys, dest, gates, tok_off, g, ln_g, ln_b, alpha):
    b, s, d = x.shape
    tiles = d // LANES
    n_tok = gates.shape[0]
    tm = min(s, 512)
    nt = s // tm
    off = tok_off // tm
    vec = lambda: pl.BlockSpec((1, d), lambda bi, i: (0, 0))
    out = pl.pallas_call(
        functools.partial(_moe_ln_kernel, tm=tm, tiles=tiles, alpha=alpha),
        grid=(b, nt),
        in_specs=[
            pl.BlockSpec((1, 1, 2 * tm), lambda bi, i: (off + bi * nt + i, 0, 0), memory_space=pltpu.SMEM),
            pl.BlockSpec((tm, d), lambda bi, i: (bi * nt + i, 0)),
            pl.BlockSpec((tm, 2), lambda bi, i: (off + bi * nt + i, 0)),
            pl.BlockSpec((1, 1, d), lambda bi, i: (bi, 0, 0)),
            vec(), vec(),
            pl.BlockSpec(memory_space=pl.ANY),
        ],
        out_specs=pl.BlockSpec((tm, d), lambda bi, i: (bi * nt + i, 0)),
        out_shape=jax.ShapeDtypeStruct((b * s, d), F32),
        scratch_shapes=[pltpu.VMEM((tm * tiles, LANES), F32), pltpu.VMEM((tm * tiles, LANES), F32),
                        pltpu.SemaphoreType.DMA(()), pltpu.SemaphoreType.DMA(())],
        compiler_params=_cparams("arbitrary", "arbitrary"),
        name="moe_res_ln",
    )(dest.reshape(n_tok // tm, 1, 2 * tm), x.reshape(b * s, d), gates, g,
      ln_g.reshape(1, d), ln_b.reshape(1, d), ys)
    return out.reshape(b, s, d)


def _route(logits):
    n = logits.shape[0]
    l1 = logits[:, :N_GROUPS]
    grp = jnp.argmax(l1, axis=-1).astype(jnp.int32)
    p_grp = 1.0 / jnp.sum(jnp.exp(l1 - jnp.max(l1, axis=-1, keepdims=True)), axis=-1)
    lg2 = logits[:, N_GROUPS:N_GROUPS + N_EXPERTS].reshape(n, N_GROUPS, EXPERTS_PER_GROUP)
    gsel = grp[:, None] == jnp.arange(N_GROUPS, dtype=jnp.int32)[None, :]
    sel = jnp.sum(jnp.where(gsel[:, :, None], lg2, 0.0), axis=1)
    i1 = jnp.argmax(sel, axis=-1).astype(jnp.int32)
    v1 = jnp.max(sel, axis=-1)
    col = jnp.arange(EXPERTS_PER_GROUP, dtype=jnp.int32)[None, :]
    sel2 = jnp.where(col == i1[:, None], -jnp.inf, sel)
    i2 = jnp.argmax(sel2, axis=-1).astype(jnp.int32)
    v2 = jnp.max(sel2, axis=-1)
    e2 = jnp.exp(v2 - v1)
    w1 = 1.0 / (1.0 + e2)
    gates = p_grp[:, None] * jnp.stack([w1, e2 * w1], axis=-1)
    experts = grp[:, None] * EXPERTS_PER_GROUP + jnp.stack([i1, i2], axis=-1)
    return experts, gates.astype(F32)


def _moe(h_all, logits, w_gate, w_up, w_down):
    n, d = h_all.shape
    m = 2 * n
    tb = EXPERT_BLOCK
    experts, gates = _route(logits)
    flat_e = experts.reshape(m)
    rank, counts = _expert_ranks(flat_e)
    padded = (counts + tb - 1) // tb * tb
    pad_end = jnp.cumsum(padded)
    pad_start = pad_end - padded
    onehot = flat_e[:, None] == jnp.arange(N_EXPERTS, dtype=jnp.int32)[None, :]
    dest = jnp.sum(jnp.where(onehot, pad_start[None, :], 0), axis=-1).astype(jnp.int32) + rank
    nb = -(-m // tb) + N_EXPERTS
    cap = nb * tb
    blk_start = jnp.arange(nb, dtype=jnp.int32) * tb
    block_e = jnp.minimum(jnp.sum(pad_end[None, :] <= blk_start[:, None], axis=-1), N_EXPERTS - 1)
    n_used = (pad_end[-1] // tb).reshape(1)

    xs = _dispatch(h_all, dest, cap)
    ys = _experts(xs, block_e.astype(jnp.int32), n_used.astype(jnp.int32), w_gate, w_up, w_down)
    return ys, dest, gates


def _prep_kernel(z_ref, cos_ref, sin_ref, gq_ref, gk_ref, grp_ref,
                 qt_ref, kc_ref, vt_ref, qd_ref, ktd_ref, vd_ref, *, hq, hkv, hd, ts, dchunk):
    cos = cos_ref[...]
    sin = sin_ref[...]
    lane = lax.broadcasted_iota(jnp.int32, (ts, LANES), 1)
    first_half = (lane % 32) < 16

    def norm_rope(x, g):
        ss = jnp.dot(x * x, grp_ref[...], preferred_element_type=F32, precision=HIGHEST)
        xn = x * lax.rsqrt(ss * (1.0 / HEAD_DIM) + RMS_EPS) * g
        partner = jnp.where(first_half, pltpu.roll(xn, LANES - 16, 1), pltpu.roll(xn, 16, 1))
        return xn * cos + partner * sin

    col = 0
    for s in range(hq // 2):
        x = norm_rope(z_ref[0, :, col:col + LANES], gq_ref[...]) * (ATTN_SCALE * LOG2E)
        t = x.T.astype(BF16)
        qt_ref[0, 2 * s] = t[0:HEAD_DIM]
        qt_ref[0, 2 * s + 1] = t[HEAD_DIM:]
        col += LANES
    for s in range(hkv // 2):
        x = norm_rope(z_ref[0, :, col:col + LANES], gk_ref[...]).astype(BF16)
        kc_ref[0, 2 * s, 0] = x[:, 0:HEAD_DIM]
        kc_ref[0, 2 * s + 1, 0] = x[:, HEAD_DIM:]
        col += LANES
    for s in range(hkv // 2):
        t = z_ref[0, :, col:col + LANES].T.astype(BF16)
        tail = (lax.broadcasted_iota(jnp.int32, (V_ROWS - HEAD_DIM, ts), 0) == 0).astype(BF16)
        for hh in range(2):
            vt_ref[0, 2 * s + hh, 0, 0:HEAD_DIM, :] = t[hh * HEAD_DIM:(hh + 1) * HEAD_DIM]
            vt_ref[0, 2 * s + hh, 0, HEAD_DIM:V_ROWS, :] = tail
        col += LANES
    for s in range(hd // 2):
        x = (z_ref[0, :, col:col + LANES] * ATTN_SCALE).astype(BF16)
        qd_ref[0, 2 * s] = x[:, 0:HEAD_DIM]
        qd_ref[0, 2 * s + 1] = x[:, HEAD_DIM:]
        col += LANES
    for s in range(hd // 2):
        t = z_ref[0, :, col:col + LANES].T.astype(BF16)
        for c in range(ts // dchunk):
            ktd_ref[0, 2 * s, c] = t[0:HEAD_DIM, c * dchunk:(c + 1) * dchunk]
            ktd_ref[0, 2 * s + 1, c] = t[HEAD_DIM:, c * dchunk:(c + 1) * dchunk]
        col += LANES
    for s in range(hd // 2):
        x = z_ref[0, :, col:col + LANES].astype(BF16)
        for c in range(ts // dchunk):
            vd_ref[0, 2 * s, c] = x[c * dchunk:(c + 1) * dchunk, 0:HEAD_DIM]
            vd_ref[0, 2 * s + 1, c] = x[c * dchunk:(c + 1) * dchunk, HEAD_DIM:]
        col += LANES


def _attn_prep(z, cos, sin, q_g, k_g, hq, hkv, hd):
    b, s, n = z.shape
    ts = min(s, 512)
    dchunk = 256
    nt = s // ts
    hdim = HEAD_DIM
    gq = jnp.tile(q_g, 2).reshape(1, LANES)
    gk = jnp.tile(k_g, 2).reshape(1, LANES)
    li = jnp.arange(LANES)
    grp = (li[:, None] // hdim == li[None, :] // hdim).astype(F32)
    kern = functools.partial(_prep_kernel, hq=hq, hkv=hkv, hd=hd, ts=ts, dchunk=dchunk)
    const = lambda shape: pl.BlockSpec(shape, lambda bi, i: (0,) * len(shape))
    return pl.pallas_call(
        kern,
        grid=(b, nt),
        in_specs=[
            pl.BlockSpec((1, ts, n), lambda bi, i: (bi, i, 0)),
            pl.BlockSpec((ts, LANES), lambda bi, i: (i, 0)),
            pl.BlockSpec((ts, LANES), lambda bi, i: (i, 0)),
            const((1, LANES)), const((1, LANES)), const((LANES, LANES)),
        ],
        out_specs=[
            pl.BlockSpec((1, hq, hdim, ts), lambda bi, i: (bi, 0, 0, i)),
            pl.BlockSpec((1, hkv, 1, ts, hdim), lambda bi, i: (bi, 0, i, 0, 0)),
            pl.BlockSpec((1, hkv, 1, V_ROWS, ts), lambda bi, i: (bi, 0, i, 0, 0)),
            pl.BlockSpec((1, hd, ts, hdim), lambda bi, i: (bi, 0, i, 0)),
            pl.BlockSpec((1, hd, ts // dchunk, hdim, dchunk), lambda bi, i: (bi, 0, i, 0, 0)),
            pl.BlockSpec((1, hd, ts // dchunk, dchunk, hdim), lambda bi, i: (bi, 0, i, 0, 0)),
        ],
        out_shape=[
            jax.ShapeDtypeStruct((b, hq, hdim, s), BF16),
            jax.ShapeDtypeStruct((b, hkv, nt, ts, hdim), BF16),
            jax.ShapeDtypeStruct((b, hkv, nt, V_ROWS, ts), BF16),
            jax.ShapeDtypeStruct((b, hd, s, hdim), BF16),
            jax.ShapeDtypeStruct((b, hd, s // dchunk, hdim, dchunk), BF16),
            jax.ShapeDtypeStruct((b, hd, s // dchunk, dchunk, hdim), BF16),
        ],
        compiler_params=_cparams("parallel", "parallel"),
        name="attn_prep",
    )(z, cos, sin, gq, gk, grp)


def _rope_tables(s):
    t = jnp.arange(s)
    row = (t // GRID_W).astype(F32)
    colp = (t % GRID_W).astype(F32)
    axis_dim = HEAD_DIM // 2
    inv = ROPE_THETA ** (-jnp.arange(0, axis_dim, 2, dtype=F32) / axis_dim)
    ar = row[:, None] * inv
    ac = colp[:, None] * inv
    cr, sr, cc, sc = jnp.cos(ar), jnp.sin(ar), jnp.cos(ac), jnp.sin(ac)
    cos = jnp.concatenate([cr, cr, cc, cc], axis=-1)
    sin = jnp.concatenate([-sr, sr, -sc, sc], axis=-1)
    return jnp.tile(cos, (1, 2)), jnp.tile(sin, (1, 2))


def _gqa_kernel(qt_ref, k_ref, vt_ref, kc_ref, vtc_ref, o_ref, acc_ref, s_ref, *, group):
    n_chunks = k_ref.shape[2]
    tq = qt_ref.shape[3]

    def q_all():
        return jnp.concatenate([qt_ref[0, g] for g in range(group)], axis=1)

    def scores(kb, slot):
        st = jnp.dot(kb, q_all(), preferred_element_type=F32)
        s_ref[slot] = st
        return jnp.max(st, axis=0, keepdims=True)

    def absorb(slot, vb, cmax, m):
        m_new = jnp.maximum(m, cmax)
        alpha = jnp.exp2(m - m_new)
        p = jnp.exp2(s_ref[slot] - m_new)
        acc_ref[...] = alpha * acc_ref[...] + jnp.dot(vb, p.astype(BF16), preferred_element_type=F32)
        return m_new

    st = jnp.dot(kc_ref[0, 0], q_all(), preferred_element_type=F32)
    m = jnp.max(st, axis=0, keepdims=True)
    p = jnp.exp2(st - m)
    acc_ref[...] = jnp.dot(vtc_ref[0, 0], p.astype(BF16), preferred_element_type=F32)

    per_trip = 4 if n_chunks % 4 == 0 else 2
    cm = scores(k_ref[0, 0, 0], 0)

    def body(i, carry):
        m, cm_cur = carry
        j0 = per_trip * i
        for u in range(per_trip):
            cm_next = scores(k_ref[0, 0, jnp.minimum(j0 + u + 1, n_chunks - 1)], (u + 1) % 2)
            m = absorb(u % 2, vt_ref[0, 0, j0 + u], cm_cur, m)
            cm_cur = cm_next
        return m, cm_cur

    lax.fori_loop(0, n_chunks // per_trip, body, (m, cm))
    out = acc_ref[0:HEAD_DIM, :] / acc_ref[HEAD_DIM:HEAD_DIM + 1, :]
    for g in range(group):
        o_ref[0, g] = out[:, g * tq:(g + 1) * tq].astype(BF16)


def _gqa(qt, k, vt, k_ctx, vt_ctx):
    b, hq, hdim, s = qt.shape
    hkv, n_chunks, tk = k.shape[1], k.shape[2], k.shape[3]
    n_ctx = k_ctx.shape[2]
    group = hq // hkv
    tq = min(s, 256)
    return pl.pallas_call(
        functools.partial(_gqa_kernel, group=group),
        grid=(b, hkv, s // tq),
        in_specs=[
            pl.BlockSpec((1, group, hdim, tq), lambda bi, h, i: (bi, h, 0, i)),
            pl.BlockSpec((1, 1, n_chunks, tk, hdim), lambda bi, h, i: (bi, h, 0, 0, 0)),
            pl.BlockSpec((1, 1, n_chunks, V_ROWS, tk), lambda bi, h, i: (bi, h, 0, 0, 0)),
            pl.BlockSpec((1, 1, n_ctx, hdim), lambda bi, h, i: (bi, h, 0, 0)),
            pl.BlockSpec((1, 1, V_ROWS, n_ctx), lambda bi, h, i: (bi, h, 0, 0)),
        ],
        out_specs=pl.BlockSpec((1, group, hdim, tq), lambda bi, h, i: (bi, h, 0, i)),
        out_shape=jax.ShapeDtypeStruct((b, hq, hdim, s), BF16),
        scratch_shapes=[pltpu.VMEM((V_ROWS, group * tq), F32), pltpu.VMEM((2, tk, group * tq), F32)],
        compiler_params=_cparams("parallel", "parallel", "parallel"),
        name="gqa_attention",
    )(qt, k, vt, k_ctx, vt_ctx)


def _ctx_gqa_kernel(qt_ref, k_ref, vt_ref, o_ref):
    st = jnp.dot(k_ref[0, 0], qt_ref[0, 0], preferred_element_type=F32)
    p = jnp.exp2(st - jnp.max(st, axis=0, keepdims=True))
    acc = jnp.dot(vt_ref[0, 0], p.astype(BF16), preferred_element_type=F32)
    o_ref[0, 0] = (acc[0:HEAD_DIM] / acc[HEAD_DIM:HEAD_DIM + 1]).astype(BF16)


def _ctx_gqa(qt, k, vt):
    b, hq, hdim, n = qt.shape
    group = hq // k.shape[1]
    return pl.pallas_call(
        _ctx_gqa_kernel,
        grid=(b, hq),
        in_specs=[
            pl.BlockSpec((1, 1, hdim, n), lambda bi, h: (bi, h, 0, 0)),
            pl.BlockSpec((1, 1, n, hdim), lambda bi, h: (bi, h // group, 0, 0)),
            pl.BlockSpec((1, 1, V_ROWS, n), lambda bi, h: (bi, h // group, 0, 0)),
        ],
        out_specs=pl.BlockSpec((1, 1, hdim, n), lambda bi, h: (bi, h, 0, 0)),
        out_shape=jax.ShapeDtypeStruct((b, hq, hdim, n), BF16),
        compiler_params=_cparams("parallel", "parallel"),
        name="ctx_gqa",
    )(qt, k, vt)


def _ctx_mha_kernel(q_ref, kt_ref, v_ref, o_ref):
    s = jnp.dot(q_ref[0, 0], kt_ref[0, 0], preferred_element_type=F32)
    p = jnp.exp(s - jnp.max(s, axis=-1, keepdims=True))
    l = jnp.sum(p, axis=-1, keepdims=True)
    acc = jnp.dot(p.astype(BF16), v_ref[0, 0], preferred_element_type=F32)
    o_ref[0, 0] = (acc / l).T.astype(BF16)


def _ctx_mha(q, kt, v):
    b, h, n, hdim = q.shape
    return pl.pallas_call(
        _ctx_mha_kernel,
        grid=(b, h),
        in_specs=[
            pl.BlockSpec((1, 1, n, hdim), lambda bi, hi: (bi, hi, 0, 0)),
            pl.BlockSpec((1, 1, hdim, n), lambda bi, hi: (bi, hi, 0, 0)),
            pl.BlockSpec((1, 1, n, hdim), lambda bi, hi: (bi, hi, 0, 0)),
        ],
        out_specs=pl.BlockSpec((1, 1, hdim, n), lambda bi, hi: (bi, hi, 0, 0)),
        out_shape=jax.ShapeDtypeStruct((b, h, hdim, n), BF16),
        compiler_params=_cparams("parallel", "parallel"),
        name="ctx_mha",
    )(q, kt, v)


def _na_kernel(q_ref, kt_ref, v_ref, ktc_ref, vc_ref, bias_ref, o_ref, *, n_band):
    i = pl.program_id(2)
    nq = pl.num_programs(2)
    c0 = jnp.clip(i - 1, 0, nq - n_band)
    q = q_ref[0, 0]
    s_ctx = jnp.dot(q, ktc_ref[0, 0], preferred_element_type=F32)
    m = jnp.max(s_ctx, axis=-1, keepdims=True)
    s_lat = []
    for c in range(n_band):
        cw = kt_ref.shape[4]
        sl = jnp.dot(q, kt_ref[0, 0, c0 + c], preferred_element_type=F32)
        sl = sl + bias_ref[0, 0, :, c * cw:(c + 1) * cw]
        m = jnp.maximum(m, jnp.max(sl, axis=-1, keepdims=True))
        s_lat.append(sl)
    p = jnp.exp(s_ctx - m)
    l = jnp.sum(p, axis=-1, keepdims=True)
    acc = jnp.dot(p.astype(BF16), vc_ref[0, 0], preferred_element_type=F32)
    for c in range(n_band):
        p = jnp.exp(s_lat[c] - m)
        l = l + jnp.sum(p, axis=-1, keepdims=True)
        acc = acc + jnp.dot(p.astype(BF16), v_ref[0, 0, c0 + c], preferred_element_type=F32)
    o_ref[0, 0] = (acc / l).T.astype(BF16)


def _na_bias_table(rel_bias, rows):
    w = GRID_W
    h, n_rel_rows, n_rel_cols = rel_bias.shape
    qcol = np.arange(w)
    cstart = np.clip(qcol - NA_WIN_COLS // 2, 0, w - NA_WIN_COLS)
    in_win = (qcol[None, :] >= cstart[:, None]) & (qcol[None, :] < cstart[:, None] + NA_WIN_COLS)
    col_idx = np.clip(qcol[None, :] - qcol[:, None] + NA_WIN_COLS - 1, 0, n_rel_cols - 1)
    onehot = (col_idx.reshape(-1)[None, :] == np.arange(n_rel_cols)[:, None]).astype(np.float32)
    cols = jnp.einsum('hrj,jn->hrn', rel_bias, onehot, precision=HIGHEST).reshape(h, n_rel_rows, w, w)
    cols = jnp.where(in_win[None, None], cols, NEG_BIG)
    ext = jnp.concatenate([cols, jnp.full((h, 1, w, w), NEG_BIG, F32)], axis=1)
    a = np.arange(NA_Q_ROWS)[:, None]
    c = np.arange(NA_BAND_ROWS)[None, :]
    tables = []
    for r0, rs0 in ((0, 0), (NA_Q_ROWS, 0), (rows - NA_Q_ROWS, rows - NA_BAND_ROWS)):
        r = r0 + a
        kr = rs0 + c
        rs = np.clip(r - NA_WIN_ROWS // 2, 0, rows - NA_WIN_ROWS)
        valid = (kr >= rs) & (kr < rs + NA_WIN_ROWS)
        row_idx = np.where(valid, kr - r + NA_WIN_ROWS - 1, n_rel_rows)
        bias = ext[:, row_idx.reshape(-1)].reshape(h, NA_Q_ROWS, NA_BAND_ROWS, w, w)
        tables.append(bias.transpose(0, 1, 3, 2, 4).reshape(h, NA_Q_ROWS * w, NA_BAND_ROWS * w))
    return jnp.stack(tables, axis=1).astype(F32)


def _na(q, kt, v, kt_ctx, v_ctx, bias):
    b, h, s, hdim = q.shape
    n_chunks, cw = kt.shape[2], kt.shape[4]
    n_ctx = v_ctx.shape[2]
    tq = NA_Q_ROWS * GRID_W
    nq = s // tq
    n_band = NA_BAND_ROWS * GRID_W // cw
    band_w = NA_BAND_ROWS * GRID_W
    return pl.pallas_call(
        functools.partial(_na_kernel, n_band=n_band),
        grid=(b, h, nq),
        in_specs=[
            pl.BlockSpec((1, 1, tq, hdim), lambda bi, hi, i: (bi, hi, i, 0)),
            pl.BlockSpec((1, 1, n_chunks, hdim, cw), lambda bi, hi, i: (bi, hi, 0, 0, 0)),
            pl.BlockSpec((1, 1, n_chunks, cw, hdim), lambda bi, hi, i: (bi, hi, 0, 0, 0)),
            pl.BlockSpec((1, 1, hdim, n_ctx), lambda bi, hi, i: (bi, hi, 0, 0)),
            pl.BlockSpec((1, 1, n_ctx, hdim), lambda bi, hi, i: (bi, hi, 0, 0)),
            pl.BlockSpec((1, 1, tq, band_w),
                         lambda bi, hi, i: (hi, jnp.where(i == 0, 0, jnp.where(i == nq - 1, 2, 1)), 0, 0)),
        ],
        out_specs=pl.BlockSpec((1, 1, hdim, tq), lambda bi, hi, i: (bi, hi, 0, i)),
        out_shape=jax.ShapeDtypeStruct((b, h, hdim, s), BF16),
        compiler_params=_cparams("parallel", "parallel", "parallel"),
        name="na_attention",
    )(q, kt, v, kt_ctx, v_ctx, bias)


def kernel(x, c, ctx, c_ctx, ada_w, ada_b, ln1_g, ln1_b, ln2_g, ln2_b, conv_w_in, conv_w_out, conv_a_dw_w, conv_a_dw_b, conv_a_ln_g, conv_a_ln_b, conv_b_dw_w, attn_w_in, attn_w_out, q_norm_g, k_norm_g, na_rel_bias, router_w1, router_b1, router_w2, router_b2, expert_w_gate, expert_w_up, expert_w_down):
    bsz, seq, d = x.shape
    n_ctx = ctx.shape[1]
    depth = ada_w.shape[0]
    alpha = (2 * depth) ** 0.25
    hq = (d // 2) // HEAD_DIM
    hkv = hq // 4
    hd = (d // 2) // HEAD_DIM

    cond8 = jnp.zeros((8, d), F32).at[:bsz].set(c).at[bsz].set(c_ctx)
    mods = _ada(cond8, ada_w, ada_b)

    def lat_mod(i, k):
        return mods[i, :bsz, k * d:(k + 1) * d].reshape(bsz, 1, d)

    def ctx_mod(i, k):
        return jnp.broadcast_to(mods[i, bsz, k * d:(k + 1) * d].reshape(1, 1, d), (bsz, 1, d))

    wr = jnp.zeros((depth, d, ROUTER_COLS), F32)
    wr = wr.at[:, :, :N_GROUPS].set(router_w1).at[:, :, N_GROUPS:N_GROUPS + N_EXPERTS].set(router_w2)
    br = jnp.zeros((depth, 1, ROUTER_COLS), F32)
    br = br.at[:, 0, :N_GROUPS].set(router_b1).at[:, 0, N_GROUPS:N_GROUPS + N_EXPERTS].set(router_b2)

    cos_lat, sin_lat = _rope_tables(seq)
    cos_ctx = jnp.ones((n_ctx, LANES), F32)
    sin_ctx = jnp.zeros((n_ctx, LANES), F32)

    x_lat, x_ctx = x, ctx
    for i in range(depth):
        last = i == depth - 1
        odd = i % 2 == 1
        j = i // 2
        use_ctx = odd or not last
        if odd:
            w_in = attn_w_in[j].astype(BF16)
            w_out = attn_w_out[j].astype(BF16)
            z_lat = _mod_matmul(x_lat, lat_mod(i, 1), lat_mod(i, 0), w_in)
            z_ctx = _mod_matmul(x_ctx, ctx_mod(i, 1), ctx_mod(i, 0), w_in)
            qt, kc, vtc, qd, ktd, vd = _attn_prep(z_lat, cos_lat, sin_lat, q_norm_g[j], k_norm_g[j], hq, hkv, hd)
            cqt, ckc, cvtc, cqd, cktd, cvd = _attn_prep(z_ctx, cos_ctx, sin_ctx, q_norm_g[j], k_norm_g[j],
                                                        hq, hkv, hd)
            ckc = ckc.reshape(bsz, hkv, n_ctx, HEAD_DIM)
            cvtc = cvtc.reshape(bsz, hkv, V_ROWS, n_ctx)
            cktd = cktd.reshape(bsz, hd, HEAD_DIM, n_ctx)
            cvd = cvd.reshape(bsz, hd, n_ctx, HEAD_DIM)
            ot_c = _gqa(qt, kc, vtc, ckc, cvtc)
            bias = _na_bias_table(na_rel_bias[j], seq // GRID_W)
            o_d = _na(qd, ktd, vd, cktd, cvd, bias)
            mix_lat = (ot_c, o_d)
            if not last:
                mix_ctx = (_ctx_gqa(cqt, ckc, cvtc), _ctx_mha(cqd, cktd, cvd))
        else:
            w_in = conv_w_in[j].astype(BF16)
            w_out = conv_w_out[j].astype(BF16)
            conv_w = (conv_a_dw_w[j], conv_a_dw_b[j], conv_a_ln_g[j], conv_a_ln_b[j], conv_b_dw_w[j])
            z_lat = _mod_matmul(x_lat, lat_mod(i, 1), lat_mod(i, 0), w_in)
            mix_lat = _conv_mixers(z_lat, *conv_w)
            if not last:
                z_ctx = _mod_matmul(x_ctx, ctx_mod(i, 1), ctx_mod(i, 0), w_in)
                mix_ctx = _conv_mixers(z_ctx, *conv_w)

        n_lat = bsz * seq
        n_tok = n_lat if last else n_lat + bsz * n_ctx
        bufs = None if last else (jnp.zeros((n_tok, d), F32), jnp.zeros((n_tok, ROUTER_COLS), F32))
        x_lat, h_all, lg_all = _proj_ln(mix_lat, w_out, x_lat, lat_mod(i, 2), ln1_g[i], ln1_b[i],
                                        lat_mod(i, 4), lat_mod(i, 3), wr[i], br[i], alpha, n_tok, 0, bufs=bufs)
        moe_w = (expert_w_gate[i], expert_w_up[i], expert_w_down[i])
        if last:
            ys, dest, gates = _moe(h_all, lg_all, *moe_w)
            x_lat = _moe_ln(x_lat, ys, dest, gates, 0, lat_mod(i, 5), ln2_g[i], ln2_b[i], alpha)
        else:
            x_ctx, h_all, lg_all = _proj_ln(mix_ctx, w_out, x_ctx, ctx_mod(i, 2), ln1_g[i], ln1_b[i],
                                            ctx_mod(i, 4), ctx_mod(i, 3), wr[i], br[i], alpha, n_tok, n_lat,
                                            bufs=(h_all, lg_all))
            ys, dest, gates = _moe(h_all, lg_all, *moe_w)
            x_lat = _moe_ln(x_lat, ys, dest, gates, 0, lat_mod(i, 5), ln2_g[i], ln2_b[i], alpha)
            x_ctx = _moe_ln(x_ctx, ys, dest, gates, n_lat, ctx_mod(i, 5), ln2_g[i], ln2_b[i], alpha)
    return x_lat
```

```python
import functools
import math

import jax
import jax.numpy as jnp
import numpy as np
from jax import lax
from jax.experimental import pallas as pl
from jax.experimental.pallas import tpu as pltpu

F32 = jnp.float32
BF16 = jnp.bfloat16
HIGHEST = lax.Precision.HIGHEST

GRID_W = 64
HEAD_DIM = 64
V_ROWS = HEAD_DIM + 16
ATTN_SCALE = HEAD_DIM ** -0.5
LOG2E = math.log2(math.e)
CONV_A_WIDTH = 31
CONV_B_WIDTH = 3
CONV_HALO = 16
NA_WIN_ROWS = 8
NA_WIN_COLS = 16
NA_Q_ROWS = 4
NA_BAND_ROWS = 12
NA_BLOCKS_PER_STEP = 4
ROPE_THETA = 10000.0
N_GROUPS = 4
EXPERTS_PER_GROUP = 8
N_EXPERTS = N_GROUPS * EXPERTS_PER_GROUP
LN_EPS = 1e-5
RMS_EPS = 1e-6
NEG_BIG = -1e30

VMEM_LIMIT_BYTES = 48 * 1024 * 1024
LANES = 128
ROUTER_COLS = LANES
EXPERT_BLOCK = 256
RANK_BLOCK = 512


def _cparams(*sem):
    return pltpu.CompilerParams(dimension_semantics=sem, vmem_limit_bytes=VMEM_LIMIT_BYTES)


def _layer_norm(x, g, b):
    mu = jnp.mean(x, axis=-1, keepdims=True)
    xc = x - mu
    var = jnp.mean(xc * xc, axis=-1, keepdims=True)
    return xc * lax.rsqrt(var + LN_EPS) * g + b


def _ada_kernel(c_ref, w_ref, b_ref, o_ref):
    c = c_ref[...]
    s = c * jax.nn.sigmoid(c)
    o_ref[0] = jnp.dot(s, w_ref[0], preferred_element_type=F32, precision=HIGHEST) + b_ref[0]


def _ada(cond8, ada_w, ada_b):
    n_layers, d, n = ada_w.shape
    tn = 1536
    return pl.pallas_call(
        _ada_kernel,
        grid=(n_layers, n // tn),
        in_specs=[
            pl.BlockSpec((8, d), lambda l, j: (0, 0)),
            pl.BlockSpec((1, d, tn), lambda l, j: (l, 0, j)),
            pl.BlockSpec((1, 1, tn), lambda l, j: (l, 0, j)),
        ],
        out_specs=pl.BlockSpec((1, 8, tn), lambda l, j: (l, 0, j)),
        out_shape=jax.ShapeDtypeStruct((n_layers, 8, n), F32),
        compiler_params=_cparams("parallel", "parallel"),
        name="ada_mod",
    )(cond8, ada_w, ada_b.reshape(n_layers, 1, n))


def _mod_matmul_kernel(x_ref, sc_ref, sh_ref, w_ref, o_ref):
    h = x_ref[0] * (1.0 + sc_ref[0]) + sh_ref[0]
    o_ref[0] = jnp.dot(h.astype(BF16), w_ref[...], preferred_element_type=F32)


def _mod_matmul(x, sc, sh, w):
    b, s, d = x.shape
    n = w.shape[1]
    tm = min(s, 512)
    return pl.pallas_call(
        _mod_matmul_kernel,
        grid=(b, s // tm),
        in_specs=[
            pl.BlockSpec((1, tm, d), lambda bi, i: (bi, i, 0)),
            pl.BlockSpec((1, 1, d), lambda bi, i: (bi, 0, 0)),
            pl.BlockSpec((1, 1, d), lambda bi, i: (bi, 0, 0)),
            pl.BlockSpec((d, n), lambda bi, i: (0, 0)),
        ],
        out_specs=pl.BlockSpec((1, tm, n), lambda bi, i: (bi, i, 0)),
        out_shape=jax.ShapeDtypeStruct((b, s, n), F32),
        compiler_params=_cparams("parallel", "parallel"),
        name="mod_matmul",
    )(x, sc, sh, w)


def _conv_kernel(z_ref, zp_ref, zn_ref, aw_ref, ab_ref, ag_ref, abb_ref, bw_ref,
                 o_ref, ea_ref, eb_ref, sh_ref, *, tile, ch):
    i = pl.program_id(1)
    n = pl.num_programs(1)
    halo = CONV_HALO

    def glu(zz):
        return zz[:, 0:ch] * jax.nn.sigmoid(zz[:, ch:2 * ch])

    def cv(zz):
        return zz[:, 3 * ch:4 * ch] * zz[:, 4 * ch:5 * ch]

    pmask = (i > 0).astype(F32)
    nmask = (i < n - 1).astype(F32)
    zp = zp_ref[0]
    zn = zn_ref[0]
    ea_ref[0:halo, :] = glu(zp) * pmask
    eb_ref[0:halo, :] = cv(zp) * pmask
    ea_ref[halo + tile:2 * halo + tile, :] = glu(zn) * nmask
    eb_ref[halo + tile:2 * halo + tile, :] = cv(zn) * nmask
    ea_ref[halo:halo + tile, :] = glu(z_ref[0])
    eb_ref[halo:halo + tile, :] = cv(z_ref[0])

    rows = 32
    pad_a = CONV_A_WIDTH // 2
    pad_b = CONV_B_WIDTH // 2
    n_sh = sh_ref.shape[1]
    piece = 40
    for b in range(1, 8):
        for r in range(0, n_sh, piece):
            sh_ref[b - 1, r:r + piece, :] = ea_ref[pl.ds(r + b, piece), :]
    for r0 in range(0, tile, rows):
        acc = jnp.zeros((rows, ch), F32) + ab_ref[...]
        for k in range(CONV_A_WIDTH):
            a8, b = divmod(halo + k - pad_a, 8)
            if b == 0:
                tap = ea_ref[pl.ds(r0 + 8 * a8, rows), :]
            else:
                tap = sh_ref[b - 1, pl.ds(r0 + 8 * a8, rows), :]
            acc = acc + aw_ref[k:k + 1, :] * tap
        ya = _layer_norm(acc, ag_ref[...], abb_ref[...])
        ya = ya * jax.nn.sigmoid(ya)
        cb = jnp.zeros((rows, ch), F32)
        for k in range(CONV_B_WIDTH):
            cb = cb + bw_ref[k:k + 1, :] * eb_ref[pl.ds(halo + r0 + k - pad_b, rows), :]
        yb = z_ref[0, r0:r0 + rows, 2 * ch:3 * ch] * cb
        o_ref[0, r0:r0 + rows, 0:ch] = ya.astype(BF16)
        o_ref[0, r0:r0 + rows, ch:2 * ch] = yb.astype(BF16)


def _conv_mixers(z, a_dw_w, a_dw_b, a_ln_g, a_ln_b, b_dw_w):
    b, s, n = z.shape
    ch = n // 5
    tile = min(s, 256)
    hb = tile // CONV_HALO
    last_hb = s // CONV_HALO - 1
    kern = functools.partial(_conv_kernel, tile=tile, ch=ch)
    vec = lambda v: v.reshape(1, ch)
    return pl.pallas_call(
        kern,
        grid=(b, s // tile),
        in_specs=[
            pl.BlockSpec((1, tile, n), lambda bi, i: (bi, i, 0)),
            pl.BlockSpec((1, CONV_HALO, n), lambda bi, i: (bi, jnp.maximum(i * hb - 1, 0), 0)),
            pl.BlockSpec((1, CONV_HALO, n), lambda bi, i: (bi, jnp.minimum((i + 1) * hb, last_hb), 0)),
            pl.BlockSpec((CONV_A_WIDTH, ch), lambda bi, i: (0, 0)),
            pl.BlockSpec((1, ch), lambda bi, i: (0, 0)),
            pl.BlockSpec((1, ch), lambda bi, i: (0, 0)),
            pl.BlockSpec((1, ch), lambda bi, i: (0, 0)),
            pl.BlockSpec((CONV_B_WIDTH, ch), lambda bi, i: (0, 0)),
        ],
        out_specs=pl.BlockSpec((1, tile, 2 * ch), lambda bi, i: (bi, i, 0)),
        out_shape=jax.ShapeDtypeStruct((b, s, 2 * ch), BF16),
        scratch_shapes=[pltpu.VMEM((tile + 2 * CONV_HALO, ch), F32),
                        pltpu.VMEM((tile + 2 * CONV_HALO, ch), F32),
                        pltpu.VMEM((7, tile + 2 * CONV_HALO - 8, ch), F32)],
        compiler_params=_cparams("parallel", "parallel"),
        name="conv_mixers",
    )(z, z, z, a_dw_w, vec(a_dw_b), vec(a_ln_g), vec(a_ln_b), b_dw_w)


def _proj_ln_kernel(*refs, n_mix, n_alias, alpha):
    mix_refs = refs[:n_mix]
    (w_ref, x_ref, g_ref, lng_ref, lnb_ref, sc_ref, sh_ref, wrh_ref, wrl_ref, br_ref) = refs[n_mix:n_mix + 10]
    xo_ref, h_ref, lg_ref = refs[n_mix + 10 + n_alias:]
    if n_mix == 2:
        ut = jnp.concatenate([r[0].reshape(r.shape[1] * r.shape[2], r.shape[3]) for r in mix_refs], axis=0)
        y = lax.dot_general(ut, w_ref[...], (((0,), (0,)), ((), ())), preferred_element_type=F32)
    else:
        y = jnp.dot(mix_refs[0][0], w_ref[...], preferred_element_type=F32)
    xn = _layer_norm(alpha * x_ref[0] + g_ref[0] * y, lng_ref[...], lnb_ref[...])
    xo_ref[0] = xn
    h2 = xn * (1.0 + sc_ref[0]) + sh_ref[0]
    h_ref[...] = h2
    h_hi = h2.astype(BF16)
    h_lo = (h2 - h_hi.astype(F32)).astype(BF16)
    lg_ref[...] = (jnp.dot(h_hi, wrh_ref[...], preferred_element_type=F32)
                   + (jnp.dot(h_lo, wrh_ref[...], preferred_element_type=F32)
                      + jnp.dot(h_hi, wrl_ref[...], preferred_element_type=F32))
                   + br_ref[...])


def _proj_ln(mix, w_out, x, g, ln_g, ln_b, sc2, sh2, wr, br, alpha, n_tok, tok_off, bufs=None):
    b, s, d = x.shape
    tm = min(s, 512)
    nt = s // tm
    off = tok_off // tm
    if isinstance(mix, tuple):
        mix_specs = [pl.BlockSpec((1, o.shape[1], HEAD_DIM, tm), lambda bi, i: (bi, 0, 0, i)) for o in mix]
        mix_args = list(mix)
    else:
        mix_specs = [pl.BlockSpec((1, tm, mix.shape[2]), lambda bi, i: (bi, i, 0))]
        mix_args = [mix]
    row = lambda: pl.BlockSpec((1, tm, d), lambda bi, i: (bi, i, 0))
    per_b = lambda: pl.BlockSpec((1, 1, d), lambda bi, i: (bi, 0, 0))
    vec = lambda: pl.BlockSpec((1, d), lambda bi, i: (0, 0))
    alias_args = [] if bufs is None else list(bufs)
    n_in = len(mix_args) + 10
    wr_hi = wr.astype(BF16)
    wr_lo = (wr - wr_hi.astype(F32)).astype(BF16)
    kern = functools.partial(_proj_ln_kernel, n_mix=len(mix_args), n_alias=len(alias_args), alpha=alpha)
    return pl.pallas_call(
        kern,
        grid=(b, nt),
        in_specs=mix_specs + [
            pl.BlockSpec(w_out.shape, lambda bi, i: (0, 0)),
            row(), per_b(), vec(), vec(), per_b(), per_b(),
            pl.BlockSpec((d, ROUTER_COLS), lambda bi, i: (0, 0)),
            pl.BlockSpec((d, ROUTER_COLS), lambda bi, i: (0, 0)),
            pl.BlockSpec((1, ROUTER_COLS), lambda bi, i: (0, 0)),
        ] + [pl.BlockSpec(memory_space=pl.ANY) for _ in alias_args],
        out_specs=[row(),
                   pl.BlockSpec((tm, d), lambda bi, i: (off + bi * nt + i, 0)),
                   pl.BlockSpec((tm, ROUTER_COLS), lambda bi, i: (off + bi * nt + i, 0))],
        out_shape=[jax.ShapeDtypeStruct((b, s, d), F32),
                   jax.ShapeDtypeStruct((n_tok, d), F32),
                   jax.ShapeDtypeStruct((n_tok, ROUTER_COLS), F32)],
        input_output_aliases={n_in: 1, n_in + 1: 2} if alias_args else {},
        compiler_params=_cparams("parallel", "parallel"),
        name="proj_res_ln",
    )(*mix_args, w_out, x, g, ln_g.reshape(1, d), ln_b.reshape(1, d), sc2, sh2, wr_hi, wr_lo, br, *alias_args)


def _rank_kernel(e_ref, tri_ref, rank_ref, cnt_ref, carry_ref):
    @pl.when(pl.program_id(0) == 0)
    def _():
        carry_ref[...] = jnp.zeros_like(carry_ref)

    e = e_ref[0]
    blk = e.shape[1]
    ids = lax.broadcasted_iota(jnp.int32, (N_EXPERTS, blk), 0)
    oh = jnp.where(ids == e, 1.0, 0.0).astype(F32)
    before = jnp.dot(oh.astype(BF16), tri_ref[...], preferred_element_type=F32)
    carry = carry_ref[...]
    before = before + carry[:, 0:1]
    rank_ref[0] = jnp.sum(oh * before, axis=0, keepdims=True).astype(jnp.int32)
    carry = carry + jnp.sum(oh, axis=1, keepdims=True)
    carry_ref[...] = carry
    cnt_ref[...] = carry.astype(jnp.int32)


def _expert_ranks(flat_e):
    m = flat_e.shape[0]
    blk = RANK_BLOCK
    nblk = m // blk
    tri = (lax.broadcasted_iota(jnp.int32, (blk, blk), 0)
           < lax.broadcasted_iota(jnp.int32, (blk, blk), 1)).astype(BF16)
    rank, cnt = pl.pallas_call(
        _rank_kernel,
        grid=(nblk,),
        in_specs=[pl.BlockSpec((1, 1, blk), lambda i: (i, 0, 0)),
                  pl.BlockSpec((blk, blk), lambda i: (0, 0))],
        out_specs=[pl.BlockSpec((1, 1, blk), lambda i: (i, 0, 0)),
                   pl.BlockSpec((N_EXPERTS, LANES), lambda i: (0, 0))],
        out_shape=[jax.ShapeDtypeStruct((nblk, 1, blk), jnp.int32),
                   jax.ShapeDtypeStruct((N_EXPERTS, LANES), jnp.int32)],
        scratch_shapes=[pltpu.VMEM((N_EXPERTS, LANES), F32)],
        compiler_params=_cparams("arbitrary"),
        name="expert_rank",
    )(flat_e.reshape(nblk, 1, blk), tri)
    return rank.reshape(m), cnt[:, 0]


def _to_row_tiles(ref, x, n_rows, tiles):
    for s in range(tiles):
        ref[pl.ds(s, n_rows, stride=tiles), :] = x[:, s * LANES:(s + 1) * LANES]


def _from_row_tiles(ref, n_rows, tiles):
    return jnp.concatenate([ref[pl.ds(s, n_rows, stride=tiles), :] for s in range(tiles)], axis=1)


def _dispatch_kernel(dest_ref, x_ref, init_ref, xs_ref, buf_ref, sem, *, tm, tiles):
    del init_ref
    _to_row_tiles(buf_ref, x_ref[...], tm, tiles)

    def issue(r, c):
        src = buf_ref.at[pl.ds(pl.multiple_of(r * tiles, tiles), tiles), :]
        for k in range(2):
            d = pl.multiple_of(dest_ref[0, 0, 2 * r + k] * tiles, tiles)
            pltpu.make_async_copy(src, xs_ref.at[pl.ds(d, tiles), :], sem).start(priority=k)
        return c

    lax.fori_loop(0, tm, issue, 0, unroll=4)
    for _ in range(2):
        pltpu.make_async_copy(buf_ref, xs_ref.at[pl.ds(0, tm * tiles), :], sem).wait()


def _dispatch(h_all, dest, cap):
    n, d = h_all.shape
    tiles = d // LANES
    tm = 512
    nt = n // tm
    return pl.pallas_call(
        functools.partial(_dispatch_kernel, tm=tm, tiles=tiles),
        grid=(nt,),
        in_specs=[
            pl.BlockSpec((1, 1, 2 * tm), lambda i: (i, 0, 0), memory_space=pltpu.SMEM),
            pl.BlockSpec((tm, d), lambda i: (i, 0)),
            pl.BlockSpec(memory_space=pl.ANY),
        ],
        out_specs=pl.BlockSpec(memory_space=pl.ANY),
        out_shape=jax.ShapeDtypeStruct((cap * tiles, LANES), F32),
        scratch_shapes=[pltpu.VMEM((tm * tiles, LANES), F32), pltpu.SemaphoreType.DMA(())],
        input_output_aliases={2: 0},
        compiler_params=_cparams("arbitrary"),
        name="moe_dispatch",
    )(dest.reshape(nt, 1, 2 * tm), h_all, jnp.zeros((cap * tiles, LANES), F32))


def _experts_kernel(be_ref, nu_ref, x_ref, wg_ref, wu_ref, wd_ref, o_ref, wgb_ref, wub_ref, wdb_ref,
                    *, tb, tiles):
    i = pl.program_id(0)
    changed = jnp.logical_or(i == 0, be_ref[i] != be_ref[jnp.maximum(i - 1, 0)])

    @pl.when(changed)
    def _():
        wgb_ref[...] = wg_ref[0, 0].astype(BF16)
        wub_ref[...] = wu_ref[0, 0].astype(BF16)
        wdb_ref[...] = wd_ref[0, 0].astype(BF16)

    @pl.when(i < nu_ref[0])
    def _():
        x = _from_row_tiles(x_ref, tb, tiles).astype(BF16)
        g = jnp.dot(x, wgb_ref[...], preferred_element_type=F32)
        u = jnp.dot(x, wub_ref[...], preferred_element_type=F32)
        a = (g * jax.nn.sigmoid(g) * u).astype(BF16)
        _to_row_tiles(o_ref, jnp.dot(a, wdb_ref[...], preferred_element_type=F32), tb, tiles)

    @pl.when(i >= nu_ref[0])
    def _():
        o_ref[...] = jnp.zeros_like(o_ref)


def _experts(xs, block_e, n_used, w_gate, w_up, w_down, layer):
    d, hid = w_gate.shape[2], w_gate.shape[3]
    tiles = d // LANES
    tb = EXPERT_BLOCK
    nb = xs.shape[0] // (tb * tiles)
    grid_spec = pltpu.PrefetchScalarGridSpec(
        num_scalar_prefetch=2,
        grid=(nb,),
        in_specs=[
            pl.BlockSpec((tb * tiles, LANES), lambda i, be, nu: (i, 0)),
            pl.BlockSpec((1, 1, d, hid), lambda i, be, nu: (layer, be[i], 0, 0)),
            pl.BlockSpec((1, 1, d, hid), lambda i, be, nu: (layer, be[i], 0, 0)),
            pl.BlockSpec((1, 1, hid, d), lambda i, be, nu: (layer, be[i], 0, 0)),
        ],
        out_specs=pl.BlockSpec((tb * tiles, LANES), lambda i, be, nu: (i, 0)),
        scratch_shapes=[pltpu.VMEM((d, hid), BF16), pltpu.VMEM((d, hid), BF16), pltpu.VMEM((hid, d), BF16)],
    )
    return pl.pallas_call(
        functools.partial(_experts_kernel, tb=tb, tiles=tiles),
        grid_spec=grid_spec,
        out_shape=jax.ShapeDtypeStruct(xs.shape, F32),
        compiler_params=_cparams("arbitrary"),
        name="experts",
    )(block_e, n_used, xs, w_gate, w_up, w_down)


def _moe_ln_kernel(dest_ref, dest_next_ref, x_ref, gt_ref, g_ref, lng_ref, lnb_ref, ys_ref, o_ref,
                   y0_ref, y1_ref, sem0, sem1, *, tm, tiles, alpha):
    i = pl.program_id(0)
    n = pl.num_programs(0)
    slot = i % 2

    def gather(idx_ref, slot):
        def issue(r, c):
            dst = pl.ds(pl.multiple_of(r * tiles, tiles), tiles)
            d0 = pl.multiple_of(idx_ref[0, 0, 2 * r] * tiles, tiles)
            d1 = pl.multiple_of(idx_ref[0, 0, 2 * r + 1] * tiles, tiles)
            pltpu.make_async_copy(ys_ref.at[pl.ds(d0, tiles), :], y0_ref.at[slot, dst, :],
                                  sem0.at[slot]).start(priority=0)
            pltpu.make_async_copy(ys_ref.at[pl.ds(d1, tiles), :], y1_ref.at[slot, dst, :],
                                  sem1.at[slot]).start(priority=1)
            return c

        lax.fori_loop(0, tm, issue, 0, unroll=4)

    @pl.when(i == 0)
    def _():
        gather(dest_ref, 0)

    @pl.when(i + 1 < n)
    def _():
        gather(dest_next_ref, 1 - slot)

    pltpu.make_async_copy(ys_ref.at[pl.ds(0, tm * tiles), :], y0_ref.at[slot], sem0.at[slot]).wait()
    pltpu.make_async_copy(ys_ref.at[pl.ds(0, tm * tiles), :], y1_ref.at[slot], sem1.at[slot]).wait()
    gt = gt_ref[...]
    y = (gt[:, 0:1] * _from_row_tiles(y0_ref.at[slot], tm, tiles)
         + gt[:, 1:2] * _from_row_tiles(y1_ref.at[slot], tm, tiles))
    o_ref[...] = _layer_norm(alpha * x_ref[...] + g_ref[0] * y, lng_ref[...], lnb_ref[...])


def _moe_ln(x, ys, dest, gates, tok_off, g, ln_g, ln_b, alpha):
    b, s, d = x.shape
    tiles = d // LANES
    n_tok = gates.shape[0]
    tm = min(s, 512)
    nt = s // tm
    off = tok_off // tm
    n_steps = b * nt
    vec = lambda: pl.BlockSpec((1, d), lambda i: (0, 0))
    dest_blocks = dest.reshape(n_tok // tm, 1, 2 * tm)
    out = pl.pallas_call(
        functools.partial(_moe_ln_kernel, tm=tm, tiles=tiles, alpha=alpha),
        grid=(n_steps,),
        in_specs=[
            pl.BlockSpec((1, 1, 2 * tm), lambda i: (off + i, 0, 0), memory_space=pltpu.SMEM),
            pl.BlockSpec((1, 1, 2 * tm), lambda i: (off + jnp.minimum(i + 1, n_steps - 1), 0, 0),
                         memory_space=pltpu.SMEM),
            pl.BlockSpec((tm, d), lambda i: (i, 0)),
            pl.BlockSpec((tm, 2), lambda i: (off + i, 0)),
            pl.BlockSpec((1, 1, d), lambda i: (i // nt, 0, 0)),
            vec(), vec(),
            pl.BlockSpec(memory_space=pl.ANY),
        ],
        out_specs=pl.BlockSpec((tm, d), lambda i: (i, 0)),
        out_shape=jax.ShapeDtypeStruct((b * s, d), F32),
        scratch_shapes=[pltpu.VMEM((2, tm * tiles, LANES), F32), pltpu.VMEM((2, tm * tiles, LANES), F32),
                        pltpu.SemaphoreType.DMA((2,)), pltpu.SemaphoreType.DMA((2,))],
        compiler_params=_cparams("arbitrary"),
        name="moe_res_ln",
    )(dest_blocks, dest_blocks, x.reshape(b * s, d), gates, g, ln_g.reshape(1, d), ln_b.reshape(1, d), ys)
    return out.reshape(b, s, d)


def _route(logits):
    n = logits.shape[0]
    l1 = logits[:, :N_GROUPS]
    grp = jnp.argmax(l1, axis=-1).astype(jnp.int32)
    p_grp = 1.0 / jnp.sum(jnp.exp(l1 - jnp.max(l1, axis=-1, keepdims=True)), axis=-1)
    lg2 = logits[:, N_GROUPS:N_GROUPS + N_EXPERTS].reshape(n, N_GROUPS, EXPERTS_PER_GROUP)
    gsel = grp[:, None] == jnp.arange(N_GROUPS, dtype=jnp.int32)[None, :]
    sel = jnp.sum(jnp.where(gsel[:, :, None], lg2, 0.0), axis=1)
    i1 = jnp.argmax(sel, axis=-1).astype(jnp.int32)
    v1 = jnp.max(sel, axis=-1)
    col = jnp.arange(EXPERTS_PER_GROUP, dtype=jnp.int32)[None, :]
    sel2 = jnp.where(col == i1[:, None], -jnp.inf, sel)
    i2 = jnp.argmax(sel2, axis=-1).astype(jnp.int32)
    v2 = jnp.max(sel2, axis=-1)
    e2 = jnp.exp(v2 - v1)
    w1 = 1.0 / (1.0 + e2)
    gates = p_grp[:, None] * jnp.stack([w1, e2 * w1], axis=-1)
    experts = grp[:, None] * EXPERTS_PER_GROUP + jnp.stack([i1, i2], axis=-1)
    return experts, gates.astype(F32)


def _moe(h_all, logits, w_gate, w_up, w_down, layer):
    n, d = h_all.shape
    m = 2 * n
    tb = EXPERT_BLOCK
    experts, gates = _route(logits)
    flat_e = experts.reshape(m)
    rank, counts = _expert_ranks(flat_e)
    padded = (counts + tb - 1) // tb * tb
    pad_end = jnp.cumsum(padded)
    pad_start = pad_end - padded
    onehot = flat_e[:, None] == jnp.arange(N_EXPERTS, dtype=jnp.int32)[None, :]
    dest = jnp.sum(jnp.where(onehot, pad_start[None, :], 0), axis=-1).astype(jnp.int32) + rank
    nb = -(-m // tb) + N_EXPERTS
    cap = nb * tb
    blk_start = jnp.arange(nb, dtype=jnp.int32) * tb
    block_e = jnp.minimum(jnp.sum(pad_end[None, :] <= blk_start[:, None], axis=-1), N_EXPERTS - 1)
    n_used = (pad_end[-1] // tb).reshape(1)

    xs = _dispatch(h_all, dest, cap)
    ys = _experts(xs, block_e.astype(jnp.int32), n_used.astype(jnp.int32), w_gate, w_up, w_down, layer)
    return ys, dest, gates


def _prep_kernel(z_ref, cos_ref, sin_ref, gq_ref, gk_ref, grp_ref,
                 qt_ref, kc_ref, vt_ref, qtd_ref, kd_ref, vtd_ref, *, hq, hkv, hd, ts, dchunk):
    cos = cos_ref[...]
    sin = sin_ref[...]
    lane = lax.broadcasted_iota(jnp.int32, (ts, LANES), 1)
    first_half = (lane % 32) < 16

    def norm_rope(x, g):
        ss = jnp.dot(x * x, grp_ref[...], preferred_element_type=F32, precision=HIGHEST)
        xn = x * lax.rsqrt(ss * (1.0 / HEAD_DIM) + RMS_EPS) * g
        partner = jnp.where(first_half, pltpu.roll(xn, LANES - 16, 1), pltpu.roll(xn, 16, 1))
        return xn * cos + partner * sin

    col = 0
    for s in range(hq // 2):
        x = norm_rope(z_ref[0, :, col:col + LANES], gq_ref[...]) * (ATTN_SCALE * LOG2E)
        t = x.T.astype(BF16)
        qt_ref[0, 2 * s] = t[0:HEAD_DIM]
        qt_ref[0, 2 * s + 1] = t[HEAD_DIM:]
        col += LANES
    for s in range(hkv // 2):
        x = norm_rope(z_ref[0, :, col:col + LANES], gk_ref[...]).astype(BF16)
        kc_ref[0, 2 * s, 0] = x[:, 0:HEAD_DIM]
        kc_ref[0, 2 * s + 1, 0] = x[:, HEAD_DIM:]
        col += LANES
    tail = (lax.broadcasted_iota(jnp.int32, (V_ROWS - HEAD_DIM, ts), 0) == 0).astype(BF16)
    tail_d = (lax.broadcasted_iota(jnp.int32, (V_ROWS - HEAD_DIM, dchunk), 0) == 0).astype(BF16)
    for s in range(hkv // 2):
        t = z_ref[0, :, col:col + LANES].T.astype(BF16)
        for hh in range(2):
            vt_ref[0, 2 * s + hh, 0, 0:HEAD_DIM, :] = t[hh * HEAD_DIM:(hh + 1) * HEAD_DIM]
            vt_ref[0, 2 * s + hh, 0, HEAD_DIM:V_ROWS, :] = tail
        col += LANES
    for s in range(hd // 2):
        t = (z_ref[0, :, col:col + LANES] * (ATTN_SCALE * LOG2E)).T.astype(BF16)
        qtd_ref[0, 2 * s] = t[0:HEAD_DIM]
        qtd_ref[0, 2 * s + 1] = t[HEAD_DIM:]
        col += LANES
    for s in range(hd // 2):
        x = z_ref[0, :, col:col + LANES].astype(BF16)
        for c in range(ts // dchunk):
            kd_ref[0, 2 * s, c] = x[c * dchunk:(c + 1) * dchunk, 0:HEAD_DIM]
            kd_ref[0, 2 * s + 1, c] = x[c * dchunk:(c + 1) * dchunk, HEAD_DIM:]
        col += LANES
    for s in range(hd // 2):
        t = z_ref[0, :, col:col + LANES].T.astype(BF16)
        for c in range(ts // dchunk):
            for hh in range(2):
                vtd_ref[0, 2 * s + hh, c, 0:HEAD_DIM, :] = t[hh * HEAD_DIM:(hh + 1) * HEAD_DIM,
                                                             c * dchunk:(c + 1) * dchunk]
                vtd_ref[0, 2 * s + hh, c, HEAD_DIM:V_ROWS, :] = tail_d
        col += LANES


def _attn_prep(z, cos, sin, q_g, k_g, hq, hkv, hd):
    b, s, n = z.shape
    ts = min(s, 512)
    dchunk = 256
    nt = s // ts
    hdim = HEAD_DIM
    gq = jnp.tile(q_g, 2).reshape(1, LANES)
    gk = jnp.tile(k_g, 2).reshape(1, LANES)
    li = jnp.arange(LANES)
    grp = (li[:, None] // hdim == li[None, :] // hdim).astype(F32)
    kern = functools.partial(_prep_kernel, hq=hq, hkv=hkv, hd=hd, ts=ts, dchunk=dchunk)
    const = lambda shape: pl.BlockSpec(shape, lambda bi, i: (0,) * len(shape))
    return pl.pallas_call(
        kern,
        grid=(b, nt),
        in_specs=[
            pl.BlockSpec((1, ts, n), lambda bi, i: (bi, i, 0)),
            pl.BlockSpec((ts, LANES), lambda bi, i: (i, 0)),
            pl.BlockSpec((ts, LANES), lambda bi, i: (i, 0)),
            const((1, LANES)), const((1, LANES)), const((LANES, LANES)),
        ],
        out_specs=[
            pl.BlockSpec((1, hq, hdim, ts), lambda bi, i: (bi, 0, 0, i)),
            pl.BlockSpec((1, hkv, 1, ts, hdim), lambda bi, i: (bi, 0, i, 0, 0)),
            pl.BlockSpec((1, hkv, 1, V_ROWS, ts), lambda bi, i: (bi, 0, i, 0, 0)),
            pl.BlockSpec((1, hd, hdim, ts), lambda bi, i: (bi, 0, 0, i)),
            pl.BlockSpec((1, hd, ts // dchunk, dchunk, hdim), lambda bi, i: (bi, 0, i, 0, 0)),
            pl.BlockSpec((1, hd, ts // dchunk, V_ROWS, dchunk), lambda bi, i: (bi, 0, i, 0, 0)),
        ],
        out_shape=[
            jax.ShapeDtypeStruct((b, hq, hdim, s), BF16),
            jax.ShapeDtypeStruct((b, hkv, nt, ts, hdim), BF16),
            jax.ShapeDtypeStruct((b, hkv, nt, V_ROWS, ts), BF16),
            jax.ShapeDtypeStruct((b, hd, hdim, s), BF16),
            jax.ShapeDtypeStruct((b, hd, s // dchunk, dchunk, hdim), BF16),
            jax.ShapeDtypeStruct((b, hd, s // dchunk, V_ROWS, dchunk), BF16),
        ],
        compiler_params=_cparams("parallel", "parallel"),
        name="attn_prep",
    )(z, cos, sin, gq, gk, grp)


def _rope_tables(s):
    t = jnp.arange(s)
    row = (t // GRID_W).astype(F32)
    colp = (t % GRID_W).astype(F32)
    axis_dim = HEAD_DIM // 2
    inv = ROPE_THETA ** (-jnp.arange(0, axis_dim, 2, dtype=F32) / axis_dim)
    ar = row[:, None] * inv
    ac = colp[:, None] * inv
    cr, sr, cc, sc = jnp.cos(ar), jnp.sin(ar), jnp.cos(ac), jnp.sin(ac)
    cos = jnp.concatenate([cr, cr, cc, cc], axis=-1)
    sin = jnp.concatenate([-sr, sr, -sc, sc], axis=-1)
    return jnp.tile(cos, (1, 2)), jnp.tile(sin, (1, 2))


def _gqa_kernel(qt_ref, k_ref, vt_ref, kc_ref, vtc_ref, o_ref, acc_ref, s_ref, *, group):
    n_chunks = k_ref.shape[2]
    tq = qt_ref.shape[3]

    def q_all():
        return jnp.concatenate([qt_ref[0, g] for g in range(group)], axis=1)

    def scores(kb, slot):
        st = jnp.dot(kb, q_all(), preferred_element_type=F32)
        s_ref[slot] = st
        return jnp.max(st, axis=0, keepdims=True)

    def absorb(slot, vb, cmax, m):
        m_new = jnp.maximum(m, cmax)
        alpha = jnp.exp2(m - m_new)
        p = jnp.exp2(s_ref[slot] - m_new)
        acc_ref[...] = alpha * acc_ref[...] + jnp.dot(vb, p.astype(BF16), preferred_element_type=F32)
        return m_new

    st = jnp.dot(kc_ref[0, 0], q_all(), preferred_element_type=F32)
    m = jnp.max(st, axis=0, keepdims=True)
    p = jnp.exp2(st - m)
    acc_ref[...] = jnp.dot(vtc_ref[0, 0], p.astype(BF16), preferred_element_type=F32)

    per_trip = 8 if n_chunks % 8 == 0 else 2
    cm = scores(k_ref[0, 0, 0], 0)

    def body(i, carry):
        m, cm_cur = carry
        j0 = per_trip * i
        for u in range(per_trip):
            cm_next = scores(k_ref[0, 0, jnp.minimum(j0 + u + 1, n_chunks - 1)], (u + 1) % 2)
            m = absorb(u % 2, vt_ref[0, 0, j0 + u], cm_cur, m)
            cm_cur = cm_next
        return m, cm_cur

    lax.fori_loop(0, n_chunks // per_trip, body, (m, cm))
    out = acc_ref[0:HEAD_DIM, :] / acc_ref[HEAD_DIM:HEAD_DIM + 1, :]
    for g in range(group):
        o_ref[0, g] = out[:, g * tq:(g + 1) * tq].astype(BF16)


def _gqa(qt, k, vt, k_ctx, vt_ctx):
    b, hq, hdim, s = qt.shape
    hkv, n_chunks, tk = k.shape[1], k.shape[2], k.shape[3]
    n_ctx = k_ctx.shape[2]
    group = hq // hkv
    tq = min(s, 256)
    return pl.pallas_call(
        functools.partial(_gqa_kernel, group=group),
        grid=(b, hkv, s // tq),
        in_specs=[
            pl.BlockSpec((1, group, hdim, tq), lambda bi, h, i: (bi, h, 0, i)),
            pl.BlockSpec((1, 1, n_chunks, tk, hdim), lambda bi, h, i: (bi, h, 0, 0, 0)),
            pl.BlockSpec((1, 1, n_chunks, V_ROWS, tk), lambda bi, h, i: (bi, h, 0, 0, 0)),
            pl.BlockSpec((1, 1, n_ctx, hdim), lambda bi, h, i: (bi, h, 0, 0)),
            pl.BlockSpec((1, 1, V_ROWS, n_ctx), lambda bi, h, i: (bi, h, 0, 0)),
        ],
        out_specs=pl.BlockSpec((1, group, hdim, tq), lambda bi, h, i: (bi, h, 0, i)),
        out_shape=jax.ShapeDtypeStruct((b, hq, hdim, s), BF16),
        scratch_shapes=[pltpu.VMEM((V_ROWS, group * tq), F32), pltpu.VMEM((2, tk, group * tq), F32)],
        compiler_params=_cparams("parallel", "parallel", "parallel"),
        name="gqa_attention",
    )(qt, k, vt, k_ctx, vt_ctx)


def _ctx_gqa_kernel(qt_ref, k_ref, vt_ref, o_ref):
    st = jnp.dot(k_ref[0, 0], qt_ref[0, 0], preferred_element_type=F32)
    p = jnp.exp2(st - jnp.max(st, axis=0, keepdims=True))
    acc = jnp.dot(vt_ref[0, 0], p.astype(BF16), preferred_element_type=F32)
    o_ref[0, 0] = (acc[0:HEAD_DIM] / acc[HEAD_DIM:HEAD_DIM + 1]).astype(BF16)


def _ctx_gqa(qt, k, vt):
    b, hq, hdim, n = qt.shape
    group = hq // k.shape[1]
    return pl.pallas_call(
        _ctx_gqa_kernel,
        grid=(b, hq),
        in_specs=[
            pl.BlockSpec((1, 1, hdim, n), lambda bi, h: (bi, h, 0, 0)),
            pl.BlockSpec((1, 1, n, hdim), lambda bi, h: (bi, h // group, 0, 0)),
            pl.BlockSpec((1, 1, V_ROWS, n), lambda bi, h: (bi, h // group, 0, 0)),
        ],
        out_specs=pl.BlockSpec((1, 1, hdim, n), lambda bi, h: (bi, h, 0, 0)),
        out_shape=jax.ShapeDtypeStruct((b, hq, hdim, n), BF16),
        compiler_params=_cparams("parallel", "parallel"),
        name="ctx_gqa",
    )(qt, k, vt)


def _na_kernel(qt_ref, k_ref, vt_ref, kc_ref, vtc_ref, bias_ref, o_ref, s_ref, *, n_band, nq, per_step):
    step = pl.program_id(2)
    tq = NA_Q_ROWS * GRID_W
    cw = k_ref.shape[3]
    n_ctx = kc_ref.shape[2]

    def scores(j, slot):
        i = step * per_step + j
        c0 = jnp.clip(i - 1, 0, nq - n_band)
        variant = jnp.where(i == 0, 0, jnp.where(i == nq - 1, 2, 1))
        qt = qt_ref[0, 0, :, j * tq:(j + 1) * tq]
        st = jnp.dot(kc_ref[0, 0], qt, preferred_element_type=F32)
        s_ref[slot, 0:n_ctx, :] = st
        m = jnp.max(st, axis=0, keepdims=True)
        for c in range(n_band):
            b0 = pl.multiple_of((variant * n_band + c) * cw, cw)
            st = (jnp.dot(k_ref[0, 0, c0 + c], qt, preferred_element_type=F32)
                  + bias_ref[0, pl.ds(b0, cw), :])
            s_ref[slot, n_ctx + c * cw:n_ctx + (c + 1) * cw, :] = st
            m = jnp.maximum(m, jnp.max(st, axis=0, keepdims=True))
        return m, c0

    def finish(j, slot, m, c0):
        p = jnp.exp2(s_ref[slot, 0:n_ctx, :] - m).astype(BF16)
        acc = jnp.dot(vtc_ref[0, 0], p, preferred_element_type=F32)
        for c in range(n_band):
            p = jnp.exp2(s_ref[slot, n_ctx + c * cw:n_ctx + (c + 1) * cw, :] - m).astype(BF16)
            acc = acc + jnp.dot(vt_ref[0, 0, c0 + c], p, preferred_element_type=F32)
        o_ref[0, 0, :, j * tq:(j + 1) * tq] = (acc[0:HEAD_DIM] / acc[HEAD_DIM:HEAD_DIM + 1]).astype(BF16)

    nxt = scores(0, 0)
    for j in range(per_step):
        cur = nxt
        if j + 1 < per_step:
            nxt = scores(j + 1, (j + 1) % 2)
        finish(j, j % 2, *cur)


def _na_bias_table(rel_bias, rows):
    w = GRID_W
    h, n_rel_rows, n_rel_cols = rel_bias.shape
    qcol = np.arange(w)
    cstart = np.clip(qcol - NA_WIN_COLS // 2, 0, w - NA_WIN_COLS)
    in_win = (qcol[None, :] >= cstart[:, None]) & (qcol[None, :] < cstart[:, None] + NA_WIN_COLS)
    col_idx = np.clip(qcol[None, :] - qcol[:, None] + NA_WIN_COLS - 1, 0, n_rel_cols - 1)
    onehot = (col_idx.reshape(-1)[None, :] == np.arange(n_rel_cols)[:, None]).astype(np.float32)
    cols = jnp.einsum('hrj,jn->hrn', rel_bias, onehot, precision=HIGHEST).reshape(h, n_rel_rows, w, w)
    cols = jnp.where(in_win[None, None], cols, NEG_BIG)
    ext = jnp.concatenate([cols, jnp.full((h, 1, w, w), NEG_BIG, F32)], axis=1)
    a = np.arange(NA_Q_ROWS)[:, None]
    c = np.arange(NA_BAND_ROWS)[None, :]
    tables = []
    for r0, rs0 in ((0, 0), (NA_Q_ROWS, 0), (rows - NA_Q_ROWS, rows - NA_BAND_ROWS)):
        r = r0 + a
        kr = rs0 + c
        rs = np.clip(r - NA_WIN_ROWS // 2, 0, rows - NA_WIN_ROWS)
        valid = (kr >= rs) & (kr < rs + NA_WIN_ROWS)
        row_idx = np.where(valid, kr - r + NA_WIN_ROWS - 1, n_rel_rows)
        bias = ext[:, row_idx.reshape(-1)].reshape(h, NA_Q_ROWS, NA_BAND_ROWS, w, w)
        tables.append(bias.transpose(0, 2, 4, 1, 3).reshape(h, NA_BAND_ROWS * w, NA_Q_ROWS * w))
    return (jnp.stack(tables, axis=1) * LOG2E).astype(F32)


def _na(qt, k, vt, k_ctx, vt_ctx, bias):
    b, h, hdim, s = qt.shape
    n_chunks, cw = k.shape[2], k.shape[3]
    n_ctx = k_ctx.shape[2]
    tq = NA_Q_ROWS * GRID_W
    nq = s // tq
    n_band = NA_BAND_ROWS * GRID_W // cw
    band_w = NA_BAND_ROWS * GRID_W
    per_step = NA_BLOCKS_PER_STEP
    return pl.pallas_call(
        functools.partial(_na_kernel, n_band=n_band, nq=nq, per_step=per_step),
        grid=(b, h, nq // per_step),
        in_specs=[
            pl.BlockSpec((1, 1, hdim, per_step * tq), lambda bi, hi, i: (bi, hi, 0, i)),
            pl.BlockSpec((1, 1, n_chunks, cw, hdim), lambda bi, hi, i: (bi, hi, 0, 0, 0)),
            pl.BlockSpec((1, 1, n_chunks, V_ROWS, cw), lambda bi, hi, i: (bi, hi, 0, 0, 0)),
            pl.BlockSpec((1, 1, n_ctx, hdim), lambda bi, hi, i: (bi, hi, 0, 0)),
            pl.BlockSpec((1, 1, V_ROWS, n_ctx), lambda bi, hi, i: (bi, hi, 0, 0)),
            pl.BlockSpec((1, 3 * band_w, tq), lambda bi, hi, i: (hi, 0, 0)),
        ],
        out_specs=pl.BlockSpec((1, 1, hdim, per_step * tq), lambda bi, hi, i: (bi, hi, 0, i)),
        out_shape=jax.ShapeDtypeStruct((b, h, hdim, s), BF16),
        scratch_shapes=[pltpu.VMEM((2, n_ctx + band_w, tq), F32)],
        compiler_params=_cparams("parallel", "parallel", "parallel"),
        name="na_attention",
    )(qt, k, vt, k_ctx, vt_ctx, bias.reshape(h, 3 * band_w, tq))


def kernel(x, c, ctx, c_ctx, ada_w, ada_b, ln1_g, ln1_b, ln2_g, ln2_b, conv_w_in, conv_w_out, conv_a_dw_w, conv_a_dw_b, conv_a_ln_g, conv_a_ln_b, conv_b_dw_w, attn_w_in, attn_w_out, q_norm_g, k_norm_g, na_rel_bias, router_w1, router_b1, router_w2, router_b2, expert_w_gate, expert_w_up, expert_w_down):
    bsz, seq, d = x.shape
    n_ctx = ctx.shape[1]
    depth = ada_w.shape[0]
    alpha = (2 * depth) ** 0.25
    hq = (d // 2) // HEAD_DIM
    hkv = hq // 4
    hd = (d // 2) // HEAD_DIM

    cond8 = jnp.zeros((8, d), F32).at[:bsz].set(c).at[bsz].set(c_ctx)
    mods = _ada(cond8, ada_w, ada_b)

    def lat_mod(i, k):
        return mods[i, :bsz, k * d:(k + 1) * d].reshape(bsz, 1, d)

    def ctx_mod(i, k):
        return jnp.broadcast_to(mods[i, bsz, k * d:(k + 1) * d].reshape(1, 1, d), (bsz, 1, d))

    wr = jnp.zeros((depth, d, ROUTER_COLS), F32)
    wr = wr.at[:, :, :N_GROUPS].set(router_w1).at[:, :, N_GROUPS:N_GROUPS + N_EXPERTS].set(router_w2)
    br = jnp.zeros((depth, 1, ROUTER_COLS), F32)
    br = br.at[:, 0, :N_GROUPS].set(router_b1).at[:, 0, N_GROUPS:N_GROUPS + N_EXPERTS].set(router_b2)

    cos_lat, sin_lat = _rope_tables(seq)
    cos_ctx = jnp.ones((n_ctx, LANES), F32)
    sin_ctx = jnp.zeros((n_ctx, LANES), F32)

    x_lat, x_ctx = x, ctx
    for i in range(depth):
        last = i == depth - 1
        odd = i % 2 == 1
        j = i // 2
        use_ctx = odd or not last
        if odd:
            w_in = attn_w_in[j].astype(BF16)
            w_out = attn_w_out[j].astype(BF16)
            z_lat = _mod_matmul(x_lat, lat_mod(i, 1), lat_mod(i, 0), w_in)
            z_ctx = _mod_matmul(x_ctx, ctx_mod(i, 1), ctx_mod(i, 0), w_in)
            qt, kc, vtc, qtd, kd, vtd = _attn_prep(z_lat, cos_lat, sin_lat, q_norm_g[j], k_norm_g[j], hq, hkv, hd)
            cqt, ckc, cvtc, cqtd, ckd, cvtd = _attn_prep(z_ctx, cos_ctx, sin_ctx, q_norm_g[j], k_norm_g[j],
                                                         hq, hkv, hd)
            ckc = ckc.reshape(bsz, hkv, n_ctx, HEAD_DIM)
            cvtc = cvtc.reshape(bsz, hkv, V_ROWS, n_ctx)
            ckd = ckd.reshape(bsz, hd, n_ctx, HEAD_DIM)
            cvtd = cvtd.reshape(bsz, hd, V_ROWS, n_ctx)
            ot_c = _gqa(qt, kc, vtc, ckc, cvtc)
            bias = _na_bias_table(na_rel_bias[j], seq // GRID_W)
            ot_d = _na(qtd, kd, vtd, ckd, cvtd, bias)
            mix_lat = (ot_c, ot_d)
            if not last:
                mix_ctx = (_ctx_gqa(cqt, ckc, cvtc), _ctx_gqa(cqtd, ckd, cvtd))
        else:
            w_in = conv_w_in[j].astype(BF16)
            w_out = conv_w_out[j].astype(BF16)
            conv_w = (conv_a_dw_w[j], conv_a_dw_b[j], conv_a_ln_g[j], conv_a_ln_b[j], conv_b_dw_w[j])
            z_lat = _mod_matmul(x_lat, lat_mod(i, 1), lat_mod(i, 0), w_in)
            mix_lat = _conv_mixers(z_lat, *conv_w)
            if not last:
                z_ctx = _mod_matmul(x_ctx, ctx_mod(i, 1), ctx_mod(i, 0), w_in)
                mix_ctx = _conv_mixers(z_ctx, *conv_w)

        n_lat = bsz * seq
        n_tok = n_lat if last else n_lat + bsz * n_ctx
        bufs = None if last else (jnp.zeros((n_tok, d), F32), jnp.zeros((n_tok, ROUTER_COLS), F32))
        x_lat, h_all, lg_all = _proj_ln(mix_lat, w_out, x_lat, lat_mod(i, 2), ln1_g[i], ln1_b[i],
                                        lat_mod(i, 4), lat_mod(i, 3), wr[i], br[i], alpha, n_tok, 0, bufs=bufs)
        moe_w = (expert_w_gate, expert_w_up, expert_w_down, i)
        if last:
            ys, dest, gates = _moe(h_all, lg_all, *moe_w)
            x_lat = _moe_ln(x_lat, ys, dest, gates, 0, lat_mod(i, 5), ln2_g[i], ln2_b[i], alpha)
        else:
            x_ctx, h_all, lg_all = _proj_ln(mix_ctx, w_out, x_ctx, ctx_mod(i, 2), ln1_g[i], ln1_b[i],
                                            ctx_mod(i, 4), ctx_mod(i, 3), wr[i], br[i], alpha, n_tok, n_lat,
                                            bufs=(h_all, lg_all))
            ys, dest, gates = _moe(h_all, lg_all, *moe_w)
            x_lat = _moe_ln(x_lat, ys, dest, gates, 0, lat_mod(i, 5), ln2_g[i], ln2_b[i], alpha)
            x_ctx = _moe_ln(x_ctx, ys, dest, gates, n_lat, ctx_mod(i, 5), ln2_g[i], ln2_b[i], alpha)
    return x_lat
```

```python
import functools
import math

import jax
import jax.numpy as jnp
import numpy as np
from jax import lax
from jax.experimental import pallas as pl
from jax.experimental.pallas import tpu as pltpu

F32 = jnp.float32
BF16 = jnp.bfloat16
HIGHEST = lax.Precision.HIGHEST

GRID_W = 64
HEAD_DIM = 64
V_ROWS = HEAD_DIM + 16
ATTN_SCALE = HEAD_DIM ** -0.5
LOG2E = math.log2(math.e)
CONV_A_WIDTH = 31
CONV_B_WIDTH = 3
CONV_HALO = 16
NA_WIN_ROWS = 8
NA_WIN_COLS = 16
NA_Q_ROWS = 4
NA_BAND_ROWS = 12
NA_BLOCKS_PER_STEP = 8
ROPE_THETA = 10000.0
N_GROUPS = 4
EXPERTS_PER_GROUP = 8
N_EXPERTS = N_GROUPS * EXPERTS_PER_GROUP
LN_EPS = 1e-5
RMS_EPS = 1e-6
NEG_BIG = -1e30

VMEM_LIMIT_BYTES = 48 * 1024 * 1024
LANES = 128
ROUTER_COLS = LANES
EXPERT_BLOCK = 256
RANK_BLOCK = 512


def _cparams(*sem):
    return pltpu.CompilerParams(dimension_semantics=sem, vmem_limit_bytes=VMEM_LIMIT_BYTES)


def _layer_norm(x, g, b):
    mu = jnp.mean(x, axis=-1, keepdims=True)
    xc = x - mu
    var = jnp.mean(xc * xc, axis=-1, keepdims=True)
    return xc * lax.rsqrt(var + LN_EPS) * g + b


def _ada_kernel(c_ref, w_ref, b_ref, o_ref):
    c = c_ref[...]
    s = c * jax.nn.sigmoid(c)
    o_ref[0] = jnp.dot(s, w_ref[0], preferred_element_type=F32, precision=HIGHEST) + b_ref[0]


def _ada(cond8, ada_w, ada_b):
    n_layers, d, n = ada_w.shape
    tn = 1536
    return pl.pallas_call(
        _ada_kernel,
        grid=(n_layers, n // tn),
        in_specs=[
            pl.BlockSpec((8, d), lambda l, j: (0, 0)),
            pl.BlockSpec((1, d, tn), lambda l, j: (l, 0, j)),
            pl.BlockSpec((1, 1, tn), lambda l, j: (l, 0, j)),
        ],
        out_specs=pl.BlockSpec((1, 8, tn), lambda l, j: (l, 0, j)),
        out_shape=jax.ShapeDtypeStruct((n_layers, 8, n), F32),
        compiler_params=_cparams("parallel", "parallel"),
        name="ada_mod",
    )(cond8, ada_w, ada_b.reshape(n_layers, 1, n))


def _mod_matmul_kernel(x_ref, sc_ref, sh_ref, w_ref, o_ref):
    h = x_ref[0] * (1.0 + sc_ref[0]) + sh_ref[0]
    o_ref[0] = jnp.dot(h.astype(BF16), w_ref[...], preferred_element_type=F32)


def _mod_matmul(x, sc, sh, w):
    b, s, d = x.shape
    n = w.shape[1]
    tm = min(s, 512)
    return pl.pallas_call(
        _mod_matmul_kernel,
        grid=(b, s // tm),
        in_specs=[
            pl.BlockSpec((1, tm, d), lambda bi, i: (bi, i, 0)),
            pl.BlockSpec((1, 1, d), lambda bi, i: (bi, 0, 0)),
            pl.BlockSpec((1, 1, d), lambda bi, i: (bi, 0, 0)),
            pl.BlockSpec((d, n), lambda bi, i: (0, 0)),
        ],
        out_specs=pl.BlockSpec((1, tm, n), lambda bi, i: (bi, i, 0)),
        out_shape=jax.ShapeDtypeStruct((b, s, n), F32),
        compiler_params=_cparams("parallel", "parallel"),
        name="mod_matmul",
    )(x, sc, sh, w)


def _conv_kernel(z_ref, zp_ref, zn_ref, aw_ref, ab_ref, ag_ref, abb_ref, bw_ref,
                 o_ref, ea_ref, eb_ref, sh_ref, *, tile, ch):
    i = pl.program_id(1)
    n = pl.num_programs(1)
    halo = CONV_HALO

    def glu(zz):
        return zz[:, 0:ch] * jax.nn.sigmoid(zz[:, ch:2 * ch])

    def cv(zz):
        return zz[:, 3 * ch:4 * ch] * zz[:, 4 * ch:5 * ch]

    pmask = (i > 0).astype(F32)
    nmask = (i < n - 1).astype(F32)
    zp = zp_ref[0]
    zn = zn_ref[0]
    ea_ref[0:halo, :] = glu(zp) * pmask
    eb_ref[0:halo, :] = cv(zp) * pmask
    ea_ref[halo + tile:2 * halo + tile, :] = glu(zn) * nmask
    eb_ref[halo + tile:2 * halo + tile, :] = cv(zn) * nmask
    ea_ref[halo:halo + tile, :] = glu(z_ref[0])
    eb_ref[halo:halo + tile, :] = cv(z_ref[0])

    rows = 32
    pad_a = CONV_A_WIDTH // 2
    pad_b = CONV_B_WIDTH // 2
    n_sh = sh_ref.shape[1]
    piece = 40
    for b in range(1, 8):
        for r in range(0, n_sh, piece):
            sh_ref[b - 1, r:r + piece, :] = ea_ref[pl.ds(r + b, piece), :]
    for r0 in range(0, tile, rows):
        acc = jnp.zeros((rows, ch), F32) + ab_ref[...]
        for k in range(CONV_A_WIDTH):
            a8, b = divmod(halo + k - pad_a, 8)
            if b == 0:
                tap = ea_ref[pl.ds(r0 + 8 * a8, rows), :]
            else:
                tap = sh_ref[b - 1, pl.ds(r0 + 8 * a8, rows), :]
            acc = acc + aw_ref[k:k + 1, :] * tap
        ya = _layer_norm(acc, ag_ref[...], abb_ref[...])
        ya = ya * jax.nn.sigmoid(ya)
        cb = jnp.zeros((rows, ch), F32)
        for k in range(CONV_B_WIDTH):
            cb = cb + bw_ref[k:k + 1, :] * eb_ref[pl.ds(halo + r0 + k - pad_b, rows), :]
        yb = z_ref[0, r0:r0 + rows, 2 * ch:3 * ch] * cb
        o_ref[0, r0:r0 + rows, 0:ch] = ya.astype(BF16)
        o_ref[0, r0:r0 + rows, ch:2 * ch] = yb.astype(BF16)


def _conv_mixers(z, a_dw_w, a_dw_b, a_ln_g, a_ln_b, b_dw_w):
    b, s, n = z.shape
    ch = n // 5
    tile = min(s, 256)
    hb = tile // CONV_HALO
    last_hb = s // CONV_HALO - 1
    kern = functools.partial(_conv_kernel, tile=tile, ch=ch)
    vec = lambda v: v.reshape(1, ch)
    return pl.pallas_call(
        kern,
        grid=(b, s // tile),
        in_specs=[
            pl.BlockSpec((1, tile, n), lambda bi, i: (bi, i, 0)),
            pl.BlockSpec((1, CONV_HALO, n), lambda bi, i: (bi, jnp.maximum(i * hb - 1, 0), 0)),
            pl.BlockSpec((1, CONV_HALO, n), lambda bi, i: (bi, jnp.minimum((i + 1) * hb, last_hb), 0)),
            pl.BlockSpec((CONV_A_WIDTH, ch), lambda bi, i: (0, 0)),
            pl.BlockSpec((1, ch), lambda bi, i: (0, 0)),
            pl.BlockSpec((1, ch), lambda bi, i: (0, 0)),
            pl.BlockSpec((1, ch), lambda bi, i: (0, 0)),
            pl.BlockSpec((CONV_B_WIDTH, ch), lambda bi, i: (0, 0)),
        ],
        out_specs=pl.BlockSpec((1, tile, 2 * ch), lambda bi, i: (bi, i, 0)),
        out_shape=jax.ShapeDtypeStruct((b, s, 2 * ch), BF16),
        scratch_shapes=[pltpu.VMEM((tile + 2 * CONV_HALO, ch), F32),
                        pltpu.VMEM((tile + 2 * CONV_HALO, ch), F32),
                        pltpu.VMEM((7, tile + 2 * CONV_HALO - 8, ch), F32)],
        compiler_params=_cparams("parallel", "parallel"),
        name="conv_mixers",
    )(z, z, z, a_dw_w, vec(a_dw_b), vec(a_ln_g), vec(a_ln_b), b_dw_w)


def _proj_ln_kernel(*refs, n_mix, n_alias, alpha):
    mix_refs = refs[:n_mix]
    (w_ref, x_ref, g_ref, lng_ref, lnb_ref, sc_ref, sh_ref, wrh_ref, wrl_ref, br_ref) = refs[n_mix:n_mix + 10]
    xo_ref, h_ref, lg_ref = refs[n_mix + 10 + n_alias:]
    if n_mix == 2:
        ut = jnp.concatenate([r[0].reshape(r.shape[1] * r.shape[2], r.shape[3]) for r in mix_refs], axis=0)
        y = lax.dot_general(ut, w_ref[...], (((0,), (0,)), ((), ())), preferred_element_type=F32)
    else:
        y = jnp.dot(mix_refs[0][0], w_ref[...], preferred_element_type=F32)
    xn = _layer_norm(alpha * x_ref[0] + g_ref[0] * y, lng_ref[...], lnb_ref[...])
    xo_ref[0] = xn
    h2 = xn * (1.0 + sc_ref[0]) + sh_ref[0]
    h_ref[...] = h2
    h_hi = h2.astype(BF16)
    h_lo = (h2 - h_hi.astype(F32)).astype(BF16)
    lg_ref[...] = (jnp.dot(h_hi, wrh_ref[...], preferred_element_type=F32)
                   + (jnp.dot(h_lo, wrh_ref[...], preferred_element_type=F32)
                      + jnp.dot(h_hi, wrl_ref[...], preferred_element_type=F32))
                   + br_ref[...])


def _proj_ln(mix, w_out, x, g, ln_g, ln_b, sc2, sh2, wr, br, alpha, n_tok, tok_off, bufs=None):
    b, s, d = x.shape
    tm = min(s, 512)
    nt = s // tm
    off = tok_off // tm
    if isinstance(mix, tuple):
        mix_specs = [pl.BlockSpec((1, o.shape[1], HEAD_DIM, tm), lambda bi, i: (bi, 0, 0, i)) for o in mix]
        mix_args = list(mix)
    else:
        mix_specs = [pl.BlockSpec((1, tm, mix.shape[2]), lambda bi, i: (bi, i, 0))]
        mix_args = [mix]
    row = lambda: pl.BlockSpec((1, tm, d), lambda bi, i: (bi, i, 0))
    per_b = lambda: pl.BlockSpec((1, 1, d), lambda bi, i: (bi, 0, 0))
    vec = lambda: pl.BlockSpec((1, d), lambda bi, i: (0, 0))
    alias_args = [] if bufs is None else list(bufs)
    n_in = len(mix_args) + 10
    wr_hi = wr.astype(BF16)
    wr_lo = (wr - wr_hi.astype(F32)).astype(BF16)
    kern = functools.partial(_proj_ln_kernel, n_mix=len(mix_args), n_alias=len(alias_args), alpha=alpha)
    return pl.pallas_call(
        kern,
        grid=(b, nt),
        in_specs=mix_specs + [
            pl.BlockSpec(w_out.shape, lambda bi, i: (0, 0)),
            row(), per_b(), vec(), vec(), per_b(), per_b(),
            pl.BlockSpec((d, ROUTER_COLS), lambda bi, i: (0, 0)),
            pl.BlockSpec((d, ROUTER_COLS), lambda bi, i: (0, 0)),
            pl.BlockSpec((1, ROUTER_COLS), lambda bi, i: (0, 0)),
        ] + [pl.BlockSpec(memory_space=pl.ANY) for _ in alias_args],
        out_specs=[row(),
                   pl.BlockSpec((tm, d), lambda bi, i: (off + bi * nt + i, 0)),
                   pl.BlockSpec((tm, ROUTER_COLS), lambda bi, i: (off + bi * nt + i, 0))],
        out_shape=[jax.ShapeDtypeStruct((b, s, d), F32),
                   jax.ShapeDtypeStruct((n_tok, d), F32),
                   jax.ShapeDtypeStruct((n_tok, ROUTER_COLS), F32)],
        input_output_aliases={n_in: 1, n_in + 1: 2} if alias_args else {},
        compiler_params=_cparams("parallel", "parallel"),
        name="proj_res_ln",
    )(*mix_args, w_out, x, g, ln_g.reshape(1, d), ln_b.reshape(1, d), sc2, sh2, wr_hi, wr_lo, br, *alias_args)


def _rank_kernel(e_ref, tri_ref, rank_ref, cnt_ref, carry_ref):
    @pl.when(pl.program_id(0) == 0)
    def _():
        carry_ref[...] = jnp.zeros_like(carry_ref)

    e = e_ref[0]
    blk = e.shape[1]
    ids = lax.broadcasted_iota(jnp.int32, (N_EXPERTS, blk), 0)
    oh = jnp.where(ids == e, 1.0, 0.0).astype(F32)
    before = jnp.dot(oh.astype(BF16), tri_ref[...], preferred_element_type=F32)
    carry = carry_ref[...]
    before = before + carry[:, 0:1]
    rank_ref[0] = jnp.sum(oh * before, axis=0, keepdims=True).astype(jnp.int32)
    carry = carry + jnp.sum(oh, axis=1, keepdims=True)
    carry_ref[...] = carry
    cnt_ref[...] = carry.astype(jnp.int32)


def _expert_ranks(flat_e):
    m = flat_e.shape[0]
    blk = RANK_BLOCK
    nblk = m // blk
    tri = (lax.broadcasted_iota(jnp.int32, (blk, blk), 0)
           < lax.broadcasted_iota(jnp.int32, (blk, blk), 1)).astype(BF16)
    rank, cnt = pl.pallas_call(
        _rank_kernel,
        grid=(nblk,),
        in_specs=[pl.BlockSpec((1, 1, blk), lambda i: (i, 0, 0)),
                  pl.BlockSpec((blk, blk), lambda i: (0, 0))],
        out_specs=[pl.BlockSpec((1, 1, blk), lambda i: (i, 0, 0)),
                   pl.BlockSpec((N_EXPERTS, LANES), lambda i: (0, 0))],
        out_shape=[jax.ShapeDtypeStruct((nblk, 1, blk), jnp.int32),
                   jax.ShapeDtypeStruct((N_EXPERTS, LANES), jnp.int32)],
        scratch_shapes=[pltpu.VMEM((N_EXPERTS, LANES), F32)],
        compiler_params=_cparams("arbitrary"),
        name="expert_rank",
    )(flat_e.reshape(nblk, 1, blk), tri)
    return rank.reshape(m), cnt[:, 0]


def _to_row_tiles(ref, x, n_rows, tiles):
    for s in range(tiles):
        ref[pl.ds(s, n_rows, stride=tiles), :] = x[:, s * LANES:(s + 1) * LANES]


def _from_row_tiles(ref, n_rows, tiles):
    return jnp.concatenate([ref[pl.ds(s, n_rows, stride=tiles), :] for s in range(tiles)], axis=1)


def _dispatch_kernel(dest_ref, x_ref, init_ref, xs_ref, buf_ref, sem, *, tm, tiles):
    del init_ref
    i = pl.program_id(0)
    n = pl.num_programs(0)
    slot = i % 2

    def drain(s):
        for _ in range(2):
            pltpu.make_async_copy(buf_ref.at[s], xs_ref.at[pl.ds(0, tm * tiles), :], sem.at[s]).wait()

    @pl.when(i >= 2)
    def _():
        drain(slot)

    _to_row_tiles(buf_ref.at[slot], x_ref[...], tm, tiles)

    def issue(r, c):
        src = buf_ref.at[slot, pl.ds(pl.multiple_of(r * tiles, tiles), tiles), :]
        for k in range(2):
            d = pl.multiple_of(dest_ref[0, 0, 2 * r + k] * tiles, tiles)
            pltpu.make_async_copy(src, xs_ref.at[pl.ds(d, tiles), :], sem.at[slot]).start(priority=k)
        return c

    lax.fori_loop(0, tm, issue, 0, unroll=4)

    @pl.when(i == n - 1)
    def _():
        @pl.when(i >= 1)
        def _():
            drain(1 - slot)

        drain(slot)


def _dispatch(h_all, dest, cap):
    n, d = h_all.shape
    tiles = d // LANES
    tm = 512
    nt = n // tm
    return pl.pallas_call(
        functools.partial(_dispatch_kernel, tm=tm, tiles=tiles),
        grid=(nt,),
        in_specs=[
            pl.BlockSpec((1, 1, 2 * tm), lambda i: (i, 0, 0), memory_space=pltpu.SMEM),
            pl.BlockSpec((tm, d), lambda i: (i, 0)),
            pl.BlockSpec(memory_space=pl.ANY),
        ],
        out_specs=pl.BlockSpec(memory_space=pl.ANY),
        out_shape=jax.ShapeDtypeStruct((cap * tiles, LANES), F32),
        scratch_shapes=[pltpu.VMEM((2, tm * tiles, LANES), F32), pltpu.SemaphoreType.DMA((2,))],
        input_output_aliases={2: 0},
        compiler_params=_cparams("arbitrary"),
        name="moe_dispatch",
    )(dest.reshape(nt, 1, 2 * tm), h_all, jnp.zeros((cap * tiles, LANES), F32))


def _experts_kernel(be_ref, nu_ref, x_ref, wg_ref, wu_ref, wd_ref, o_ref, wgb_ref, wub_ref, wdb_ref,
                    *, tb, tiles):
    i = pl.program_id(0)
    changed = jnp.logical_or(i == 0, be_ref[i] != be_ref[jnp.maximum(i - 1, 0)])

    @pl.when(changed)
    def _():
        wgb_ref[...] = wg_ref[0, 0].astype(BF16)
        wub_ref[...] = wu_ref[0, 0].astype(BF16)
        wdb_ref[...] = wd_ref[0, 0].astype(BF16)

    @pl.when(i < nu_ref[0])
    def _():
        x = _from_row_tiles(x_ref, tb, tiles).astype(BF16)
        g = jnp.dot(x, wgb_ref[...], preferred_element_type=F32)
        u = jnp.dot(x, wub_ref[...], preferred_element_type=F32)
        a = (g * jax.nn.sigmoid(g) * u).astype(BF16)
        _to_row_tiles(o_ref, jnp.dot(a, wdb_ref[...], preferred_element_type=F32), tb, tiles)

    @pl.when(i >= nu_ref[0])
    def _():
        o_ref[...] = jnp.zeros_like(o_ref)


def _experts(xs, block_e, n_used, w_gate, w_up, w_down, layer):
    d, hid = w_gate.shape[2], w_gate.shape[3]
    tiles = d // LANES
    tb = EXPERT_BLOCK
    nb = xs.shape[0] // (tb * tiles)
    grid_spec = pltpu.PrefetchScalarGridSpec(
        num_scalar_prefetch=2,
        grid=(nb,),
        in_specs=[
            pl.BlockSpec((tb * tiles, LANES), lambda i, be, nu: (i, 0)),
            pl.BlockSpec((1, 1, d, hid), lambda i, be, nu: (layer, be[i], 0, 0)),
            pl.BlockSpec((1, 1, d, hid), lambda i, be, nu: (layer, be[i], 0, 0)),
            pl.BlockSpec((1, 1, hid, d), lambda i, be, nu: (layer, be[i], 0, 0)),
        ],
        out_specs=pl.BlockSpec((tb * tiles, LANES), lambda i, be, nu: (i, 0)),
        scratch_shapes=[pltpu.VMEM((d, hid), BF16), pltpu.VMEM((d, hid), BF16), pltpu.VMEM((hid, d), BF16)],
    )
    return pl.pallas_call(
        functools.partial(_experts_kernel, tb=tb, tiles=tiles),
        grid_spec=grid_spec,
        out_shape=jax.ShapeDtypeStruct(xs.shape, F32),
        compiler_params=_cparams("arbitrary"),
        name="experts",
    )(block_e, n_used, xs, w_gate, w_up, w_down)


def _moe_ln_kernel(dest_ref, dest_next_ref, x_ref, gt_ref, g_ref, lng_ref, lnb_ref, ys_ref, o_ref,
                   y0_ref, y1_ref, sem0, sem1, *, tm, tiles, alpha):
    i = pl.program_id(0)
    n = pl.num_programs(0)
    slot = i % 2

    def gather(idx_ref, slot):
        def issue(r, c):
            dst = pl.ds(pl.multiple_of(r * tiles, tiles), tiles)
            d0 = pl.multiple_of(idx_ref[0, 0, 2 * r] * tiles, tiles)
            d1 = pl.multiple_of(idx_ref[0, 0, 2 * r + 1] * tiles, tiles)
            pltpu.make_async_copy(ys_ref.at[pl.ds(d0, tiles), :], y0_ref.at[slot, dst, :],
                                  sem0.at[slot]).start(priority=0)
            pltpu.make_async_copy(ys_ref.at[pl.ds(d1, tiles), :], y1_ref.at[slot, dst, :],
                                  sem1.at[slot]).start(priority=1)
            return c

        lax.fori_loop(0, tm, issue, 0, unroll=4)

    @pl.when(i == 0)
    def _():
        gather(dest_ref, 0)

    @pl.when(i + 1 < n)
    def _():
        gather(dest_next_ref, 1 - slot)

    pltpu.make_async_copy(ys_ref.at[pl.ds(0, tm * tiles), :], y0_ref.at[slot], sem0.at[slot]).wait()
    pltpu.make_async_copy(ys_ref.at[pl.ds(0, tm * tiles), :], y1_ref.at[slot], sem1.at[slot]).wait()
    gt = gt_ref[...]
    y = (gt[:, 0:1] * _from_row_tiles(y0_ref.at[slot], tm, tiles)
         + gt[:, 1:2] * _from_row_tiles(y1_ref.at[slot], tm, tiles))
    o_ref[...] = _layer_norm(alpha * x_ref[...] + g_ref[0] * y, lng_ref[...], lnb_ref[...])


def _moe_ln(x, ys, dest, gates, tok_off, g, ln_g, ln_b, alpha):
    b, s, d = x.shape
    tiles = d // LANES
    n_tok = gates.shape[0]
    tm = min(s, 512)
    nt = s // tm
    off = tok_off // tm
    n_steps = b * nt
    vec = lambda: pl.BlockSpec((1, d), lambda i: (0, 0))
    dest_blocks = dest.reshape(n_tok // tm, 1, 2 * tm)
    out = pl.pallas_call(
        functools.partial(_moe_ln_kernel, tm=tm, tiles=tiles, alpha=alpha),
        grid=(n_steps,),
        in_specs=[
            pl.BlockSpec((1, 1, 2 * tm), lambda i: (off + i, 0, 0), memory_space=pltpu.SMEM),
            pl.BlockSpec((1, 1, 2 * tm), lambda i: (off + jnp.minimum(i + 1, n_steps - 1), 0, 0),
                         memory_space=pltpu.SMEM),
            pl.BlockSpec((tm, d), lambda i: (i, 0)),
            pl.BlockSpec((tm, 2), lambda i: (off + i, 0)),
            pl.BlockSpec((1, 1, d), lambda i: (i // nt, 0, 0)),
            vec(), vec(),
            pl.BlockSpec(memory_space=pl.ANY),
        ],
        out_specs=pl.BlockSpec((tm, d), lambda i: (i, 0)),
        out_shape=jax.ShapeDtypeStruct((b * s, d), F32),
        scratch_shapes=[pltpu.VMEM((2, tm * tiles, LANES), F32), pltpu.VMEM((2, tm * tiles, LANES), F32),
                        pltpu.SemaphoreType.DMA((2,)), pltpu.SemaphoreType.DMA((2,))],
        compiler_params=_cparams("arbitrary"),
        name="moe_res_ln",
    )(dest_blocks, dest_blocks, x.reshape(b * s, d), gates, g, ln_g.reshape(1, d), ln_b.reshape(1, d), ys)
    return out.reshape(b, s, d)


def _route(logits):
    n = logits.shape[0]
    l1 = logits[:, :N_GROUPS]
    grp = jnp.argmax(l1, axis=-1).astype(jnp.int32)
    p_grp = 1.0 / jnp.sum(jnp.exp(l1 - jnp.max(l1, axis=-1, keepdims=True)), axis=-1)
    lg2 = logits[:, N_GROUPS:N_GROUPS + N_EXPERTS].reshape(n, N_GROUPS, EXPERTS_PER_GROUP)
    gsel = grp[:, None] == jnp.arange(N_GROUPS, dtype=jnp.int32)[None, :]
    sel = jnp.sum(jnp.where(gsel[:, :, None], lg2, 0.0), axis=1)
    i1 = jnp.argmax(sel, axis=-1).astype(jnp.int32)
    v1 = jnp.max(sel, axis=-1)
    col = jnp.arange(EXPERTS_PER_GROUP, dtype=jnp.int32)[None, :]
    sel2 = jnp.where(col == i1[:, None], -jnp.inf, sel)
    i2 = jnp.argmax(sel2, axis=-1).astype(jnp.int32)
    v2 = jnp.max(sel2, axis=-1)
    e2 = jnp.exp(v2 - v1)
    w1 = 1.0 / (1.0 + e2)
    gates = p_grp[:, None] * jnp.stack([w1, e2 * w1], axis=-1)
    experts = grp[:, None] * EXPERTS_PER_GROUP + jnp.stack([i1, i2], axis=-1)
    return experts, gates.astype(F32)


def _moe(h_all, logits, w_gate, w_up, w_down, layer):
    n, d = h_all.shape
    m = 2 * n
    tb = EXPERT_BLOCK
    experts, gates = _route(logits)
    flat_e = experts.reshape(m)
    rank, counts = _expert_ranks(flat_e)
    padded = (counts + tb - 1) // tb * tb
    pad_end = jnp.cumsum(padded)
    pad_start = pad_end - padded
    onehot = flat_e[:, None] == jnp.arange(N_EXPERTS, dtype=jnp.int32)[None, :]
    dest = jnp.sum(jnp.where(onehot, pad_start[None, :], 0), axis=-1).astype(jnp.int32) + rank
    nb = -(-m // tb) + N_EXPERTS
    cap = nb * tb
    blk_start = jnp.arange(nb, dtype=jnp.int32) * tb
    block_e = jnp.minimum(jnp.sum(pad_end[None, :] <= blk_start[:, None], axis=-1), N_EXPERTS - 1)
    n_used = (pad_end[-1] // tb).reshape(1)

    xs = _dispatch(h_all, dest, cap)
    ys = _experts(xs, block_e.astype(jnp.int32), n_used.astype(jnp.int32), w_gate, w_up, w_down, layer)
    return ys, dest, gates


def _prep_kernel(z_ref, cos_ref, sin_ref, gq_ref, gk_ref, grp_ref,
                 qt_ref, kc_ref, vt_ref, qtd_ref, kd_ref, vtd_ref, *, hq, hkv, hd, ts, dchunk):
    cos = cos_ref[...]
    sin = sin_ref[...]
    lane = lax.broadcasted_iota(jnp.int32, (ts, LANES), 1)
    first_half = (lane % 32) < 16

    def norm_rope(x, g):
        ss = jnp.dot(x * x, grp_ref[...], preferred_element_type=F32, precision=HIGHEST)
        xn = x * lax.rsqrt(ss * (1.0 / HEAD_DIM) + RMS_EPS) * g
        partner = jnp.where(first_half, pltpu.roll(xn, LANES - 16, 1), pltpu.roll(xn, 16, 1))
        return xn * cos + partner * sin

    col = 0
    for s in range(hq // 2):
        x = norm_rope(z_ref[0, :, col:col + LANES], gq_ref[...]) * (ATTN_SCALE * LOG2E)
        t = x.T.astype(BF16)
        qt_ref[0, 2 * s] = t[0:HEAD_DIM]
        qt_ref[0, 2 * s + 1] = t[HEAD_DIM:]
        col += LANES
    for s in range(hkv // 2):
        x = norm_rope(z_ref[0, :, col:col + LANES], gk_ref[...]).astype(BF16)
        kc_ref[0, 2 * s, 0] = x[:, 0:HEAD_DIM]
        kc_ref[0, 2 * s + 1, 0] = x[:, HEAD_DIM:]
        col += LANES
    tail = (lax.broadcasted_iota(jnp.int32, (V_ROWS - HEAD_DIM, ts), 0) == 0).astype(BF16)
    tail_d = (lax.broadcasted_iota(jnp.int32, (V_ROWS - HEAD_DIM, dchunk), 0) == 0).astype(BF16)
    for s in range(hkv // 2):
        t = z_ref[0, :, col:col + LANES].T.astype(BF16)
        for hh in range(2):
            vt_ref[0, 2 * s + hh, 0, 0:HEAD_DIM, :] = t[hh * HEAD_DIM:(hh + 1) * HEAD_DIM]
            vt_ref[0, 2 * s + hh, 0, HEAD_DIM:V_ROWS, :] = tail
        col += LANES
    for s in range(hd // 2):
        t = (z_ref[0, :, col:col + LANES] * (ATTN_SCALE * LOG2E)).T.astype(BF16)
        qtd_ref[0, 2 * s] = t[0:HEAD_DIM]
        qtd_ref[0, 2 * s + 1] = t[HEAD_DIM:]
        col += LANES
    for s in range(hd // 2):
        x = z_ref[0, :, col:col + LANES].astype(BF16)
        for c in range(ts // dchunk):
            kd_ref[0, 2 * s, c] = x[c * dchunk:(c + 1) * dchunk, 0:HEAD_DIM]
            kd_ref[0, 2 * s + 1, c] = x[c * dchunk:(c + 1) * dchunk, HEAD_DIM:]
        col += LANES
    for s in range(hd // 2):
        t = z_ref[0, :, col:col + LANES].T.astype(BF16)
        for c in range(ts // dchunk):
            for hh in range(2):
                vtd_ref[0, 2 * s + hh, c, 0:HEAD_DIM, :] = t[hh * HEAD_DIM:(hh + 1) * HEAD_DIM,
                                                             c * dchunk:(c + 1) * dchunk]
                vtd_ref[0, 2 * s + hh, c, HEAD_DIM:V_ROWS, :] = tail_d
        col += LANES


def _attn_prep(z, cos, sin, q_g, k_g, hq, hkv, hd):
    b, s, n = z.shape
    ts = min(s, 512)
    dchunk = 256
    nt = s // ts
    hdim = HEAD_DIM
    gq = jnp.tile(q_g, 2).reshape(1, LANES)
    gk = jnp.tile(k_g, 2).reshape(1, LANES)
    li = jnp.arange(LANES)
    grp = (li[:, None] // hdim == li[None, :] // hdim).astype(F32)
    kern = functools.partial(_prep_kernel, hq=hq, hkv=hkv, hd=hd, ts=ts, dchunk=dchunk)
    const = lambda shape: pl.BlockSpec(shape, lambda bi, i: (0,) * len(shape))
    return pl.pallas_call(
        kern,
        grid=(b, nt),
        in_specs=[
            pl.BlockSpec((1, ts, n), lambda bi, i: (bi, i, 0)),
            pl.BlockSpec((ts, LANES), lambda bi, i: (i, 0)),
            pl.BlockSpec((ts, LANES), lambda bi, i: (i, 0)),
            const((1, LANES)), const((1, LANES)), const((LANES, LANES)),
        ],
        out_specs=[
            pl.BlockSpec((1, hq, hdim, ts), lambda bi, i: (bi, 0, 0, i)),
            pl.BlockSpec((1, hkv, 1, ts, hdim), lambda bi, i: (bi, 0, i, 0, 0)),
            pl.BlockSpec((1, hkv, 1, V_ROWS, ts), lambda bi, i: (bi, 0, i, 0, 0)),
            pl.BlockSpec((1, hd, hdim, ts), lambda bi, i: (bi, 0, 0, i)),
            pl.BlockSpec((1, hd, ts // dchunk, dchunk, hdim), lambda bi, i: (bi, 0, i, 0, 0)),
            pl.BlockSpec((1, hd, ts // dchunk, V_ROWS, dchunk), lambda bi, i: (bi, 0, i, 0, 0)),
        ],
        out_shape=[
            jax.ShapeDtypeStruct((b, hq, hdim, s), BF16),
            jax.ShapeDtypeStruct((b, hkv, nt, ts, hdim), BF16),
            jax.ShapeDtypeStruct((b, hkv, nt, V_ROWS, ts), BF16),
            jax.ShapeDtypeStruct((b, hd, hdim, s), BF16),
            jax.ShapeDtypeStruct((b, hd, s // dchunk, dchunk, hdim), BF16),
            jax.ShapeDtypeStruct((b, hd, s // dchunk, V_ROWS, dchunk), BF16),
        ],
        compiler_params=_cparams("parallel", "parallel"),
        name="attn_prep",
    )(z, cos, sin, gq, gk, grp)


def _rope_tables(s):
    t = jnp.arange(s)
    row = (t // GRID_W).astype(F32)
    colp = (t % GRID_W).astype(F32)
    axis_dim = HEAD_DIM // 2
    inv = ROPE_THETA ** (-jnp.arange(0, axis_dim, 2, dtype=F32) / axis_dim)
    ar = row[:, None] * inv
    ac = colp[:, None] * inv
    cr, sr, cc, sc = jnp.cos(ar), jnp.sin(ar), jnp.cos(ac), jnp.sin(ac)
    cos = jnp.concatenate([cr, cr, cc, cc], axis=-1)
    sin = jnp.concatenate([-sr, sr, -sc, sc], axis=-1)
    return jnp.tile(cos, (1, 2)), jnp.tile(sin, (1, 2))


def _gqa_kernel(qt_ref, k_ref, vt_ref, kc_ref, vtc_ref, o_ref, acc_ref, s_ref, sc_ref, *, group):
    n_chunks = k_ref.shape[2]
    tq = qt_ref.shape[3]

    def q_all():
        return jnp.concatenate([qt_ref[0, g] for g in range(group)], axis=1)

    def scores(kb, slot):
        st = jnp.dot(kb, q_all(), preferred_element_type=F32)
        s_ref[slot] = st
        return jnp.max(st, axis=0, keepdims=True)

    def absorb(slot, vb, cmax, m):
        m_new = jnp.maximum(m, cmax)
        alpha = jnp.exp2(m - m_new)
        p = jnp.exp2(s_ref[slot] - m_new)
        acc_ref[...] = alpha * acc_ref[...] + jnp.dot(vb, p.astype(BF16), preferred_element_type=F32)
        return m_new

    per_trip = 16 if n_chunks % 16 == 0 else 2
    st = jnp.dot(kc_ref[0, 0], q_all(), preferred_element_type=F32)
    sc_ref[...] = st
    m = jnp.max(st, axis=0, keepdims=True)
    cm = scores(k_ref[0, 0, 0], 0)
    p = jnp.exp2(sc_ref[...] - m)
    acc_ref[...] = jnp.dot(vtc_ref[0, 0], p.astype(BF16), preferred_element_type=F32)

    def body(i, carry):
        m, cm_cur = carry
        j0 = per_trip * i
        for u in range(per_trip):
            cm_next = scores(k_ref[0, 0, jnp.minimum(j0 + u + 1, n_chunks - 1)], (u + 1) % 2)
            m = absorb(u % 2, vt_ref[0, 0, j0 + u], cm_cur, m)
            cm_cur = cm_next
        return m, cm_cur

    lax.fori_loop(0, n_chunks // per_trip, body, (m, cm))
    out = acc_ref[0:HEAD_DIM, :] / acc_ref[HEAD_DIM:HEAD_DIM + 1, :]
    for g in range(group):
        o_ref[0, g] = out[:, g * tq:(g + 1) * tq].astype(BF16)


def _gqa(qt, k, vt, k_ctx, vt_ctx):
    b, hq, hdim, s = qt.shape
    hkv, n_chunks, tk = k.shape[1], k.shape[2], k.shape[3]
    n_ctx = k_ctx.shape[2]
    group = hq // hkv
    tq = min(s, 256)
    return pl.pallas_call(
        functools.partial(_gqa_kernel, group=group),
        grid=(b, hkv, s // tq),
        in_specs=[
            pl.BlockSpec((1, group, hdim, tq), lambda bi, h, i: (bi, h, 0, i)),
            pl.BlockSpec((1, 1, n_chunks, tk, hdim), lambda bi, h, i: (bi, h, 0, 0, 0)),
            pl.BlockSpec((1, 1, n_chunks, V_ROWS, tk), lambda bi, h, i: (bi, h, 0, 0, 0)),
            pl.BlockSpec((1, 1, n_ctx, hdim), lambda bi, h, i: (bi, h, 0, 0)),
            pl.BlockSpec((1, 1, V_ROWS, n_ctx), lambda bi, h, i: (bi, h, 0, 0)),
        ],
        out_specs=pl.BlockSpec((1, group, hdim, tq), lambda bi, h, i: (bi, h, 0, i)),
        out_shape=jax.ShapeDtypeStruct((b, hq, hdim, s), BF16),
        scratch_shapes=[pltpu.VMEM((V_ROWS, group * tq), F32), pltpu.VMEM((2, tk, group * tq), F32),
                        pltpu.VMEM((n_ctx, group * tq), F32)],
        compiler_params=_cparams("parallel", "parallel", "parallel"),
        name="gqa_attention",
    )(qt, k, vt, k_ctx, vt_ctx)


def _ctx_gqa_kernel(qt_ref, k_ref, vt_ref, o_ref):
    st = jnp.dot(k_ref[0, 0], qt_ref[0, 0], preferred_element_type=F32)
    p = jnp.exp2(st - jnp.max(st, axis=0, keepdims=True))
    acc = jnp.dot(vt_ref[0, 0], p.astype(BF16), preferred_element_type=F32)
    o_ref[0, 0] = (acc[0:HEAD_DIM] / acc[HEAD_DIM:HEAD_DIM + 1]).astype(BF16)


def _ctx_gqa(qt, k, vt):
    b, hq, hdim, n = qt.shape
    group = hq // k.shape[1]
    return pl.pallas_call(
        _ctx_gqa_kernel,
        grid=(b, hq),
        in_specs=[
            pl.BlockSpec((1, 1, hdim, n), lambda bi, h: (bi, h, 0, 0)),
            pl.BlockSpec((1, 1, n, hdim), lambda bi, h: (bi, h // group, 0, 0)),
            pl.BlockSpec((1, 1, V_ROWS, n), lambda bi, h: (bi, h // group, 0, 0)),
        ],
        out_specs=pl.BlockSpec((1, 1, hdim, n), lambda bi, h: (bi, h, 0, 0)),
        out_shape=jax.ShapeDtypeStruct((b, hq, hdim, n), BF16),
        compiler_params=_cparams("parallel", "parallel"),
        name="ctx_gqa",
    )(qt, k, vt)


def _na_kernel(qt_ref, k_ref, vt_ref, kc_ref, vtc_ref, bias_ref, o_ref, s_ref, *, n_band, nq, per_step):
    step = pl.program_id(2)
    tq = NA_Q_ROWS * GRID_W
    cw = k_ref.shape[3]
    n_ctx = kc_ref.shape[2]

    def scores(j, slot):
        i = step * per_step + j
        c0 = jnp.clip(i - 1, 0, nq - n_band)
        variant = jnp.where(i == 0, 0, jnp.where(i == nq - 1, 2, 1))
        qt = qt_ref[0, 0, :, j * tq:(j + 1) * tq]
        st = jnp.dot(kc_ref[0, 0], qt, preferred_element_type=F32)
        s_ref[slot, 0:n_ctx, :] = st
        m = jnp.max(st, axis=0, keepdims=True)
        for c in range(n_band):
            b0 = pl.multiple_of((variant * n_band + c) * cw, cw)
            st = (jnp.dot(k_ref[0, 0, c0 + c], qt, preferred_element_type=F32)
                  + bias_ref[0, pl.ds(b0, cw), :])
            s_ref[slot, n_ctx + c * cw:n_ctx + (c + 1) * cw, :] = st
            m = jnp.maximum(m, jnp.max(st, axis=0, keepdims=True))
        return m, c0

    def finish(j, slot, m, c0):
        p = jnp.exp2(s_ref[slot, 0:n_ctx, :] - m).astype(BF16)
        acc = jnp.dot(vtc_ref[0, 0], p, preferred_element_type=F32)
        for c in range(n_band):
            p = jnp.exp2(s_ref[slot, n_ctx + c * cw:n_ctx + (c + 1) * cw, :] - m).astype(BF16)
            acc = acc + jnp.dot(vt_ref[0, 0, c0 + c], p, preferred_element_type=F32)
        o_ref[0, 0, :, j * tq:(j + 1) * tq] = (acc[0:HEAD_DIM] / acc[HEAD_DIM:HEAD_DIM + 1]).astype(BF16)

    nxt = scores(0, 0)
    for j in range(per_step):
        cur = nxt
        if j + 1 < per_step:
            nxt = scores(j + 1, (j + 1) % 2)
        finish(j, j % 2, *cur)


def _na_bias_table(rel_bias, rows):
    w = GRID_W
    h, n_rel_rows, n_rel_cols = rel_bias.shape
    qcol = np.arange(w)
    cstart = np.clip(qcol - NA_WIN_COLS // 2, 0, w - NA_WIN_COLS)
    in_win = (qcol[None, :] >= cstart[:, None]) & (qcol[None, :] < cstart[:, None] + NA_WIN_COLS)
    col_idx = np.clip(qcol[None, :] - qcol[:, None] + NA_WIN_COLS - 1, 0, n_rel_cols - 1)
    onehot = (col_idx.reshape(-1)[None, :] == np.arange(n_rel_cols)[:, None]).astype(np.float32)
    cols = jnp.einsum('hrj,jn->hrn', rel_bias, onehot, precision=HIGHEST).reshape(h, n_rel_rows, w, w)
    cols = jnp.where(in_win[None, None], cols, NEG_BIG)
    ext = jnp.concatenate([cols, jnp.full((h, 1, w, w), NEG_BIG, F32)], axis=1)
    a = np.arange(NA_Q_ROWS)[:, None]
    c = np.arange(NA_BAND_ROWS)[None, :]
    tables = []
    for r0, rs0 in ((0, 0), (NA_Q_ROWS, 0), (rows - NA_Q_ROWS, rows - NA_BAND_ROWS)):
        r = r0 + a
        kr = rs0 + c
        rs = np.clip(r - NA_WIN_ROWS // 2, 0, rows - NA_WIN_ROWS)
        valid = (kr >= rs) & (kr < rs + NA_WIN_ROWS)
        row_idx = np.where(valid, kr - r + NA_WIN_ROWS - 1, n_rel_rows)
        bias = ext[:, row_idx.reshape(-1)].reshape(h, NA_Q_ROWS, NA_BAND_ROWS, w, w)
        tables.append(bias.transpose(0, 2, 4, 1, 3).reshape(h, NA_BAND_ROWS * w, NA_Q_ROWS * w))
    return (jnp.stack(tables, axis=1) * LOG2E).astype(F32)


def _na(qt, k, vt, k_ctx, vt_ctx, bias):
    b, h, hdim, s = qt.shape
    n_chunks, cw = k.shape[2], k.shape[3]
    n_ctx = k_ctx.shape[2]
    tq = NA_Q_ROWS * GRID_W
    nq = s // tq
    n_band = NA_BAND_ROWS * GRID_W // cw
    band_w = NA_BAND_ROWS * GRID_W
    per_step = math.gcd(nq, NA_BLOCKS_PER_STEP)
    return pl.pallas_call(
        functools.partial(_na_kernel, n_band=n_band, nq=nq, per_step=per_step),
        grid=(b, h, nq // per_step),
        in_specs=[
            pl.BlockSpec((1, 1, hdim, per_step * tq), lambda bi, hi, i: (bi, hi, 0, i)),
            pl.BlockSpec((1, 1, n_chunks, cw, hdim), lambda bi, hi, i: (bi, hi, 0, 0, 0)),
            pl.BlockSpec((1, 1, n_chunks, V_ROWS, cw), lambda bi, hi, i: (bi, hi, 0, 0, 0)),
            pl.BlockSpec((1, 1, n_ctx, hdim), lambda bi, hi, i: (bi, hi, 0, 0)),
            pl.BlockSpec((1, 1, V_ROWS, n_ctx), lambda bi, hi, i: (bi, hi, 0, 0)),
            pl.BlockSpec((1, 3 * band_w, tq), lambda bi, hi, i: (hi, 0, 0)),
        ],
        out_specs=pl.BlockSpec((1, 1, hdim, per_step * tq), lambda bi, hi, i: (bi, hi, 0, i)),
        out_shape=jax.ShapeDtypeStruct((b, h, hdim, s), BF16),
        scratch_shapes=[pltpu.VMEM((2, n_ctx + band_w, tq), F32)],
        compiler_params=_cparams("parallel", "parallel", "parallel"),
        name="na_attention",
    )(qt, k, vt, k_ctx, vt_ctx, bias.reshape(h, 3 * band_w, tq))


def kernel(x, c, ctx, c_ctx, ada_w, ada_b, ln1_g, ln1_b, ln2_g, ln2_b, conv_w_in, conv_w_out, conv_a_dw_w, conv_a_dw_b, conv_a_ln_g, conv_a_ln_b, conv_b_dw_w, attn_w_in, attn_w_out, q_norm_g, k_norm_g, na_rel_bias, router_w1, router_b1, router_w2, router_b2, expert_w_gate, expert_w_up, expert_w_down):
    bsz, seq, d = x.shape
    n_ctx = ctx.shape[1]
    depth = ada_w.shape[0]
    alpha = (2 * depth) ** 0.25
    hq = (d // 2) // HEAD_DIM
    hkv = hq // 4
    hd = (d // 2) // HEAD_DIM

    cond8 = jnp.zeros((8, d), F32).at[:bsz].set(c).at[bsz].set(c_ctx)
    mods = _ada(cond8, ada_w, ada_b)

    def lat_mod(i, k):
        return mods[i, :bsz, k * d:(k + 1) * d].reshape(bsz, 1, d)

    def ctx_mod(i, k):
        return jnp.broadcast_to(mods[i, bsz, k * d:(k + 1) * d].reshape(1, 1, d), (bsz, 1, d))

    wr = jnp.zeros((depth, d, ROUTER_COLS), F32)
    wr = wr.at[:, :, :N_GROUPS].set(router_w1).at[:, :, N_GROUPS:N_GROUPS + N_EXPERTS].set(router_w2)
    br = jnp.zeros((depth, 1, ROUTER_COLS), F32)
    br = br.at[:, 0, :N_GROUPS].set(router_b1).at[:, 0, N_GROUPS:N_GROUPS + N_EXPERTS].set(router_b2)

    cos_lat, sin_lat = _rope_tables(seq)
    cos_ctx = jnp.ones((n_ctx, LANES), F32)
    sin_ctx = jnp.zeros((n_ctx, LANES), F32)

    x_lat, x_ctx = x, ctx
    for i in range(depth):
        last = i == depth - 1
        odd = i % 2 == 1
        j = i // 2
        use_ctx = odd or not last
        if odd:
            w_in = attn_w_in[j].astype(BF16)
            w_out = attn_w_out[j].astype(BF16)
            z_lat = _mod_matmul(x_lat, lat_mod(i, 1), lat_mod(i, 0), w_in)
            z_ctx = _mod_matmul(x_ctx, ctx_mod(i, 1), ctx_mod(i, 0), w_in)
            qt, kc, vtc, qtd, kd, vtd = _attn_prep(z_lat, cos_lat, sin_lat, q_norm_g[j], k_norm_g[j], hq, hkv, hd)
            cqt, ckc, cvtc, cqtd, ckd, cvtd = _attn_prep(z_ctx, cos_ctx, sin_ctx, q_norm_g[j], k_norm_g[j],
                                                         hq, hkv, hd)
            ckc = ckc.reshape(bsz, hkv, n_ctx, HEAD_DIM)
            cvtc = cvtc.reshape(bsz, hkv, V_ROWS, n_ctx)
            ckd = ckd.reshape(bsz, hd, n_ctx, HEAD_DIM)
            cvtd = cvtd.reshape(bsz, hd, V_ROWS, n_ctx)
            ot_c = _gqa(qt, kc, vtc, ckc, cvtc)
            bias = _na_bias_table(na_rel_bias[j], seq // GRID_W)
            ot_d = _na(qtd, kd, vtd, ckd, cvtd, bias)
            mix_lat = (ot_c, ot_d)
            if not last:
                mix_ctx = (_ctx_gqa(cqt, ckc, cvtc), _ctx_gqa(cqtd, ckd, cvtd))
        else:
            w_in = conv_w_in[j].astype(BF16)
            w_out = conv_w_out[j].astype(BF16)
            conv_w = (conv_a_dw_w[j], conv_a_dw_b[j], conv_a_ln_g[j], conv_a_ln_b[j], conv_b_dw_w[j])
            z_lat = _mod_matmul(x_lat, lat_mod(i, 1), lat_mod(i, 0), w_in)
            mix_lat = _conv_mixers(z_lat, *conv_w)
            if not last:
                z_ctx = _mod_matmul(x_ctx, ctx_mod(i, 1), ctx_mod(i, 0), w_in)
                mix_ctx = _conv_mixers(z_ctx, *conv_w)

        n_lat = bsz * seq
        n_tok = n_lat if last else n_lat + bsz * n_ctx
        bufs = None if last else (jnp.zeros((n_tok, d), F32), jnp.zeros((n_tok, ROUTER_COLS), F32))
        x_lat, h_all, lg_all = _proj_ln(mix_lat, w_out, x_lat, lat_mod(i, 2), ln1_g[i], ln1_b[i],
                                        lat_mod(i, 4), lat_mod(i, 3), wr[i], br[i], alpha, n_tok, 0, bufs=bufs)
        moe_w = (expert_w_gate, expert_w_up, expert_w_down, i)
        if last:
            ys, dest, gates = _moe(h_all, lg_all, *moe_w)
            x_lat = _moe_ln(x_lat, ys, dest, gates, 0, lat_mod(i, 5), ln2_g[i], ln2_b[i], alpha)
        else:
            x_ctx, h_all, lg_all = _proj_ln(mix_ctx, w_out, x_ctx, ctx_mod(i, 2), ln1_g[i], ln1_b[i],
                                            ctx_mod(i, 4), ctx_mod(i, 3), wr[i], br[i], alpha, n_tok, n_lat,
                                            bufs=(h_all, lg_all))
            ys, dest, gates = _moe(h_all, lg_all, *moe_w)
            x_lat = _moe_ln(x_lat, ys, dest, gates, 0, lat_mod(i, 5), ln2_g[i], ln2_b[i], alpha)
            x_ctx = _moe_ln(x_ctx, ys, dest, gates, n_lat, ctx_mod(i, 5), ln2_g[i], ln2_b[i], alpha)
    return x_lat
```

```python
import functools
import math

import jax
import jax.numpy as jnp
import numpy as np
from jax import lax
from jax.experimental import pallas as pl
from jax.experimental.pallas import tpu as pltpu

F32 = jnp.float32
BF16 = jnp.bfloat16
HIGHEST = lax.Precision.HIGHEST

GRID_W = 64
HEAD_DIM = 64
V_ROWS = HEAD_DIM + 16
ATTN_SCALE = HEAD_DIM ** -0.5
LOG2E = math.log2(math.e)
CONV_A_WIDTH = 31
CONV_B_WIDTH = 3
CONV_HALO = 16
NA_WIN_ROWS = 8
NA_WIN_COLS = 16
NA_Q_ROWS = 4
NA_BAND_ROWS = 12
NA_BLOCKS_PER_STEP = 8
ROPE_THETA = 10000.0
N_GROUPS = 4
EXPERTS_PER_GROUP = 8
N_EXPERTS = N_GROUPS * EXPERTS_PER_GROUP
LN_EPS = 1e-5
RMS_EPS = 1e-6
NEG_BIG = -1e30

VMEM_LIMIT_BYTES = 48 * 1024 * 1024
LANES = 128
ROUTER_COLS = LANES
EXPERT_BLOCK = 256
RANK_BLOCK = 512


def _cparams(*sem):
    return pltpu.CompilerParams(dimension_semantics=sem, vmem_limit_bytes=VMEM_LIMIT_BYTES)


def _layer_norm(x, g, b):
    mu = jnp.mean(x, axis=-1, keepdims=True)
    xc = x - mu
    var = jnp.mean(xc * xc, axis=-1, keepdims=True)
    return xc * lax.rsqrt(var + LN_EPS) * g + b


def _ada_kernel(c_ref, w_ref, b_ref, o_ref):
    c = c_ref[...]
    s = c * jax.nn.sigmoid(c)
    o_ref[0] = jnp.dot(s, w_ref[0], preferred_element_type=F32, precision=HIGHEST) + b_ref[0]


def _ada(cond8, ada_w, ada_b):
    n_layers, d, n = ada_w.shape
    tn = 1536
    return pl.pallas_call(
        _ada_kernel,
        grid=(n_layers, n // tn),
        in_specs=[
            pl.BlockSpec((8, d), lambda l, j: (0, 0)),
            pl.BlockSpec((1, d, tn), lambda l, j: (l, 0, j)),
            pl.BlockSpec((1, 1, tn), lambda l, j: (l, 0, j)),
        ],
        out_specs=pl.BlockSpec((1, 8, tn), lambda l, j: (l, 0, j)),
        out_shape=jax.ShapeDtypeStruct((n_layers, 8, n), F32),
        compiler_params=_cparams("parallel", "parallel"),
        name="ada_mod",
    )(cond8, ada_w, ada_b.reshape(n_layers, 1, n))


def _mod_matmul_kernel(x_ref, sc_ref, sh_ref, w_ref, o_ref):
    h = x_ref[0] * (1.0 + sc_ref[0]) + sh_ref[0]
    o_ref[0] = jnp.dot(h.astype(BF16), w_ref[...], preferred_element_type=F32)


def _mod_matmul(x, sc, sh, w):
    b, s, d = x.shape
    n = w.shape[1]
    tm = min(s, 512)
    return pl.pallas_call(
        _mod_matmul_kernel,
        grid=(b, s // tm),
        in_specs=[
            pl.BlockSpec((1, tm, d), lambda bi, i: (bi, i, 0)),
            pl.BlockSpec((1, 1, d), lambda bi, i: (bi, 0, 0)),
            pl.BlockSpec((1, 1, d), lambda bi, i: (bi, 0, 0)),
            pl.BlockSpec((d, n), lambda bi, i: (0, 0)),
        ],
        out_specs=pl.BlockSpec((1, tm, n), lambda bi, i: (bi, i, 0)),
        out_shape=jax.ShapeDtypeStruct((b, s, n), F32),
        compiler_params=_cparams("parallel", "parallel"),
        name="mod_matmul",
    )(x, sc, sh, w)


def _conv_kernel(z_ref, zp_ref, zn_ref, aw_ref, ab_ref, ag_ref, abb_ref, bw_ref,
                 o_ref, ea_ref, eb_ref, sh_ref, *, tile, ch):
    i = pl.program_id(1)
    n = pl.num_programs(1)
    halo = CONV_HALO

    def glu(zz):
        return zz[:, 0:ch] * jax.nn.sigmoid(zz[:, ch:2 * ch])

    def cv(zz):
        return zz[:, 3 * ch:4 * ch] * zz[:, 4 * ch:5 * ch]

    pmask = (i > 0).astype(F32)
    nmask = (i < n - 1).astype(F32)
    zp = zp_ref[0]
    zn = zn_ref[0]
    ea_ref[0:halo, :] = glu(zp) * pmask
    eb_ref[0:halo, :] = cv(zp) * pmask
    ea_ref[halo + tile:2 * halo + tile, :] = glu(zn) * nmask
    eb_ref[halo + tile:2 * halo + tile, :] = cv(zn) * nmask
    ea_ref[halo:halo + tile, :] = glu(z_ref[0])
    eb_ref[halo:halo + tile, :] = cv(z_ref[0])

    rows = 32
    pad_a = CONV_A_WIDTH // 2
    pad_b = CONV_B_WIDTH // 2
    n_sh = sh_ref.shape[1]
    piece = 40
    for b in range(1, 8):
        for r in range(0, n_sh, piece):
            sh_ref[b - 1, r:r + piece, :] = ea_ref[pl.ds(r + b, piece), :]
    groups = rows // 8
    for r0 in range(0, tile, rows):
        acc = jnp.zeros((groups, 8, ch), F32) + ab_ref[...]
        for k in range(CONV_A_WIDTH):
            a8, b = divmod(halo + k - pad_a, 8)
            if b == 0:
                tap = ea_ref[pl.ds(r0 + 8 * a8, rows), :]
            else:
                tap = sh_ref[b - 1, pl.ds(r0 + 8 * a8, rows), :]
            acc = acc + aw_ref[8 * k:8 * k + 8, :] * tap.reshape(groups, 8, ch)
        ya = _layer_norm(acc.reshape(rows, ch), ag_ref[...], abb_ref[...])
        ya = ya * jax.nn.sigmoid(ya)
        cb = jnp.zeros((rows, ch), F32)
        for k in range(CONV_B_WIDTH):
            cb = cb + bw_ref[k:k + 1, :] * eb_ref[pl.ds(halo + r0 + k - pad_b, rows), :]
        yb = z_ref[0, r0:r0 + rows, 2 * ch:3 * ch] * cb
        o_ref[0, r0:r0 + rows, 0:ch] = ya.astype(BF16)
        o_ref[0, r0:r0 + rows, ch:2 * ch] = yb.astype(BF16)


def _conv_mixers(z, a_dw_w, a_dw_b, a_ln_g, a_ln_b, b_dw_w):
    b, s, n = z.shape
    ch = n // 5
    tile = min(s, 256)
    hb = tile // CONV_HALO
    last_hb = s // CONV_HALO - 1
    kern = functools.partial(_conv_kernel, tile=tile, ch=ch)
    vec = lambda v: v.reshape(1, ch)
    return pl.pallas_call(
        kern,
        grid=(b, s // tile),
        in_specs=[
            pl.BlockSpec((1, tile, n), lambda bi, i: (bi, i, 0)),
            pl.BlockSpec((1, CONV_HALO, n), lambda bi, i: (bi, jnp.maximum(i * hb - 1, 0), 0)),
            pl.BlockSpec((1, CONV_HALO, n), lambda bi, i: (bi, jnp.minimum((i + 1) * hb, last_hb), 0)),
            pl.BlockSpec((8 * CONV_A_WIDTH, ch), lambda bi, i: (0, 0)),
            pl.BlockSpec((1, ch), lambda bi, i: (0, 0)),
            pl.BlockSpec((1, ch), lambda bi, i: (0, 0)),
            pl.BlockSpec((1, ch), lambda bi, i: (0, 0)),
            pl.BlockSpec((CONV_B_WIDTH, ch), lambda bi, i: (0, 0)),
        ],
        out_specs=pl.BlockSpec((1, tile, 2 * ch), lambda bi, i: (bi, i, 0)),
        out_shape=jax.ShapeDtypeStruct((b, s, 2 * ch), BF16),
        scratch_shapes=[pltpu.VMEM((tile + 2 * CONV_HALO, ch), F32),
                        pltpu.VMEM((tile + 2 * CONV_HALO, ch), F32),
                        pltpu.VMEM((7, tile + 2 * CONV_HALO - 8, ch), F32)],
        compiler_params=_cparams("parallel", "parallel"),
        name="conv_mixers",
    )(z, z, z, jnp.repeat(a_dw_w, 8, axis=0), vec(a_dw_b), vec(a_ln_g), vec(a_ln_b), b_dw_w)


def _proj_ln_kernel(*refs, n_mix, n_alias, alpha):
    mix_refs = refs[:n_mix]
    (w_ref, x_ref, g_ref, lng_ref, lnb_ref, sc_ref, sh_ref, wrh_ref, wrl_ref, br_ref) = refs[n_mix:n_mix + 10]
    xo_ref, h_ref, lg_ref = refs[n_mix + 10 + n_alias:]
    if n_mix == 2:
        ut = jnp.concatenate([r[0].reshape(r.shape[1] * r.shape[2], r.shape[3]) for r in mix_refs], axis=0)
        y = lax.dot_general(ut, w_ref[...], (((0,), (0,)), ((), ())), preferred_element_type=F32)
    else:
        y = jnp.dot(mix_refs[0][0], w_ref[...], preferred_element_type=F32)
    xn = _layer_norm(alpha * x_ref[0] + g_ref[0] * y, lng_ref[...], lnb_ref[...])
    xo_ref[0] = xn
    h2 = xn * (1.0 + sc_ref[0]) + sh_ref[0]
    h_ref[...] = h2.astype(BF16)
    h_hi = h2.astype(BF16)
    h_lo = (h2 - h_hi.astype(F32)).astype(BF16)
    lg_ref[...] = (jnp.dot(h_hi, wrh_ref[...], preferred_element_type=F32)
                   + (jnp.dot(h_lo, wrh_ref[...], preferred_element_type=F32)
                      + jnp.dot(h_hi, wrl_ref[...], preferred_element_type=F32))
                   + br_ref[...])


def _proj_ln(mix, w_out, x, g, ln_g, ln_b, sc2, sh2, wr, br, alpha, n_tok, tok_off, bufs=None):
    b, s, d = x.shape
    tm = min(s, 512)
    nt = s // tm
    off = tok_off // tm
    if isinstance(mix, tuple):
        mix_specs = [pl.BlockSpec((1, o.shape[1], HEAD_DIM, tm), lambda bi, i: (bi, 0, 0, i)) for o in mix]
        mix_args = list(mix)
    else:
        mix_specs = [pl.BlockSpec((1, tm, mix.shape[2]), lambda bi, i: (bi, i, 0))]
        mix_args = [mix]
    row = lambda: pl.BlockSpec((1, tm, d), lambda bi, i: (bi, i, 0))
    per_b = lambda: pl.BlockSpec((1, 1, d), lambda bi, i: (bi, 0, 0))
    vec = lambda: pl.BlockSpec((1, d), lambda bi, i: (0, 0))
    alias_args = [] if bufs is None else list(bufs)
    n_in = len(mix_args) + 10
    wr_hi = wr.astype(BF16)
    wr_lo = (wr - wr_hi.astype(F32)).astype(BF16)
    kern = functools.partial(_proj_ln_kernel, n_mix=len(mix_args), n_alias=len(alias_args), alpha=alpha)
    return pl.pallas_call(
        kern,
        grid=(b, nt),
        in_specs=mix_specs + [
            pl.BlockSpec(w_out.shape, lambda bi, i: (0, 0)),
            row(), per_b(), vec(), vec(), per_b(), per_b(),
            pl.BlockSpec((d, ROUTER_COLS), lambda bi, i: (0, 0)),
            pl.BlockSpec((d, ROUTER_COLS), lambda bi, i: (0, 0)),
            pl.BlockSpec((1, ROUTER_COLS), lambda bi, i: (0, 0)),
        ] + [pl.BlockSpec(memory_space=pl.ANY) for _ in alias_args],
        out_specs=[row(),
                   pl.BlockSpec((tm, d), lambda bi, i: (off + bi * nt + i, 0)),
                   pl.BlockSpec((tm, ROUTER_COLS), lambda bi, i: (off + bi * nt + i, 0))],
        out_shape=[jax.ShapeDtypeStruct((b, s, d), F32),
                   jax.ShapeDtypeStruct((n_tok, d), BF16),
                   jax.ShapeDtypeStruct((n_tok, ROUTER_COLS), F32)],
        input_output_aliases={n_in: 1, n_in + 1: 2} if alias_args else {},
        compiler_params=_cparams("parallel", "parallel"),
        name="proj_res_ln",
    )(*mix_args, w_out, x, g, ln_g.reshape(1, d), ln_b.reshape(1, d), sc2, sh2, wr_hi, wr_lo, br, *alias_args)


def _rank_kernel(e_ref, tri_ref, rank_ref, cnt_ref, carry_ref):
    @pl.when(pl.program_id(0) == 0)
    def _():
        carry_ref[...] = jnp.zeros_like(carry_ref)

    e = e_ref[0]
    blk = e.shape[1]
    ids = lax.broadcasted_iota(jnp.int32, (N_EXPERTS, blk), 0)
    oh = jnp.where(ids == e, 1.0, 0.0).astype(F32)
    before = jnp.dot(oh.astype(BF16), tri_ref[...], preferred_element_type=F32)
    carry = carry_ref[...]
    before = before + carry[:, 0:1]
    rank_ref[0] = jnp.sum(oh * before, axis=0, keepdims=True).astype(jnp.int32)
    carry = carry + jnp.sum(oh, axis=1, keepdims=True)
    carry_ref[...] = carry
    cnt_ref[...] = carry.astype(jnp.int32)


def _expert_ranks(flat_e):
    m = flat_e.shape[0]
    blk = RANK_BLOCK
    nblk = m // blk
    tri = (lax.broadcasted_iota(jnp.int32, (blk, blk), 0)
           < lax.broadcasted_iota(jnp.int32, (blk, blk), 1)).astype(BF16)
    rank, cnt = pl.pallas_call(
        _rank_kernel,
        grid=(nblk,),
        in_specs=[pl.BlockSpec((1, 1, blk), lambda i: (i, 0, 0)),
                  pl.BlockSpec((blk, blk), lambda i: (0, 0))],
        out_specs=[pl.BlockSpec((1, 1, blk), lambda i: (i, 0, 0)),
                   pl.BlockSpec((N_EXPERTS, LANES), lambda i: (0, 0))],
        out_shape=[jax.ShapeDtypeStruct((nblk, 1, blk), jnp.int32),
                   jax.ShapeDtypeStruct((N_EXPERTS, LANES), jnp.int32)],
        scratch_shapes=[pltpu.VMEM((N_EXPERTS, LANES), F32)],
        compiler_params=_cparams("arbitrary"),
        name="expert_rank",
    )(flat_e.reshape(nblk, 1, blk), tri)
    return rank.reshape(m), cnt[:, 0]


def _to_row_tiles(ref, x, n_rows, tiles):
    for s in range(tiles):
        ref[pl.ds(s, n_rows, stride=tiles), :] = x[:, s * LANES:(s + 1) * LANES]


def _from_row_tiles(ref, n_rows, tiles):
    return jnp.concatenate([ref[pl.ds(s, n_rows, stride=tiles), :] for s in range(tiles)], axis=1)


def _dispatch_kernel(dest_ref, x_ref, init_ref, xs_ref, buf_ref, sem, *, tm, tiles):
    del init_ref
    i = pl.program_id(0)
    n = pl.num_programs(0)
    slot = i % 2

    def drain(s):
        for _ in range(2):
            pltpu.make_async_copy(buf_ref.at[s], xs_ref.at[pl.ds(0, tm * tiles), :], sem.at[s]).wait()

    @pl.when(i >= 2)
    def _():
        drain(slot)

    _to_row_tiles(buf_ref.at[slot], x_ref[...].astype(F32), tm, tiles)

    def issue(r, c):
        src = buf_ref.at[slot, pl.ds(pl.multiple_of(r * tiles, tiles), tiles), :]
        for k in range(2):
            d = pl.multiple_of(dest_ref[0, 0, 2 * r + k] * tiles, tiles)
            pltpu.make_async_copy(src, xs_ref.at[pl.ds(d, tiles), :], sem.at[slot]).start(priority=k)
        return c

    lax.fori_loop(0, tm, issue, 0, unroll=4)

    @pl.when(i == n - 1)
    def _():
        @pl.when(i >= 1)
        def _():
            drain(1 - slot)

        drain(slot)


def _dispatch(h_all, dest, cap):
    n, d = h_all.shape
    tiles = d // LANES
    tm = 512
    nt = n // tm
    return pl.pallas_call(
        functools.partial(_dispatch_kernel, tm=tm, tiles=tiles),
        grid=(nt,),
        in_specs=[
            pl.BlockSpec((1, 1, 2 * tm), lambda i: (i, 0, 0), memory_space=pltpu.SMEM),
            pl.BlockSpec((tm, d), lambda i: (i, 0)),
            pl.BlockSpec(memory_space=pl.ANY),
        ],
        out_specs=pl.BlockSpec(memory_space=pl.ANY),
        out_shape=jax.ShapeDtypeStruct((cap * tiles, LANES), F32),
        scratch_shapes=[pltpu.VMEM((2, tm * tiles, LANES), F32), pltpu.SemaphoreType.DMA((2,))],
        input_output_aliases={2: 0},
        compiler_params=_cparams("arbitrary"),
        name="moe_dispatch",
    )(dest.reshape(nt, 1, 2 * tm), h_all, jnp.zeros((cap * tiles, LANES), F32))


def _experts_kernel(be_ref, nu_ref, x_ref, wg_ref, wu_ref, wd_ref, o_ref, wgb_ref, wub_ref, wdb_ref,
                    *, tb, tiles):
    i = pl.program_id(0)
    changed = jnp.logical_or(i == 0, be_ref[i] != be_ref[jnp.maximum(i - 1, 0)])

    @pl.when(changed)
    def _():
        wgb_ref[...] = wg_ref[0, 0].astype(BF16)
        wub_ref[...] = wu_ref[0, 0].astype(BF16)
        wdb_ref[...] = wd_ref[0, 0].astype(BF16)

    @pl.when(i < nu_ref[0])
    def _():
        x = _from_row_tiles(x_ref, tb, tiles).astype(BF16)
        g = jnp.dot(x, wgb_ref[...], preferred_element_type=F32)
        u = jnp.dot(x, wub_ref[...], preferred_element_type=F32)
        a = (g * jax.nn.sigmoid(g) * u).astype(BF16)
        _to_row_tiles(o_ref, jnp.dot(a, wdb_ref[...], preferred_element_type=F32), tb, tiles)

    @pl.when(i >= nu_ref[0])
    def _():
        o_ref[...] = jnp.zeros_like(o_ref)


def _experts(xs, block_e, n_used, w_gate, w_up, w_down, layer):
    d, hid = w_gate.shape[2], w_gate.shape[3]
    tiles = d // LANES
    tb = EXPERT_BLOCK
    nb = xs.shape[0] // (tb * tiles)
    grid_spec = pltpu.PrefetchScalarGridSpec(
        num_scalar_prefetch=2,
        grid=(nb,),
        in_specs=[
            pl.BlockSpec((tb * tiles, LANES), lambda i, be, nu: (i, 0)),
            pl.BlockSpec((1, 1, d, hid), lambda i, be, nu: (layer, be[i], 0, 0)),
            pl.BlockSpec((1, 1, d, hid), lambda i, be, nu: (layer, be[i], 0, 0)),
            pl.BlockSpec((1, 1, hid, d), lambda i, be, nu: (layer, be[i], 0, 0)),
        ],
        out_specs=pl.BlockSpec((tb * tiles, LANES), lambda i, be, nu: (i, 0)),
        scratch_shapes=[pltpu.VMEM((d, hid), BF16), pltpu.VMEM((d, hid), BF16), pltpu.VMEM((hid, d), BF16)],
    )
    return pl.pallas_call(
        functools.partial(_experts_kernel, tb=tb, tiles=tiles),
        grid_spec=grid_spec,
        out_shape=jax.ShapeDtypeStruct(xs.shape, F32),
        compiler_params=_cparams("arbitrary"),
        name="experts",
    )(block_e, n_used, xs, w_gate, w_up, w_down)


def _moe_ln_kernel(dest_ref, dest_next_ref, x_ref, gt_ref, g_ref, lng_ref, lnb_ref, ys_ref, o_ref,
                   y0_ref, y1_ref, sem0, sem1, *, tm, tiles, alpha):
    i = pl.program_id(0)
    n = pl.num_programs(0)
    slot = i % 2

    def gather(idx_ref, slot):
        def issue(r, c):
            dst = pl.ds(pl.multiple_of(r * tiles, tiles), tiles)
            d0 = pl.multiple_of(idx_ref[0, 0, 2 * r] * tiles, tiles)
            d1 = pl.multiple_of(idx_ref[0, 0, 2 * r + 1] * tiles, tiles)
            pltpu.make_async_copy(ys_ref.at[pl.ds(d0, tiles), :], y0_ref.at[slot, dst, :],
                                  sem0.at[slot]).start(priority=0)
            pltpu.make_async_copy(ys_ref.at[pl.ds(d1, tiles), :], y1_ref.at[slot, dst, :],
                                  sem1.at[slot]).start(priority=1)
            return c

        lax.fori_loop(0, tm, issue, 0, unroll=4)

    @pl.when(i == 0)
    def _():
        gather(dest_ref, 0)

    @pl.when(i + 1 < n)
    def _():
        gather(dest_next_ref, 1 - slot)

    pltpu.make_async_copy(ys_ref.at[pl.ds(0, tm * tiles), :], y0_ref.at[slot], sem0.at[slot]).wait()
    pltpu.make_async_copy(ys_ref.at[pl.ds(0, tm * tiles), :], y1_ref.at[slot], sem1.at[slot]).wait()
    gt = gt_ref[...]
    y = (gt[:, 0:1] * _from_row_tiles(y0_ref.at[slot], tm, tiles)
         + gt[:, 1:2] * _from_row_tiles(y1_ref.at[slot], tm, tiles))
    o_ref[...] = _layer_norm(alpha * x_ref[...] + g_ref[0] * y, lng_ref[...], lnb_ref[...])


def _moe_ln(x, ys, dest, gates, tok_off, g, ln_g, ln_b, alpha):
    b, s, d = x.shape
    tiles = d // LANES
    n_tok = gates.shape[0]
    tm = min(s, 512)
    nt = s // tm
    off = tok_off // tm
    n_steps = b * nt
    vec = lambda: pl.BlockSpec((1, d), lambda i: (0, 0))
    dest_blocks = dest.reshape(n_tok // tm, 1, 2 * tm)
    out = pl.pallas_call(
        functools.partial(_moe_ln_kernel, tm=tm, tiles=tiles, alpha=alpha),
        grid=(n_steps,),
        in_specs=[
            pl.BlockSpec((1, 1, 2 * tm), lambda i: (off + i, 0, 0), memory_space=pltpu.SMEM),
            pl.BlockSpec((1, 1, 2 * tm), lambda i: (off + jnp.minimum(i + 1, n_steps - 1), 0, 0),
                         memory_space=pltpu.SMEM),
            pl.BlockSpec((tm, d), lambda i: (i, 0)),
            pl.BlockSpec((tm, 2), lambda i: (off + i, 0)),
            pl.BlockSpec((1, 1, d), lambda i: (i // nt, 0, 0)),
            vec(), vec(),
            pl.BlockSpec(memory_space=pl.ANY),
        ],
        out_specs=pl.BlockSpec((tm, d), lambda i: (i, 0)),
        out_shape=jax.ShapeDtypeStruct((b * s, d), F32),
        scratch_shapes=[pltpu.VMEM((2, tm * tiles, LANES), F32), pltpu.VMEM((2, tm * tiles, LANES), F32),
                        pltpu.SemaphoreType.DMA((2,)), pltpu.SemaphoreType.DMA((2,))],
        compiler_params=_cparams("arbitrary"),
        name="moe_res_ln",
    )(dest_blocks, dest_blocks, x.reshape(b * s, d), gates, g, ln_g.reshape(1, d), ln_b.reshape(1, d), ys)
    return out.reshape(b, s, d)


def _route(logits):
    n = logits.shape[0]
    l1 = logits[:, :N_GROUPS]
    grp = jnp.argmax(l1, axis=-1).astype(jnp.int32)
    p_grp = 1.0 / jnp.sum(jnp.exp(l1 - jnp.max(l1, axis=-1, keepdims=True)), axis=-1)
    lg2 = logits[:, N_GROUPS:N_GROUPS + N_EXPERTS].reshape(n, N_GROUPS, EXPERTS_PER_GROUP)
    gsel = grp[:, None] == jnp.arange(N_GROUPS, dtype=jnp.int32)[None, :]
    sel = jnp.sum(jnp.where(gsel[:, :, None], lg2, 0.0), axis=1)
    i1 = jnp.argmax(sel, axis=-1).astype(jnp.int32)
    v1 = jnp.max(sel, axis=-1)
    col = jnp.arange(EXPERTS_PER_GROUP, dtype=jnp.int32)[None, :]
    sel2 = jnp.where(col == i1[:, None], -jnp.inf, sel)
    i2 = jnp.argmax(sel2, axis=-1).astype(jnp.int32)
    v2 = jnp.max(sel2, axis=-1)
    e2 = jnp.exp(v2 - v1)
    w1 = 1.0 / (1.0 + e2)
    gates = p_grp[:, None] * jnp.stack([w1, e2 * w1], axis=-1)
    experts = grp[:, None] * EXPERTS_PER_GROUP + jnp.stack([i1, i2], axis=-1)
    return experts, gates.astype(F32)


def _moe(h_all, logits, w_gate, w_up, w_down, layer):
    n, d = h_all.shape
    m = 2 * n
    tb = EXPERT_BLOCK
    experts, gates = _route(logits)
    flat_e = experts.reshape(m)
    rank, counts = _expert_ranks(flat_e)
    padded = (counts + tb - 1) // tb * tb
    pad_end = jnp.cumsum(padded)
    pad_start = pad_end - padded
    onehot = flat_e[:, None] == jnp.arange(N_EXPERTS, dtype=jnp.int32)[None, :]
    dest = jnp.sum(jnp.where(onehot, pad_start[None, :], 0), axis=-1).astype(jnp.int32) + rank
    nb = -(-m // tb) + N_EXPERTS
    cap = nb * tb
    blk_start = jnp.arange(nb, dtype=jnp.int32) * tb
    block_e = jnp.minimum(jnp.sum(pad_end[None, :] <= blk_start[:, None], axis=-1), N_EXPERTS - 1)
    n_used = (pad_end[-1] // tb).reshape(1)

    xs = _dispatch(h_all, dest, cap)
    ys = _experts(xs, block_e.astype(jnp.int32), n_used.astype(jnp.int32), w_gate, w_up, w_down, layer)
    return ys, dest, gates


def _prep_kernel(z_ref, cos_ref, sin_ref, gq_ref, gk_ref, grp_ref,
                 qt_ref, kc_ref, vt_ref, qtd_ref, kd_ref, vtd_ref, *, hq, hkv, hd, ts, dchunk):
    cos = cos_ref[...]
    sin = sin_ref[...]
    lane = lax.broadcasted_iota(jnp.int32, (ts, LANES), 1)
    first_half = (lane % 32) < 16

    def norm_rope(x, g):
        ss = jnp.dot(x * x, grp_ref[...], preferred_element_type=F32, precision=HIGHEST)
        xn = x * lax.rsqrt(ss * (1.0 / HEAD_DIM) + RMS_EPS) * g
        partner = jnp.where(first_half, pltpu.roll(xn, LANES - 16, 1), pltpu.roll(xn, 16, 1))
        return xn * cos + partner * sin

    col = 0
    for s in range(hq // 2):
        x = norm_rope(z_ref[0, :, col:col + LANES], gq_ref[...]) * (ATTN_SCALE * LOG2E)
        t = x.T.astype(BF16)
        qt_ref[0, 2 * s] = t[0:HEAD_DIM]
        qt_ref[0, 2 * s + 1] = t[HEAD_DIM:]
        col += LANES
    for s in range(hkv // 2):
        x = norm_rope(z_ref[0, :, col:col + LANES], gk_ref[...]).astype(BF16)
        kc_ref[0, 2 * s, 0] = x[:, 0:HEAD_DIM]
        kc_ref[0, 2 * s + 1, 0] = x[:, HEAD_DIM:]
        col += LANES
    tail = (lax.broadcasted_iota(jnp.int32, (V_ROWS - HEAD_DIM, ts), 0) == 0).astype(BF16)
    tail_d = (lax.broadcasted_iota(jnp.int32, (V_ROWS - HEAD_DIM, dchunk), 0) == 0).astype(BF16)
    for s in range(hkv // 2):
        t = z_ref[0, :, col:col + LANES].T.astype(BF16)
        for hh in range(2):
            vt_ref[0, 2 * s + hh, 0, 0:HEAD_DIM, :] = t[hh * HEAD_DIM:(hh + 1) * HEAD_DIM]
            vt_ref[0, 2 * s + hh, 0, HEAD_DIM:V_ROWS, :] = tail
        col += LANES
    for s in range(hd // 2):
        t = (z_ref[0, :, col:col + LANES] * (ATTN_SCALE * LOG2E)).T.astype(BF16)
        qtd_ref[0, 2 * s] = t[0:HEAD_DIM]
        qtd_ref[0, 2 * s + 1] = t[HEAD_DIM:]
        col += LANES
    for s in range(hd // 2):
        x = z_ref[0, :, col:col + LANES].astype(BF16)
        for c in range(ts // dchunk):
            kd_ref[0, 2 * s, c] = x[c * dchunk:(c + 1) * dchunk, 0:HEAD_DIM]
            kd_ref[0, 2 * s + 1, c] = x[c * dchunk:(c + 1) * dchunk, HEAD_DIM:]
        col += LANES
    for s in range(hd // 2):
        t = z_ref[0, :, col:col + LANES].T.astype(BF16)
        for c in range(ts // dchunk):
            for hh in range(2):
                vtd_ref[0, 2 * s + hh, c, 0:HEAD_DIM, :] = t[hh * HEAD_DIM:(hh + 1) * HEAD_DIM,
                                                             c * dchunk:(c + 1) * dchunk]
                vtd_ref[0, 2 * s + hh, c, HEAD_DIM:V_ROWS, :] = tail_d
        col += LANES


def _attn_prep(z, cos, sin, q_g, k_g, hq, hkv, hd):
    b, s, n = z.shape
    ts = min(s, 512)
    dchunk = 256
    nt = s // ts
    hdim = HEAD_DIM
    gq = jnp.tile(q_g, 2).reshape(1, LANES)
    gk = jnp.tile(k_g, 2).reshape(1, LANES)
    li = jnp.arange(LANES)
    grp = (li[:, None] // hdim == li[None, :] // hdim).astype(F32)
    kern = functools.partial(_prep_kernel, hq=hq, hkv=hkv, hd=hd, ts=ts, dchunk=dchunk)
    const = lambda shape: pl.BlockSpec(shape, lambda bi, i: (0,) * len(shape))
    return pl.pallas_call(
        kern,
        grid=(b, nt),
        in_specs=[
            pl.BlockSpec((1, ts, n), lambda bi, i: (bi, i, 0)),
            pl.BlockSpec((ts, LANES), lambda bi, i: (i, 0)),
            pl.BlockSpec((ts, LANES), lambda bi, i: (i, 0)),
            const((1, LANES)), const((1, LANES)), const((LANES, LANES)),
        ],
        out_specs=[
            pl.BlockSpec((1, hq, hdim, ts), lambda bi, i: (bi, 0, 0, i)),
            pl.BlockSpec((1, hkv, 1, ts, hdim), lambda bi, i: (bi, 0, i, 0, 0)),
            pl.BlockSpec((1, hkv, 1, V_ROWS, ts), lambda bi, i: (bi, 0, i, 0, 0)),
            pl.BlockSpec((1, hd, hdim, ts), lambda bi, i: (bi, 0, 0, i)),
            pl.BlockSpec((1, hd, ts // dchunk, dchunk, hdim), lambda bi, i: (bi, 0, i, 0, 0)),
            pl.BlockSpec((1, hd, ts // dchunk, V_ROWS, dchunk), lambda bi, i: (bi, 0, i, 0, 0)),
        ],
        out_shape=[
            jax.ShapeDtypeStruct((b, hq, hdim, s), BF16),
            jax.ShapeDtypeStruct((b, hkv, nt, ts, hdim), BF16),
            jax.ShapeDtypeStruct((b, hkv, nt, V_ROWS, ts), BF16),
            jax.ShapeDtypeStruct((b, hd, hdim, s), BF16),
            jax.ShapeDtypeStruct((b, hd, s // dchunk, dchunk, hdim), BF16),
            jax.ShapeDtypeStruct((b, hd, s // dchunk, V_ROWS, dchunk), BF16),
        ],
        compiler_params=_cparams("parallel", "parallel"),
        name="attn_prep",
    )(z, cos, sin, gq, gk, grp)


def _rope_tables(s):
    t = jnp.arange(s)
    row = (t // GRID_W).astype(F32)
    colp = (t % GRID_W).astype(F32)
    axis_dim = HEAD_DIM // 2
    inv = ROPE_THETA ** (-jnp.arange(0, axis_dim, 2, dtype=F32) / axis_dim)
    ar = row[:, None] * inv
    ac = colp[:, None] * inv
    cr, sr, cc, sc = jnp.cos(ar), jnp.sin(ar), jnp.cos(ac), jnp.sin(ac)
    cos = jnp.concatenate([cr, cr, cc, cc], axis=-1)
    sin = jnp.concatenate([-sr, sr, -sc, sc], axis=-1)
    return jnp.tile(cos, (1, 2)), jnp.tile(sin, (1, 2))


def _gqa_kernel(qt_ref, k_ref, vt_ref, kc_ref, vtc_ref, o_ref, acc_ref, s_ref, sc_ref, *, group):
    n_chunks = k_ref.shape[2]
    tq = qt_ref.shape[3]

    def q_all():
        return jnp.concatenate([qt_ref[0, g] for g in range(group)], axis=1)

    def scores(kb, slot):
        st = jnp.dot(kb, q_all(), preferred_element_type=F32)
        s_ref[slot] = st
        return jnp.max(st, axis=0, keepdims=True)

    def absorb(slot, vb, cmax, m):
        m_new = jnp.maximum(m, cmax)
        alpha = jnp.exp2(m - m_new)
        p = jnp.exp2(s_ref[slot] - m_new)
        acc_ref[...] = alpha * acc_ref[...] + jnp.dot(vb, p.astype(BF16), preferred_element_type=F32)
        return m_new

    per_trip = 16 if n_chunks % 16 == 0 else 2
    st = jnp.dot(kc_ref[0, 0], q_all(), preferred_element_type=F32)
    sc_ref[...] = st
    m = jnp.max(st, axis=0, keepdims=True)
    cm = scores(k_ref[0, 0, 0], 0)
    p = jnp.exp2(sc_ref[...] - m)
    acc_ref[...] = jnp.dot(vtc_ref[0, 0], p.astype(BF16), preferred_element_type=F32)

    def body(i, carry):
        m, cm_cur = carry
        j0 = per_trip * i
        for u in range(per_trip):
            cm_next = scores(k_ref[0, 0, jnp.minimum(j0 + u + 1, n_chunks - 1)], (u + 1) % 2)
            m = absorb(u % 2, vt_ref[0, 0, j0 + u], cm_cur, m)
            cm_cur = cm_next
        return m, cm_cur

    lax.fori_loop(0, n_chunks // per_trip, body, (m, cm))
    out = acc_ref[0:HEAD_DIM, :] / acc_ref[HEAD_DIM:HEAD_DIM + 1, :]
    for g in range(group):
        o_ref[0, g] = out[:, g * tq:(g + 1) * tq].astype(BF16)


def _gqa(qt, k, vt, k_ctx, vt_ctx):
    b, hq, hdim, s = qt.shape
    hkv, n_chunks, tk = k.shape[1], k.shape[2], k.shape[3]
    n_ctx = k_ctx.shape[2]
    group = hq // hkv
    tq = min(s, 256)
    return pl.pallas_call(
        functools.partial(_gqa_kernel, group=group),
        grid=(b, hkv, s // tq),
        in_specs=[
            pl.BlockSpec((1, group, hdim, tq), lambda bi, h, i: (bi, h, 0, i)),
            pl.BlockSpec((1, 1, n_chunks, tk, hdim), lambda bi, h, i: (bi, h, 0, 0, 0)),
            pl.BlockSpec((1, 1, n_chunks, V_ROWS, tk), lambda bi, h, i: (bi, h, 0, 0, 0)),
            pl.BlockSpec((1, 1, n_ctx, hdim), lambda bi, h, i: (bi, h, 0, 0)),
            pl.BlockSpec((1, 1, V_ROWS, n_ctx), lambda bi, h, i: (bi, h, 0, 0)),
        ],
        out_specs=pl.BlockSpec((1, group, hdim, tq), lambda bi, h, i: (bi, h, 0, i)),
        out_shape=jax.ShapeDtypeStruct((b, hq, hdim, s), BF16),
        scratch_shapes=[pltpu.VMEM((V_ROWS, group * tq), F32), pltpu.VMEM((2, tk, group * tq), F32),
                        pltpu.VMEM((n_ctx, group * tq), F32)],
        compiler_params=_cparams("parallel", "parallel", "parallel"),
        name="gqa_attention",
    )(qt, k, vt, k_ctx, vt_ctx)


def _ctx_gqa_kernel(qt_ref, k_ref, vt_ref, o_ref):
    st = jnp.dot(k_ref[0, 0], qt_ref[0, 0], preferred_element_type=F32)
    p = jnp.exp2(st - jnp.max(st, axis=0, keepdims=True))
    acc = jnp.dot(vt_ref[0, 0], p.astype(BF16), preferred_element_type=F32)
    o_ref[0, 0] = (acc[0:HEAD_DIM] / acc[HEAD_DIM:HEAD_DIM + 1]).astype(BF16)


def _ctx_gqa(qt, k, vt):
    b, hq, hdim, n = qt.shape
    group = hq // k.shape[1]
    return pl.pallas_call(
        _ctx_gqa_kernel,
        grid=(b, hq),
        in_specs=[
            pl.BlockSpec((1, 1, hdim, n), lambda bi, h: (bi, h, 0, 0)),
            pl.BlockSpec((1, 1, n, hdim), lambda bi, h: (bi, h // group, 0, 0)),
            pl.BlockSpec((1, 1, V_ROWS, n), lambda bi, h: (bi, h // group, 0, 0)),
        ],
        out_specs=pl.BlockSpec((1, 1, hdim, n), lambda bi, h: (bi, h, 0, 0)),
        out_shape=jax.ShapeDtypeStruct((b, hq, hdim, n), BF16),
        compiler_params=_cparams("parallel", "parallel"),
        name="ctx_gqa",
    )(qt, k, vt)


def _na_kernel(qt_ref, k_ref, vt_ref, kc_ref, vtc_ref, bias_ref, o_ref, s_ref, *, n_band, nq, per_step):
    step = pl.program_id(2)
    tq = NA_Q_ROWS * GRID_W
    cw = k_ref.shape[3]
    n_ctx = kc_ref.shape[2]

    def scores(j, slot):
        i = step * per_step + j
        c0 = jnp.clip(i - 1, 0, nq - n_band)
        variant = jnp.where(i == 0, 0, jnp.where(i == nq - 1, 2, 1))
        qt = qt_ref[0, 0, :, j * tq:(j + 1) * tq]
        st = jnp.dot(kc_ref[0, 0], qt, preferred_element_type=F32)
        s_ref[slot, 0:n_ctx, :] = st
        m = jnp.max(st, axis=0, keepdims=True)
        for c in range(n_band):
            b0 = pl.multiple_of((variant * n_band + c) * cw, cw)
            st = (jnp.dot(k_ref[0, 0, c0 + c], qt, preferred_element_type=F32)
                  + bias_ref[0, pl.ds(b0, cw), :])
            s_ref[slot, n_ctx + c * cw:n_ctx + (c + 1) * cw, :] = st
            m = jnp.maximum(m, jnp.max(st, axis=0, keepdims=True))
        return m, c0

    def finish(j, slot, m, c0):
        p = jnp.exp2(s_ref[slot, 0:n_ctx, :] - m).astype(BF16)
        acc = jnp.dot(vtc_ref[0, 0], p, preferred_element_type=F32)
        for c in range(n_band):
            p = jnp.exp2(s_ref[slot, n_ctx + c * cw:n_ctx + (c + 1) * cw, :] - m).astype(BF16)
            acc = acc + jnp.dot(vt_ref[0, 0, c0 + c], p, preferred_element_type=F32)
        o_ref[0, 0, :, j * tq:(j + 1) * tq] = (acc[0:HEAD_DIM] / acc[HEAD_DIM:HEAD_DIM + 1]).astype(BF16)

    nxt = scores(0, 0)
    for j in range(per_step):
        cur = nxt
        if j + 1 < per_step:
            nxt = scores(j + 1, (j + 1) % 2)
        finish(j, j % 2, *cur)


def _na_bias_table(rel_bias, rows):
    w = GRID_W
    h, n_rel_rows, n_rel_cols = rel_bias.shape
    qcol = np.arange(w)
    cstart = np.clip(qcol - NA_WIN_COLS // 2, 0, w - NA_WIN_COLS)
    in_win = (qcol[None, :] >= cstart[:, None]) & (qcol[None, :] < cstart[:, None] + NA_WIN_COLS)
    col_idx = np.clip(qcol[None, :] - qcol[:, None] + NA_WIN_COLS - 1, 0, n_rel_cols - 1)
    onehot = (col_idx.reshape(-1)[None, :] == np.arange(n_rel_cols)[:, None]).astype(np.float32)
    cols = jnp.einsum('hrj,jn->hrn', rel_bias, onehot, precision=HIGHEST).reshape(h, n_rel_rows, w, w)
    cols = jnp.where(in_win[None, None], cols, NEG_BIG)
    ext = jnp.concatenate([cols, jnp.full((h, 1, w, w), NEG_BIG, F32)], axis=1)
    a = np.arange(NA_Q_ROWS)[:, None]
    c = np.arange(NA_BAND_ROWS)[None, :]
    tables = []
    for r0, rs0 in ((0, 0), (NA_Q_ROWS, 0), (rows - NA_Q_ROWS, rows - NA_BAND_ROWS)):
        r = r0 + a
        kr = rs0 + c
        rs = np.clip(r - NA_WIN_ROWS // 2, 0, rows - NA_WIN_ROWS)
        valid = (kr >= rs) & (kr < rs + NA_WIN_ROWS)
        row_idx = np.where(valid, kr - r + NA_WIN_ROWS - 1, n_rel_rows)
        bias = ext[:, row_idx.reshape(-1)].reshape(h, NA_Q_ROWS, NA_BAND_ROWS, w, w)
        tables.append(bias.transpose(0, 2, 4, 1, 3).reshape(h, NA_BAND_ROWS * w, NA_Q_ROWS * w))
    return (jnp.stack(tables, axis=1) * LOG2E).astype(F32)


def _na(qt, k, vt, k_ctx, vt_ctx, bias):
    b, h, hdim, s = qt.shape
    n_chunks, cw = k.shape[2], k.shape[3]
    n_ctx = k_ctx.shape[2]
    tq = NA_Q_ROWS * GRID_W
    nq = s // tq
    n_band = NA_BAND_ROWS * GRID_W // cw
    band_w = NA_BAND_ROWS * GRID_W
    per_step = math.gcd(nq, NA_BLOCKS_PER_STEP)
    return pl.pallas_call(
        functools.partial(_na_kernel, n_band=n_band, nq=nq, per_step=per_step),
        grid=(b, h, nq // per_step),
        in_specs=[
            pl.BlockSpec((1, 1, hdim, per_step * tq), lambda bi, hi, i: (bi, hi, 0, i)),
            pl.BlockSpec((1, 1, n_chunks, cw, hdim), lambda bi, hi, i: (bi, hi, 0, 0, 0)),
            pl.BlockSpec((1, 1, n_chunks, V_ROWS, cw), lambda bi, hi, i: (bi, hi, 0, 0, 0)),
            pl.BlockSpec((1, 1, n_ctx, hdim), lambda bi, hi, i: (bi, hi, 0, 0)),
            pl.BlockSpec((1, 1, V_ROWS, n_ctx), lambda bi, hi, i: (bi, hi, 0, 0)),
            pl.BlockSpec((1, 3 * band_w, tq), lambda bi, hi, i: (hi, 0, 0)),
        ],
        out_specs=pl.BlockSpec((1, 1, hdim, per_step * tq), lambda bi, hi, i: (bi, hi, 0, i)),
        out_shape=jax.ShapeDtypeStruct((b, h, hdim, s), BF16),
        scratch_shapes=[pltpu.VMEM((2, n_ctx + band_w, tq), F32)],
        compiler_params=_cparams("parallel", "parallel", "parallel"),
        name="na_attention",
    )(qt, k, vt, k_ctx, vt_ctx, bias.reshape(h, 3 * band_w, tq))


def kernel(x, c, ctx, c_ctx, ada_w, ada_b, ln1_g, ln1_b, ln2_g, ln2_b, conv_w_in, conv_w_out, conv_a_dw_w, conv_a_dw_b, conv_a_ln_g, conv_a_ln_b, conv_b_dw_w, attn_w_in, attn_w_out, q_norm_g, k_norm_g, na_rel_bias, router_w1, router_b1, router_w2, router_b2, expert_w_gate, expert_w_up, expert_w_down):
    bsz, seq, d = x.shape
    n_ctx = ctx.shape[1]
    depth = ada_w.shape[0]
    alpha = (2 * depth) ** 0.25
    hq = (d // 2) // HEAD_DIM
    hkv = hq // 4
    hd = (d // 2) // HEAD_DIM

    cond8 = jnp.zeros((8, d), F32).at[:bsz].set(c).at[bsz].set(c_ctx)
    mods = _ada(cond8, ada_w, ada_b)

    def lat_mod(i, k):
        return mods[i, :bsz, k * d:(k + 1) * d].reshape(bsz, 1, d)

    def ctx_mod(i, k):
        return jnp.broadcast_to(mods[i, bsz, k * d:(k + 1) * d].reshape(1, 1, d), (bsz, 1, d))

    wr = jnp.zeros((depth, d, ROUTER_COLS), F32)
    wr = wr.at[:, :, :N_GROUPS].set(router_w1).at[:, :, N_GROUPS:N_GROUPS + N_EXPERTS].set(router_w2)
    br = jnp.zeros((depth, 1, ROUTER_COLS), F32)
    br = br.at[:, 0, :N_GROUPS].set(router_b1).at[:, 0, N_GROUPS:N_GROUPS + N_EXPERTS].set(router_b2)

    cos_lat, sin_lat = _rope_tables(seq)
    cos_ctx = jnp.ones((n_ctx, LANES), F32)
    sin_ctx = jnp.zeros((n_ctx, LANES), F32)

    x_lat, x_ctx = x, ctx
    for i in range(depth):
        last = i == depth - 1
        odd = i % 2 == 1
        j = i // 2
        use_ctx = odd or not last
        if odd:
            w_in = attn_w_in[j].astype(BF16)
            w_out = attn_w_out[j].astype(BF16)
            z_lat = _mod_matmul(x_lat, lat_mod(i, 1), lat_mod(i, 0), w_in)
            z_ctx = _mod_matmul(x_ctx, ctx_mod(i, 1), ctx_mod(i, 0), w_in)
            qt, kc, vtc, qtd, kd, vtd = _attn_prep(z_lat, cos_lat, sin_lat, q_norm_g[j], k_norm_g[j], hq, hkv, hd)
            cqt, ckc, cvtc, cqtd, ckd, cvtd = _attn_prep(z_ctx, cos_ctx, sin_ctx, q_norm_g[j], k_norm_g[j],
                                                         hq, hkv, hd)
            ckc = ckc.reshape(bsz, hkv, n_ctx, HEAD_DIM)
            cvtc = cvtc.reshape(bsz, hkv, V_ROWS, n_ctx)
            ckd = ckd.reshape(bsz, hd, n_ctx, HEAD_DIM)
            cvtd = cvtd.reshape(bsz, hd, V_ROWS, n_ctx)
            ot_c = _gqa(qt, kc, vtc, ckc, cvtc)
            bias = _na_bias_table(na_rel_bias[j], seq // GRID_W)
            ot_d = _na(qtd, kd, vtd, ckd, cvtd, bias)
            mix_lat = (ot_c, ot_d)
            if not last:
                mix_ctx = (_ctx_gqa(cqt, ckc, cvtc), _ctx_gqa(cqtd, ckd, cvtd))
        else:
            w_in = conv_w_in[j].astype(BF16)
            w_out = conv_w_out[j].astype(BF16)
            conv_w = (conv_a_dw_w[j], conv_a_dw_b[j], conv_a_ln_g[j], conv_a_ln_b[j], conv_b_dw_w[j])
            z_lat = _mod_matmul(x_lat, lat_mod(i, 1), lat_mod(i, 0), w_in)
            mix_lat = _conv_mixers(z_lat, *conv_w)
            if not last:
                z_ctx = _mod_matmul(x_ctx, ctx_mod(i, 1), ctx_mod(i, 0), w_in)
                mix_ctx = _conv_mixers(z_ctx, *conv_w)

        n_lat = bsz * seq
        n_tok = n_lat if last else n_lat + bsz * n_ctx
        bufs = None if last else (jnp.zeros((n_tok, d), BF16), jnp.zeros((n_tok, ROUTER_COLS), F32))
        x_lat, h_all, lg_all = _proj_ln(mix_lat, w_out, x_lat, lat_mod(i, 2), ln1_g[i], ln1_b[i],
                                        lat_mod(i, 4), lat_mod(i, 3), wr[i], br[i], alpha, n_tok, 0, bufs=bufs)
        moe_w = (expert_w_gate, expert_w_up, expert_w_down, i)
        if last:
            ys, dest, gates = _moe(h_all, lg_all, *moe_w)
            x_lat = _moe_ln(x_lat, ys, dest, gates, 0, lat_mod(i, 5), ln2_g[i], ln2_b[i], alpha)
        else:
            x_ctx, h_all, lg_all = _proj_ln(mix_ctx, w_out, x_ctx, ctx_mod(i, 2), ln1_g[i], ln1_b[i],
                                            ctx_mod(i, 4), ctx_mod(i, 3), wr[i], br[i], alpha, n_tok, n_lat,
                                            bufs=(h_all, lg_all))
            ys, dest, gates = _moe(h_all, lg_all, *moe_w)
            x_lat = _moe_ln(x_lat, ys, dest, gates, 0, lat_mod(i, 5), ln2_g[i], ln2_b[i], alpha)
            x_ctx = _moe_ln(x_ctx, ys, dest, gates, n_lat, ctx_mod(i, 5), ln2_g[i], ln2_b[i], alpha)
    return x_lat
```

```python
import functools
import math

import jax
import jax.numpy as jnp
import numpy as np
from jax import lax
from jax.experimental import pallas as pl
from jax.experimental.pallas import tpu as pltpu

F32 = jnp.float32
BF16 = jnp.bfloat16
HIGHEST = lax.Precision.HIGHEST

GRID_W = 64
HEAD_DIM = 64
V_ROWS = HEAD_DIM + 16
ATTN_SCALE = HEAD_DIM ** -0.5
LOG2E = math.log2(math.e)
CONV_A_WIDTH = 31
CONV_B_WIDTH = 3
CONV_HALO = 16
NA_WIN_ROWS = 8
NA_WIN_COLS = 16
NA_Q_ROWS = 4
NA_BAND_ROWS = 12
NA_BLOCKS_PER_STEP = 8
ROPE_THETA = 10000.0
N_GROUPS = 4
EXPERTS_PER_GROUP = 8
N_EXPERTS = N_GROUPS * EXPERTS_PER_GROUP
LN_EPS = 1e-5
RMS_EPS = 1e-6
NEG_BIG = -1e30

VMEM_LIMIT_BYTES = 48 * 1024 * 1024
LANES = 128
ROUTER_COLS = LANES
EXPERT_BLOCK = 256
RANK_BLOCK = 512


def _cparams(*sem):
    return pltpu.CompilerParams(dimension_semantics=sem, vmem_limit_bytes=VMEM_LIMIT_BYTES)


def _layer_norm(x, g, b):
    mu = jnp.mean(x, axis=-1, keepdims=True)
    xc = x - mu
    var = jnp.mean(xc * xc, axis=-1, keepdims=True)
    return xc * lax.rsqrt(var + LN_EPS) * g + b


def _ada_kernel(c_ref, w_ref, b_ref, o_ref):
    c = c_ref[...]
    s = c * jax.nn.sigmoid(c)
    o_ref[0] = jnp.dot(s, w_ref[0], preferred_element_type=F32, precision=HIGHEST) + b_ref[0]


def _ada(cond8, ada_w, ada_b):
    n_layers, d, n = ada_w.shape
    tn = 1536
    return pl.pallas_call(
        _ada_kernel,
        grid=(n_layers, n // tn),
        in_specs=[
            pl.BlockSpec((8, d), lambda l, j: (0, 0)),
            pl.BlockSpec((1, d, tn), lambda l, j: (l, 0, j)),
            pl.BlockSpec((1, 1, tn), lambda l, j: (l, 0, j)),
        ],
        out_specs=pl.BlockSpec((1, 8, tn), lambda l, j: (l, 0, j)),
        out_shape=jax.ShapeDtypeStruct((n_layers, 8, n), F32),
        compiler_params=_cparams("parallel", "parallel"),
        name="ada_mod",
    )(cond8, ada_w, ada_b.reshape(n_layers, 1, n))


def _mod_matmul_kernel(x_ref, sc_ref, sh_ref, w_ref, o_ref):
    h = x_ref[0] * (1.0 + sc_ref[0]) + sh_ref[0]
    o_ref[0] = jnp.dot(h.astype(BF16), w_ref[...], preferred_element_type=F32)


def _mod_matmul(x, sc, sh, w):
    b, s, d = x.shape
    n = w.shape[1]
    tm = min(s, 512)
    return pl.pallas_call(
        _mod_matmul_kernel,
        grid=(b, s // tm),
        in_specs=[
            pl.BlockSpec((1, tm, d), lambda bi, i: (bi, i, 0)),
            pl.BlockSpec((1, 1, d), lambda bi, i: (bi, 0, 0)),
            pl.BlockSpec((1, 1, d), lambda bi, i: (bi, 0, 0)),
            pl.BlockSpec((d, n), lambda bi, i: (0, 0)),
        ],
        out_specs=pl.BlockSpec((1, tm, n), lambda bi, i: (bi, i, 0)),
        out_shape=jax.ShapeDtypeStruct((b, s, n), F32),
        compiler_params=_cparams("parallel", "parallel"),
        name="mod_matmul",
    )(x, sc, sh, w)


def _conv_kernel(z_ref, zp_ref, zn_ref, aw_ref, ab_ref, ag_ref, abb_ref, bw_ref,
                 o_ref, ea_ref, eb_ref, sh_ref, *, tile, ch):
    i = pl.program_id(1)
    n = pl.num_programs(1)
    halo = CONV_HALO

    def glu(zz):
        return zz[:, 0:ch] * jax.nn.sigmoid(zz[:, ch:2 * ch])

    def cv(zz):
        return zz[:, 3 * ch:4 * ch] * zz[:, 4 * ch:5 * ch]

    pmask = (i > 0).astype(F32)
    nmask = (i < n - 1).astype(F32)
    zp = zp_ref[0]
    zn = zn_ref[0]
    ea_ref[0:halo, :] = glu(zp) * pmask
    eb_ref[0:halo, :] = cv(zp) * pmask
    ea_ref[halo + tile:2 * halo + tile, :] = glu(zn) * nmask
    eb_ref[halo + tile:2 * halo + tile, :] = cv(zn) * nmask
    ea_ref[halo:halo + tile, :] = glu(z_ref[0])
    eb_ref[halo:halo + tile, :] = cv(z_ref[0])

    rows = 32
    pad_a = CONV_A_WIDTH // 2
    pad_b = CONV_B_WIDTH // 2
    n_sh = sh_ref.shape[1]
    piece = 40
    for b in range(1, 8):
        for r in range(0, n_sh, piece):
            sh_ref[b - 1, r:r + piece, :] = ea_ref[pl.ds(r + b, piece), :]
    groups = rows // 8
    for r0 in range(0, tile, rows):
        acc = jnp.zeros((groups, 8, ch), F32) + ab_ref[...]
        for k in range(CONV_A_WIDTH):
            a8, b = divmod(halo + k - pad_a, 8)
            if b == 0:
                tap = ea_ref[pl.ds(r0 + 8 * a8, rows), :]
            else:
                tap = sh_ref[b - 1, pl.ds(r0 + 8 * a8, rows), :]
            acc = acc + aw_ref[8 * k:8 * k + 8, :] * tap.reshape(groups, 8, ch)
        ya = _layer_norm(acc.reshape(rows, ch), ag_ref[...], abb_ref[...])
        ya = ya * jax.nn.sigmoid(ya)
        cb = jnp.zeros((rows, ch), F32)
        for k in range(CONV_B_WIDTH):
            cb = cb + bw_ref[k:k + 1, :] * eb_ref[pl.ds(halo + r0 + k - pad_b, rows), :]
        yb = z_ref[0, r0:r0 + rows, 2 * ch:3 * ch] * cb
        o_ref[0, r0:r0 + rows, 0:ch] = ya.astype(BF16)
        o_ref[0, r0:r0 + rows, ch:2 * ch] = yb.astype(BF16)


def _conv_mixers(z, a_dw_w, a_dw_b, a_ln_g, a_ln_b, b_dw_w):
    b, s, n = z.shape
    ch = n // 5
    tile = min(s, 256)
    hb = tile // CONV_HALO
    last_hb = s // CONV_HALO - 1
    kern = functools.partial(_conv_kernel, tile=tile, ch=ch)
    vec = lambda v: v.reshape(1, ch)
    return pl.pallas_call(
        kern,
        grid=(b, s // tile),
        in_specs=[
            pl.BlockSpec((1, tile, n), lambda bi, i: (bi, i, 0)),
            pl.BlockSpec((1, CONV_HALO, n), lambda bi, i: (bi, jnp.maximum(i * hb - 1, 0), 0)),
            pl.BlockSpec((1, CONV_HALO, n), lambda bi, i: (bi, jnp.minimum((i + 1) * hb, last_hb), 0)),
            pl.BlockSpec((8 * CONV_A_WIDTH, ch), lambda bi, i: (0, 0)),
            pl.BlockSpec((1, ch), lambda bi, i: (0, 0)),
            pl.BlockSpec((1, ch), lambda bi, i: (0, 0)),
            pl.BlockSpec((1, ch), lambda bi, i: (0, 0)),
            pl.BlockSpec((CONV_B_WIDTH, ch), lambda bi, i: (0, 0)),
        ],
        out_specs=pl.BlockSpec((1, tile, 2 * ch), lambda bi, i: (bi, i, 0)),
        out_shape=jax.ShapeDtypeStruct((b, s, 2 * ch), BF16),
        scratch_shapes=[pltpu.VMEM((tile + 2 * CONV_HALO, ch), F32),
                        pltpu.VMEM((tile + 2 * CONV_HALO, ch), F32),
                        pltpu.VMEM((7, tile + 2 * CONV_HALO - 8, ch), F32)],
        compiler_params=_cparams("parallel", "parallel"),
        name="conv_mixers",
    )(z, z, z, jnp.repeat(a_dw_w, 8, axis=0), vec(a_dw_b), vec(a_ln_g), vec(a_ln_b), b_dw_w)


def _proj_ln_kernel(*refs, n_mix, n_alias, alpha):
    mix_refs = refs[:n_mix]
    (w_ref, x_ref, g_ref, lng_ref, lnb_ref, sc_ref, sh_ref, wrh_ref, wrl_ref, br_ref) = refs[n_mix:n_mix + 10]
    xo_ref, h_ref, lg_ref = refs[n_mix + 10 + n_alias:]
    if n_mix == 2:
        ut = jnp.concatenate([r[0].reshape(r.shape[1] * r.shape[2], r.shape[3]) for r in mix_refs], axis=0)
        y = lax.dot_general(ut, w_ref[...], (((0,), (0,)), ((), ())), preferred_element_type=F32)
    else:
        y = jnp.dot(mix_refs[0][0], w_ref[...], preferred_element_type=F32)
    xn = _layer_norm(alpha * x_ref[0] + g_ref[0] * y, lng_ref[...], lnb_ref[...])
    xo_ref[0] = xn
    h2 = xn * (1.0 + sc_ref[0]) + sh_ref[0]
    h_ref[...] = h2.astype(BF16)
    h_hi = h2.astype(BF16)
    h_lo = (h2 - h_hi.astype(F32)).astype(BF16)
    lg_ref[...] = (jnp.dot(h_hi, wrh_ref[...], preferred_element_type=F32)
                   + (jnp.dot(h_lo, wrh_ref[...], preferred_element_type=F32)
                      + jnp.dot(h_hi, wrl_ref[...], preferred_element_type=F32))
                   + br_ref[...])


def _proj_ln(mix, w_out, x, g, ln_g, ln_b, sc2, sh2, wr, br, alpha, n_tok, tok_off, bufs=None):
    b, s, d = x.shape
    tm = min(s, 512)
    nt = s // tm
    off = tok_off // tm
    if isinstance(mix, tuple):
        mix_specs = [pl.BlockSpec((1, o.shape[1], HEAD_DIM, tm), lambda bi, i: (bi, 0, 0, i)) for o in mix]
        mix_args = list(mix)
    else:
        mix_specs = [pl.BlockSpec((1, tm, mix.shape[2]), lambda bi, i: (bi, i, 0))]
        mix_args = [mix]
    row = lambda: pl.BlockSpec((1, tm, d), lambda bi, i: (bi, i, 0))
    per_b = lambda: pl.BlockSpec((1, 1, d), lambda bi, i: (bi, 0, 0))
    vec = lambda: pl.BlockSpec((1, d), lambda bi, i: (0, 0))
    alias_args = [] if bufs is None else list(bufs)
    n_in = len(mix_args) + 10
    wr_hi = wr.astype(BF16)
    wr_lo = (wr - wr_hi.astype(F32)).astype(BF16)
    kern = functools.partial(_proj_ln_kernel, n_mix=len(mix_args), n_alias=len(alias_args), alpha=alpha)
    return pl.pallas_call(
        kern,
        grid=(b, nt),
        in_specs=mix_specs + [
            pl.BlockSpec(w_out.shape, lambda bi, i: (0, 0)),
            row(), per_b(), vec(), vec(), per_b(), per_b(),
            pl.BlockSpec((d, ROUTER_COLS), lambda bi, i: (0, 0)),
            pl.BlockSpec((d, ROUTER_COLS), lambda bi, i: (0, 0)),
            pl.BlockSpec((1, ROUTER_COLS), lambda bi, i: (0, 0)),
        ] + [pl.BlockSpec(memory_space=pl.ANY) for _ in alias_args],
        out_specs=[row(),
                   pl.BlockSpec((tm, d), lambda bi, i: (off + bi * nt + i, 0)),
                   pl.BlockSpec((tm, ROUTER_COLS), lambda bi, i: (off + bi * nt + i, 0))],
        out_shape=[jax.ShapeDtypeStruct((b, s, d), F32),
                   jax.ShapeDtypeStruct((n_tok, d), BF16),
                   jax.ShapeDtypeStruct((n_tok, ROUTER_COLS), F32)],
        input_output_aliases={n_in: 1, n_in + 1: 2} if alias_args else {},
        compiler_params=_cparams("parallel", "parallel"),
        name="proj_res_ln",
    )(*mix_args, w_out, x, g, ln_g.reshape(1, d), ln_b.reshape(1, d), sc2, sh2, wr_hi, wr_lo, br, *alias_args)


def _rank_kernel(e_ref, tri_ref, rank_ref, cnt_ref, carry_ref):
    @pl.when(pl.program_id(0) == 0)
    def _():
        carry_ref[...] = jnp.zeros_like(carry_ref)

    e = e_ref[0]
    blk = e.shape[1]
    ids = lax.broadcasted_iota(jnp.int32, (N_EXPERTS, blk), 0)
    oh = jnp.where(ids == e, 1.0, 0.0).astype(F32)
    before = jnp.dot(oh.astype(BF16), tri_ref[...], preferred_element_type=F32)
    carry = carry_ref[...]
    before = before + carry[:, 0:1]
    rank_ref[0] = jnp.sum(oh * before, axis=0, keepdims=True).astype(jnp.int32)
    carry = carry + jnp.sum(oh, axis=1, keepdims=True)
    carry_ref[...] = carry
    cnt_ref[...] = carry.astype(jnp.int32)


def _expert_ranks(flat_e):
    m = flat_e.shape[0]
    blk = RANK_BLOCK
    nblk = m // blk
    tri = (lax.broadcasted_iota(jnp.int32, (blk, blk), 0)
           < lax.broadcasted_iota(jnp.int32, (blk, blk), 1)).astype(BF16)
    rank, cnt = pl.pallas_call(
        _rank_kernel,
        grid=(nblk,),
        in_specs=[pl.BlockSpec((1, 1, blk), lambda i: (i, 0, 0)),
                  pl.BlockSpec((blk, blk), lambda i: (0, 0))],
        out_specs=[pl.BlockSpec((1, 1, blk), lambda i: (i, 0, 0)),
                   pl.BlockSpec((N_EXPERTS, LANES), lambda i: (0, 0))],
        out_shape=[jax.ShapeDtypeStruct((nblk, 1, blk), jnp.int32),
                   jax.ShapeDtypeStruct((N_EXPERTS, LANES), jnp.int32)],
        scratch_shapes=[pltpu.VMEM((N_EXPERTS, LANES), F32)],
        compiler_params=_cparams("arbitrary"),
        name="expert_rank",
    )(flat_e.reshape(nblk, 1, blk), tri)
    return rank.reshape(m), cnt[:, 0]


def _to_row_tiles(ref, x, n_rows, tiles):
    for s in range(tiles):
        ref[pl.ds(s, n_rows, stride=tiles), :] = x[:, s * LANES:(s + 1) * LANES]


def _from_row_tiles(ref, n_rows, tiles):
    return jnp.concatenate([ref[pl.ds(s, n_rows, stride=tiles), :] for s in range(tiles)], axis=1)


def _dispatch_kernel(dest_ref, x_ref, init_ref, xs_ref, buf_ref, sem, *, tm, tiles):
    del init_ref
    i = pl.program_id(0)
    n = pl.num_programs(0)
    slot = i % 2

    def drain(s):
        for _ in range(2):
            pltpu.make_async_copy(buf_ref.at[s], xs_ref.at[pl.ds(0, tm * tiles), :], sem.at[s]).wait()

    @pl.when(i >= 2)
    def _():
        drain(slot)

    _to_row_tiles(buf_ref.at[slot], x_ref[...].astype(F32), tm, tiles)

    def issue(r, c):
        src = buf_ref.at[slot, pl.ds(pl.multiple_of(r * tiles, tiles), tiles), :]
        for k in range(2):
            d = pl.multiple_of(dest_ref[0, 0, 2 * r + k] * tiles, tiles)
            pltpu.make_async_copy(src, xs_ref.at[pl.ds(d, tiles), :], sem.at[slot]).start(priority=k)
        return c

    lax.fori_loop(0, tm, issue, 0, unroll=4)

    @pl.when(i == n - 1)
    def _():
        @pl.when(i >= 1)
        def _():
            drain(1 - slot)

        drain(slot)


def _dispatch(h_all, dest, cap):
    n, d = h_all.shape
    tiles = d // LANES
    tm = 512
    nt = n // tm
    return pl.pallas_call(
        functools.partial(_dispatch_kernel, tm=tm, tiles=tiles),
        grid=(nt,),
        in_specs=[
            pl.BlockSpec((1, 1, 2 * tm), lambda i: (i, 0, 0), memory_space=pltpu.SMEM),
            pl.BlockSpec((tm, d), lambda i: (i, 0)),
            pl.BlockSpec(memory_space=pl.ANY),
        ],
        out_specs=pl.BlockSpec(memory_space=pl.ANY),
        out_shape=jax.ShapeDtypeStruct((cap * tiles, LANES), F32),
        scratch_shapes=[pltpu.VMEM((2, tm * tiles, LANES), F32), pltpu.SemaphoreType.DMA((2,))],
        input_output_aliases={2: 0},
        compiler_params=_cparams("arbitrary"),
        name="moe_dispatch",
    )(dest.reshape(nt, 1, 2 * tm), h_all, jnp.zeros((cap * tiles, LANES), F32))


def _experts_kernel(be_ref, nu_ref, x_ref, wg_ref, wu_ref, wd_ref, o_ref, wgb_ref, wub_ref, wdb_ref,
                    *, tb, tiles):
    i = pl.program_id(0)
    changed = jnp.logical_or(i == 0, be_ref[i] != be_ref[jnp.maximum(i - 1, 0)])

    @pl.when(changed)
    def _():
        wgb_ref[...] = wg_ref[0, 0].astype(BF16)
        wub_ref[...] = wu_ref[0, 0].astype(BF16)
        wdb_ref[...] = wd_ref[0, 0].astype(BF16)

    @pl.when(i < nu_ref[0])
    def _():
        x = _from_row_tiles(x_ref, tb, tiles).astype(BF16)
        g = jnp.dot(x, wgb_ref[...], preferred_element_type=F32)
        u = jnp.dot(x, wub_ref[...], preferred_element_type=F32)
        a = (g * jax.nn.sigmoid(g) * u).astype(BF16)
        _to_row_tiles(o_ref, jnp.dot(a, wdb_ref[...], preferred_element_type=F32), tb, tiles)

    @pl.when(i >= nu_ref[0])
    def _():
        o_ref[...] = jnp.zeros_like(o_ref)


def _experts(xs, block_e, n_used, w_gate, w_up, w_down, layer):
    d, hid = w_gate.shape[2], w_gate.shape[3]
    tiles = d // LANES
    tb = EXPERT_BLOCK
    nb = xs.shape[0] // (tb * tiles)
    grid_spec = pltpu.PrefetchScalarGridSpec(
        num_scalar_prefetch=2,
        grid=(nb,),
        in_specs=[
            pl.BlockSpec((tb * tiles, LANES), lambda i, be, nu: (i, 0)),
            pl.BlockSpec((1, 1, d, hid), lambda i, be, nu: (layer, be[i], 0, 0)),
            pl.BlockSpec((1, 1, d, hid), lambda i, be, nu: (layer, be[i], 0, 0)),
            pl.BlockSpec((1, 1, hid, d), lambda i, be, nu: (layer, be[i], 0, 0)),
        ],
        out_specs=pl.BlockSpec((tb * tiles, LANES), lambda i, be, nu: (i, 0)),
        scratch_shapes=[pltpu.VMEM((d, hid), BF16), pltpu.VMEM((d, hid), BF16), pltpu.VMEM((hid, d), BF16)],
    )
    return pl.pallas_call(
        functools.partial(_experts_kernel, tb=tb, tiles=tiles),
        grid_spec=grid_spec,
        out_shape=jax.ShapeDtypeStruct(xs.shape, F32),
        compiler_params=_cparams("arbitrary"),
        name="experts",
    )(block_e, n_used, xs, w_gate, w_up, w_down)


def _moe_ln_kernel(dest_ref, dest_next_ref, x_ref, gt_ref, g_ref, lng_ref, lnb_ref, ys_ref, o_ref,
                   y0_ref, y1_ref, sem0, sem1, *, tm, tiles, alpha):
    i = pl.program_id(0)
    n = pl.num_programs(0)
    slot = i % 2

    def start_row(idx_ref, slot, r):
        dst = pl.ds(pl.multiple_of(r * tiles, tiles), tiles)
        d0 = pl.multiple_of(idx_ref[0, 0, 2 * r] * tiles, tiles)
        d1 = pl.multiple_of(idx_ref[0, 0, 2 * r + 1] * tiles, tiles)
        pltpu.make_async_copy(ys_ref.at[pl.ds(d0, tiles), :], y0_ref.at[slot, dst, :],
                              sem0.at[slot]).start(priority=0)
        pltpu.make_async_copy(ys_ref.at[pl.ds(d1, tiles), :], y1_ref.at[slot, dst, :],
                              sem1.at[slot]).start(priority=1)

    def wait_slot(s):
        pltpu.make_async_copy(ys_ref.at[pl.ds(0, tm * tiles), :], y0_ref.at[s], sem0.at[s]).wait()
        pltpu.make_async_copy(ys_ref.at[pl.ds(0, tm * tiles), :], y1_ref.at[s], sem1.at[s]).wait()

    @pl.when(i == 0)
    def _():
        def issue(r, c):
            start_row(dest_ref, 0, r)
            return c

        lax.fori_loop(0, tm, issue, 0, unroll=4)

    wait_slot(slot)
    y0 = _from_row_tiles(y0_ref.at[slot], tm, tiles)
    y1 = _from_row_tiles(y1_ref.at[slot], tm, tiles)
    for r in range(tm):
        start_row(dest_next_ref, 1 - slot, r)
    gt = gt_ref[...]
    y = gt[:, 0:1] * y0 + gt[:, 1:2] * y1
    o_ref[...] = _layer_norm(alpha * x_ref[...] + g_ref[0] * y, lng_ref[...], lnb_ref[...])

    @pl.when(i == n - 1)
    def _():
        wait_slot(1 - slot)


def _moe_ln(x, ys, dest, gates, tok_off, g, ln_g, ln_b, alpha):
    b, s, d = x.shape
    tiles = d // LANES
    n_tok = gates.shape[0]
    tm = min(s, 512)
    nt = s // tm
    off = tok_off // tm
    n_steps = b * nt
    vec = lambda: pl.BlockSpec((1, d), lambda i: (0, 0))
    dest_blocks = dest.reshape(n_tok // tm, 1, 2 * tm)
    out = pl.pallas_call(
        functools.partial(_moe_ln_kernel, tm=tm, tiles=tiles, alpha=alpha),
        grid=(n_steps,),
        in_specs=[
            pl.BlockSpec((1, 1, 2 * tm), lambda i: (off + i, 0, 0), memory_space=pltpu.SMEM),
            pl.BlockSpec((1, 1, 2 * tm), lambda i: (off + jnp.minimum(i + 1, n_steps - 1), 0, 0),
                         memory_space=pltpu.SMEM),
            pl.BlockSpec((tm, d), lambda i: (i, 0)),
            pl.BlockSpec((tm, 2), lambda i: (off + i, 0)),
            pl.BlockSpec((1, 1, d), lambda i: (i // nt, 0, 0)),
            vec(), vec(),
            pl.BlockSpec(memory_space=pl.ANY),
        ],
        out_specs=pl.BlockSpec((tm, d), lambda i: (i, 0)),
        out_shape=jax.ShapeDtypeStruct((b * s, d), F32),
        scratch_shapes=[pltpu.VMEM((2, tm * tiles, LANES), F32), pltpu.VMEM((2, tm * tiles, LANES), F32),
                        pltpu.SemaphoreType.DMA((2,)), pltpu.SemaphoreType.DMA((2,))],
        compiler_params=_cparams("arbitrary"),
        name="moe_res_ln",
    )(dest_blocks, dest_blocks, x.reshape(b * s, d), gates, g, ln_g.reshape(1, d), ln_b.reshape(1, d), ys)
    return out.reshape(b, s, d)


def _route(logits):
    n = logits.shape[0]
    l1 = logits[:, :N_GROUPS]
    grp = jnp.argmax(l1, axis=-1).astype(jnp.int32)
    p_grp = 1.0 / jnp.sum(jnp.exp(l1 - jnp.max(l1, axis=-1, keepdims=True)), axis=-1)
    lg2 = logits[:, N_GROUPS:N_GROUPS + N_EXPERTS].reshape(n, N_GROUPS, EXPERTS_PER_GROUP)
    gsel = grp[:, None] == jnp.arange(N_GROUPS, dtype=jnp.int32)[None, :]
    sel = jnp.sum(jnp.where(gsel[:, :, None], lg2, 0.0), axis=1)
    i1 = jnp.argmax(sel, axis=-1).astype(jnp.int32)
    v1 = jnp.max(sel, axis=-1)
    col = jnp.arange(EXPERTS_PER_GROUP, dtype=jnp.int32)[None, :]
    sel2 = jnp.where(col == i1[:, None], -jnp.inf, sel)
    i2 = jnp.argmax(sel2, axis=-1).astype(jnp.int32)
    v2 = jnp.max(sel2, axis=-1)
    e2 = jnp.exp(v2 - v1)
    w1 = 1.0 / (1.0 + e2)
    gates = p_grp[:, None] * jnp.stack([w1, e2 * w1], axis=-1)
    experts = grp[:, None] * EXPERTS_PER_GROUP + jnp.stack([i1, i2], axis=-1)
    return experts, gates.astype(F32)


def _moe(h_all, logits, w_gate, w_up, w_down, layer):
    n, d = h_all.shape
    m = 2 * n
    tb = EXPERT_BLOCK
    experts, gates = _route(logits)
    flat_e = experts.reshape(m)
    rank, counts = _expert_ranks(flat_e)
    padded = (counts + tb - 1) // tb * tb
    pad_end = jnp.cumsum(padded)
    pad_start = pad_end - padded
    onehot = flat_e[:, None] == jnp.arange(N_EXPERTS, dtype=jnp.int32)[None, :]
    dest = jnp.sum(jnp.where(onehot, pad_start[None, :], 0), axis=-1).astype(jnp.int32) + rank
    nb = -(-m // tb) + N_EXPERTS
    cap = nb * tb
    blk_start = jnp.arange(nb, dtype=jnp.int32) * tb
    block_e = jnp.minimum(jnp.sum(pad_end[None, :] <= blk_start[:, None], axis=-1), N_EXPERTS - 1)
    n_used = (pad_end[-1] // tb).reshape(1)

    xs = _dispatch(h_all, dest, cap)
    ys = _experts(xs, block_e.astype(jnp.int32), n_used.astype(jnp.int32), w_gate, w_up, w_down, layer)
    return ys, dest, gates


def _prep_kernel(z_ref, cos_ref, sin_ref, gq_ref, gk_ref, grp_ref,
                 qt_ref, kc_ref, vt_ref, qtd_ref, kd_ref, vtd_ref, *, hq, hkv, hd, ts, dchunk):
    cos = cos_ref[...]
    sin = sin_ref[...]
    lane = lax.broadcasted_iota(jnp.int32, (ts, LANES), 1)
    first_half = (lane % 32) < 16

    def norm_rope(x, g):
        ss = jnp.dot(x * x, grp_ref[...], preferred_element_type=F32, precision=HIGHEST)
        xn = x * lax.rsqrt(ss * (1.0 / HEAD_DIM) + RMS_EPS) * g
        partner = jnp.where(first_half, pltpu.roll(xn, LANES - 16, 1), pltpu.roll(xn, 16, 1))
        return xn * cos + partner * sin

    col = 0
    for s in range(hq // 2):
        x = norm_rope(z_ref[0, :, col:col + LANES], gq_ref[...]) * (ATTN_SCALE * LOG2E)
        t = x.T.astype(BF16)
        qt_ref[0, 2 * s] = t[0:HEAD_DIM]
        qt_ref[0, 2 * s + 1] = t[HEAD_DIM:]
        col += LANES
    for s in range(hkv // 2):
        x = norm_rope(z_ref[0, :, col:col + LANES], gk_ref[...]).astype(BF16)
        kc_ref[0, 2 * s, 0] = x[:, 0:HEAD_DIM]
        kc_ref[0, 2 * s + 1, 0] = x[:, HEAD_DIM:]
        col += LANES
    tail = (lax.broadcasted_iota(jnp.int32, (V_ROWS - HEAD_DIM, ts), 0) == 0).astype(BF16)
    tail_d = (lax.broadcasted_iota(jnp.int32, (V_ROWS - HEAD_DIM, dchunk), 0) == 0).astype(BF16)
    for s in range(hkv // 2):
        t = z_ref[0, :, col:col + LANES].T.astype(BF16)
        for hh in range(2):
            vt_ref[0, 2 * s + hh, 0, 0:HEAD_DIM, :] = t[hh * HEAD_DIM:(hh + 1) * HEAD_DIM]
            vt_ref[0, 2 * s + hh, 0, HEAD_DIM:V_ROWS, :] = tail
        col += LANES
    for s in range(hd // 2):
        t = (z_ref[0, :, col:col + LANES] * (ATTN_SCALE * LOG2E)).T.astype(BF16)
        qtd_ref[0, 2 * s] = t[0:HEAD_DIM]
        qtd_ref[0, 2 * s + 1] = t[HEAD_DIM:]
        col += LANES
    for s in range(hd // 2):
        x = z_ref[0, :, col:col + LANES].astype(BF16)
        for c in range(ts // dchunk):
            kd_ref[0, 2 * s, c] = x[c * dchunk:(c + 1) * dchunk, 0:HEAD_DIM]
            kd_ref[0, 2 * s + 1, c] = x[c * dchunk:(c + 1) * dchunk, HEAD_DIM:]
        col += LANES
    for s in range(hd // 2):
        t = z_ref[0, :, col:col + LANES].T.astype(BF16)
        for c in range(ts // dchunk):
            for hh in range(2):
                vtd_ref[0, 2 * s + hh, c, 0:HEAD_DIM, :] = t[hh * HEAD_DIM:(hh + 1) * HEAD_DIM,
                                                             c * dchunk:(c + 1) * dchunk]
                vtd_ref[0, 2 * s + hh, c, HEAD_DIM:V_ROWS, :] = tail_d
        col += LANES


def _attn_prep(z, cos, sin, q_g, k_g, hq, hkv, hd):
    b, s, n = z.shape
    ts = min(s, 512)
    dchunk = 256
    nt = s // ts
    hdim = HEAD_DIM
    gq = jnp.tile(q_g, 2).reshape(1, LANES)
    gk = jnp.tile(k_g, 2).reshape(1, LANES)
    li = jnp.arange(LANES)
    grp = (li[:, None] // hdim == li[None, :] // hdim).astype(F32)
    kern = functools.partial(_prep_kernel, hq=hq, hkv=hkv, hd=hd, ts=ts, dchunk=dchunk)
    const = lambda shape: pl.BlockSpec(shape, lambda bi, i: (0,) * len(shape))
    return pl.pallas_call(
        kern,
        grid=(b, nt),
        in_specs=[
            pl.BlockSpec((1, ts, n), lambda bi, i: (bi, i, 0)),
            pl.BlockSpec((ts, LANES), lambda bi, i: (i, 0)),
            pl.BlockSpec((ts, LANES), lambda bi, i: (i, 0)),
            const((1, LANES)), const((1, LANES)), const((LANES, LANES)),
        ],
        out_specs=[
            pl.BlockSpec((1, hq, hdim, ts), lambda bi, i: (bi, 0, 0, i)),
            pl.BlockSpec((1, hkv, 1, ts, hdim), lambda bi, i: (bi, 0, i, 0, 0)),
            pl.BlockSpec((1, hkv, 1, V_ROWS, ts), lambda bi, i: (bi, 0, i, 0, 0)),
            pl.BlockSpec((1, hd, hdim, ts), lambda bi, i: (bi, 0, 0, i)),
            pl.BlockSpec((1, hd, ts // dchunk, dchunk, hdim), lambda bi, i: (bi, 0, i, 0, 0)),
            pl.BlockSpec((1, hd, ts // dchunk, V_ROWS, dchunk), lambda bi, i: (bi, 0, i, 0, 0)),
        ],
        out_shape=[
            jax.ShapeDtypeStruct((b, hq, hdim, s), BF16),
            jax.ShapeDtypeStruct((b, hkv, nt, ts, hdim), BF16),
            jax.ShapeDtypeStruct((b, hkv, nt, V_ROWS, ts), BF16),
            jax.ShapeDtypeStruct((b, hd, hdim, s), BF16),
            jax.ShapeDtypeStruct((b, hd, s // dchunk, dchunk, hdim), BF16),
            jax.ShapeDtypeStruct((b, hd, s // dchunk, V_ROWS, dchunk), BF16),
        ],
        compiler_params=_cparams("parallel", "parallel"),
        name="attn_prep",
    )(z, cos, sin, gq, gk, grp)


def _rope_tables(s):
    t = jnp.arange(s)
    row = (t // GRID_W).astype(F32)
    colp = (t % GRID_W).astype(F32)
    axis_dim = HEAD_DIM // 2
    inv = ROPE_THETA ** (-jnp.arange(0, axis_dim, 2, dtype=F32) / axis_dim)
    ar = row[:, None] * inv
    ac = colp[:, None] * inv
    cr, sr, cc, sc = jnp.cos(ar), jnp.sin(ar), jnp.cos(ac), jnp.sin(ac)
    cos = jnp.concatenate([cr, cr, cc, cc], axis=-1)
    sin = jnp.concatenate([-sr, sr, -sc, sc], axis=-1)
    return jnp.tile(cos, (1, 2)), jnp.tile(sin, (1, 2))


def _gqa_kernel(qt_ref, k_ref, vt_ref, kc_ref, vtc_ref, o_ref, acc_ref, s_ref, sc_ref, *, group):
    n_chunks = k_ref.shape[2]
    tq = qt_ref.shape[3]

    def q_all():
        return jnp.concatenate([qt_ref[0, g] for g in range(group)], axis=1)

    def scores(kb, slot):
        st = jnp.dot(kb, q_all(), preferred_element_type=F32)
        s_ref[slot] = st
        return jnp.max(st, axis=0, keepdims=True)

    def absorb(slot, vb, cmax, m):
        m_new = jnp.maximum(m, cmax)
        alpha = jnp.exp2(m - m_new)
        p = jnp.exp2(s_ref[slot] - m_new)
        acc_ref[...] = alpha * acc_ref[...] + jnp.dot(vb, p.astype(BF16), preferred_element_type=F32)
        return m_new

    per_trip = 16 if n_chunks % 16 == 0 else 2
    st = jnp.dot(kc_ref[0, 0], q_all(), preferred_element_type=F32)
    sc_ref[...] = st
    m = jnp.max(st, axis=0, keepdims=True)
    cm = scores(k_ref[0, 0, 0], 0)
    p = jnp.exp2(sc_ref[...] - m)
    acc_ref[...] = jnp.dot(vtc_ref[0, 0], p.astype(BF16), preferred_element_type=F32)

    def body(i, carry):
        m, cm_cur = carry
        j0 = per_trip * i
        for u in range(per_trip):
            cm_next = scores(k_ref[0, 0, jnp.minimum(j0 + u + 1, n_chunks - 1)], (u + 1) % 2)
            m = absorb(u % 2, vt_ref[0, 0, j0 + u], cm_cur, m)
            cm_cur = cm_next
        return m, cm_cur

    lax.fori_loop(0, n_chunks // per_trip, body, (m, cm))
    out = acc_ref[0:HEAD_DIM, :] / acc_ref[HEAD_DIM:HEAD_DIM + 1, :]
    for g in range(group):
        o_ref[0, g] = out[:, g * tq:(g + 1) * tq].astype(BF16)


def _gqa(qt, k, vt, k_ctx, vt_ctx):
    b, hq, hdim, s = qt.shape
    hkv, n_chunks, tk = k.shape[1], k.shape[2], k.shape[3]
    n_ctx = k_ctx.shape[2]
    group = hq // hkv
    tq = min(s, 256)
    return pl.pallas_call(
        functools.partial(_gqa_kernel, group=group),
        grid=(b, hkv, s // tq),
        in_specs=[
            pl.BlockSpec((1, group, hdim, tq), lambda bi, h, i: (bi, h, 0, i)),
            pl.BlockSpec((1, 1, n_chunks, tk, hdim), lambda bi, h, i: (bi, h, 0, 0, 0)),
            pl.BlockSpec((1, 1, n_chunks, V_ROWS, tk), lambda bi, h, i: (bi, h, 0, 0, 0)),
            pl.BlockSpec((1, 1, n_ctx, hdim), lambda bi, h, i: (bi, h, 0, 0)),
            pl.BlockSpec((1, 1, V_ROWS, n_ctx), lambda bi, h, i: (bi, h, 0, 0)),
        ],
        out_specs=pl.BlockSpec((1, group, hdim, tq), lambda bi, h, i: (bi, h, 0, i)),
        out_shape=jax.ShapeDtypeStruct((b, hq, hdim, s), BF16),
        scratch_shapes=[pltpu.VMEM((V_ROWS, group * tq), F32), pltpu.VMEM((2, tk, group * tq), F32),
                        pltpu.VMEM((n_ctx, group * tq), F32)],
        compiler_params=_cparams("parallel", "parallel", "parallel"),
        name="gqa_attention",
    )(qt, k, vt, k_ctx, vt_ctx)


def _ctx_gqa_kernel(qt_ref, k_ref, vt_ref, o_ref):
    st = jnp.dot(k_ref[0, 0], qt_ref[0, 0], preferred_element_type=F32)
    p = jnp.exp2(st - jnp.max(st, axis=0, keepdims=True))
    acc = jnp.dot(vt_ref[0, 0], p.astype(BF16), preferred_element_type=F32)
    o_ref[0, 0] = (acc[0:HEAD_DIM] / acc[HEAD_DIM:HEAD_DIM + 1]).astype(BF16)


def _ctx_gqa(qt, k, vt):
    b, hq, hdim, n = qt.shape
    group = hq // k.shape[1]
    return pl.pallas_call(
        _ctx_gqa_kernel,
        grid=(b, hq),
        in_specs=[
            pl.BlockSpec((1, 1, hdim, n), lambda bi, h: (bi, h, 0, 0)),
            pl.BlockSpec((1, 1, n, hdim), lambda bi, h: (bi, h // group, 0, 0)),
            pl.BlockSpec((1, 1, V_ROWS, n), lambda bi, h: (bi, h // group, 0, 0)),
        ],
        out_specs=pl.BlockSpec((1, 1, hdim, n), lambda bi, h: (bi, h, 0, 0)),
        out_shape=jax.ShapeDtypeStruct((b, hq, hdim, n), BF16),
        compiler_params=_cparams("parallel", "parallel"),
        name="ctx_gqa",
    )(qt, k, vt)


def _na_kernel(qt_ref, k_ref, vt_ref, kc_ref, vtc_ref, bias_ref, o_ref, s_ref, *, n_band, nq, per_step):
    step = pl.program_id(2)
    tq = NA_Q_ROWS * GRID_W
    cw = k_ref.shape[3]
    n_ctx = kc_ref.shape[2]

    def scores(j, slot):
        i = step * per_step + j
        c0 = jnp.clip(i - 1, 0, nq - n_band)
        variant = jnp.where(i == 0, 0, jnp.where(i == nq - 1, 2, 1))
        qt = qt_ref[0, 0, :, j * tq:(j + 1) * tq]
        st = jnp.dot(kc_ref[0, 0], qt, preferred_element_type=F32)
        s_ref[slot, 0:n_ctx, :] = st
        m = jnp.max(st, axis=0, keepdims=True)
        for c in range(n_band):
            b0 = pl.multiple_of((variant * n_band + c) * cw, cw)
            st = (jnp.dot(k_ref[0, 0, c0 + c], qt, preferred_element_type=F32)
                  + bias_ref[0, pl.ds(b0, cw), :])
            s_ref[slot, n_ctx + c * cw:n_ctx + (c + 1) * cw, :] = st
            m = jnp.maximum(m, jnp.max(st, axis=0, keepdims=True))
        return m, c0

    def finish(j, slot, m, c0):
        p = jnp.exp2(s_ref[slot, 0:n_ctx, :] - m).astype(BF16)
        acc = jnp.dot(vtc_ref[0, 0], p, preferred_element_type=F32)
        for c in range(n_band):
            p = jnp.exp2(s_ref[slot, n_ctx + c * cw:n_ctx + (c + 1) * cw, :] - m).astype(BF16)
            acc = acc + jnp.dot(vt_ref[0, 0, c0 + c], p, preferred_element_type=F32)
        o_ref[0, 0, :, j * tq:(j + 1) * tq] = (acc[0:HEAD_DIM] / acc[HEAD_DIM:HEAD_DIM + 1]).astype(BF16)

    nxt = scores(0, 0)
    for j in range(per_step):
        cur = nxt
        if j + 1 < per_step:
            nxt = scores(j + 1, (j + 1) % 2)
        finish(j, j % 2, *cur)


def _na_bias_table(rel_bias, rows):
    w = GRID_W
    h, n_rel_rows, n_rel_cols = rel_bias.shape
    qcol = np.arange(w)
    cstart = np.clip(qcol - NA_WIN_COLS // 2, 0, w - NA_WIN_COLS)
    in_win = (qcol[None, :] >= cstart[:, None]) & (qcol[None, :] < cstart[:, None] + NA_WIN_COLS)
    col_idx = np.clip(qcol[None, :] - qcol[:, None] + NA_WIN_COLS - 1, 0, n_rel_cols - 1)
    onehot = (col_idx.reshape(-1)[None, :] == np.arange(n_rel_cols)[:, None]).astype(np.float32)
    cols = jnp.einsum('hrj,jn->hrn', rel_bias, onehot, precision=HIGHEST).reshape(h, n_rel_rows, w, w)
    cols = jnp.where(in_win[None, None], cols, NEG_BIG)
    ext = jnp.concatenate([cols, jnp.full((h, 1, w, w), NEG_BIG, F32)], axis=1)
    a = np.arange(NA_Q_ROWS)[:, None]
    c = np.arange(NA_BAND_ROWS)[None, :]
    tables = []
    for r0, rs0 in ((0, 0), (NA_Q_ROWS, 0), (rows - NA_Q_ROWS, rows - NA_BAND_ROWS)):
        r = r0 + a
        kr = rs0 + c
        rs = np.clip(r - NA_WIN_ROWS // 2, 0, rows - NA_WIN_ROWS)
        valid = (kr >= rs) & (kr < rs + NA_WIN_ROWS)
        row_idx = np.where(valid, kr - r + NA_WIN_ROWS - 1, n_rel_rows)
        bias = ext[:, row_idx.reshape(-1)].reshape(h, NA_Q_ROWS, NA_BAND_ROWS, w, w)
        tables.append(bias.transpose(0, 2, 4, 1, 3).reshape(h, NA_BAND_ROWS * w, NA_Q_ROWS * w))
    return (jnp.stack(tables, axis=1) * LOG2E).astype(F32)


def _na(qt, k, vt, k_ctx, vt_ctx, bias):
    b, h, hdim, s = qt.shape
    n_chunks, cw = k.shape[2], k.shape[3]
    n_ctx = k_ctx.shape[2]
    tq = NA_Q_ROWS * GRID_W
    nq = s // tq
    n_band = NA_BAND_ROWS * GRID_W // cw
    band_w = NA_BAND_ROWS * GRID_W
    per_step = math.gcd(nq, NA_BLOCKS_PER_STEP)
    return pl.pallas_call(
        functools.partial(_na_kernel, n_band=n_band, nq=nq, per_step=per_step),
        grid=(b, h, nq // per_step),
        in_specs=[
            pl.BlockSpec((1, 1, hdim, per_step * tq), lambda bi, hi, i: (bi, hi, 0, i)),
            pl.BlockSpec((1, 1, n_chunks, cw, hdim), lambda bi, hi, i: (bi, hi, 0, 0, 0)),
            pl.BlockSpec((1, 1, n_chunks, V_ROWS, cw), lambda bi, hi, i: (bi, hi, 0, 0, 0)),
            pl.BlockSpec((1, 1, n_ctx, hdim), lambda bi, hi, i: (bi, hi, 0, 0)),
            pl.BlockSpec((1, 1, V_ROWS, n_ctx), lambda bi, hi, i: (bi, hi, 0, 0)),
            pl.BlockSpec((1, 3 * band_w, tq), lambda bi, hi, i: (hi, 0, 0)),
        ],
        out_specs=pl.BlockSpec((1, 1, hdim, per_step * tq), lambda bi, hi, i: (bi, hi, 0, i)),
        out_shape=jax.ShapeDtypeStruct((b, h, hdim, s), BF16),
        scratch_shapes=[pltpu.VMEM((2, n_ctx + band_w, tq), F32)],
        compiler_params=_cparams("parallel", "parallel", "parallel"),
        name="na_attention",
    )(qt, k, vt, k_ctx, vt_ctx, bias.reshape(h, 3 * band_w, tq))


def kernel(x, c, ctx, c_ctx, ada_w, ada_b, ln1_g, ln1_b, ln2_g, ln2_b, conv_w_in, conv_w_out, conv_a_dw_w, conv_a_dw_b, conv_a_ln_g, conv_a_ln_b, conv_b_dw_w, attn_w_in, attn_w_out, q_norm_g, k_norm_g, na_rel_bias, router_w1, router_b1, router_w2, router_b2, expert_w_gate, expert_w_up, expert_w_down):
    bsz, seq, d = x.shape
    n_ctx = ctx.shape[1]
    depth = ada_w.shape[0]
    alpha = (2 * depth) ** 0.25
    hq = (d // 2) // HEAD_DIM
    hkv = hq // 4
    hd = (d // 2) // HEAD_DIM

    cond8 = jnp.zeros((8, d), F32).at[:bsz].set(c).at[bsz].set(c_ctx)
    mods = _ada(cond8, ada_w, ada_b)

    def lat_mod(i, k):
        return mods[i, :bsz, k * d:(k + 1) * d].reshape(bsz, 1, d)

    def ctx_mod(i, k):
        return jnp.broadcast_to(mods[i, bsz, k * d:(k + 1) * d].reshape(1, 1, d), (bsz, 1, d))

    wr = jnp.zeros((depth, d, ROUTER_COLS), F32)
    wr = wr.at[:, :, :N_GROUPS].set(router_w1).at[:, :, N_GROUPS:N_GROUPS + N_EXPERTS].set(router_w2)
    br = jnp.zeros((depth, 1, ROUTER_COLS), F32)
    br = br.at[:, 0, :N_GROUPS].set(router_b1).at[:, 0, N_GROUPS:N_GROUPS + N_EXPERTS].set(router_b2)

    cos_lat, sin_lat = _rope_tables(seq)
    cos_ctx = jnp.ones((n_ctx, LANES), F32)
    sin_ctx = jnp.zeros((n_ctx, LANES), F32)

    x_lat, x_ctx = x, ctx
    for i in range(depth):
        last = i == depth - 1
        odd = i % 2 == 1
        j = i // 2
        use_ctx = odd or not last
        if odd:
            w_in = attn_w_in[j].astype(BF16)
            w_out = attn_w_out[j].astype(BF16)
            z_lat = _mod_matmul(x_lat, lat_mod(i, 1), lat_mod(i, 0), w_in)
            z_ctx = _mod_matmul(x_ctx, ctx_mod(i, 1), ctx_mod(i, 0), w_in)
            qt, kc, vtc, qtd, kd, vtd = _attn_prep(z_lat, cos_lat, sin_lat, q_norm_g[j], k_norm_g[j], hq, hkv, hd)
            cqt, ckc, cvtc, cqtd, ckd, cvtd = _attn_prep(z_ctx, cos_ctx, sin_ctx, q_norm_g[j], k_norm_g[j],
                                                         hq, hkv, hd)
            ckc = ckc.reshape(bsz, hkv, n_ctx, HEAD_DIM)
            cvtc = cvtc.reshape(bsz, hkv, V_ROWS, n_ctx)
            ckd = ckd.reshape(bsz, hd, n_ctx, HEAD_DIM)
            cvtd = cvtd.reshape(bsz, hd, V_ROWS, n_ctx)
            ot_c = _gqa(qt, kc, vtc, ckc, cvtc)
            bias = _na_bias_table(na_rel_bias[j], seq // GRID_W)
            ot_d = _na(qtd, kd, vtd, ckd, cvtd, bias)
            mix_lat = (ot_c, ot_d)
            if not last:
                mix_ctx = (_ctx_gqa(cqt, ckc, cvtc), _ctx_gqa(cqtd, ckd, cvtd))
        else:
            w_in = conv_w_in[j].astype(BF16)
            w_out = conv_w_out[j].astype(BF16)
            conv_w = (conv_a_dw_w[j], conv_a_dw_b[j], conv_a_ln_g[j], conv_a_ln_b[j], conv_b_dw_w[j])
            z_lat = _mod_matmul(x_lat, lat_mod(i, 1), lat_mod(i, 0), w_in)
            mix_lat = _conv_mixers(z_lat, *conv_w)
            if not last:
                z_ctx = _mod_matmul(x_ctx, ctx_mod(i, 1), ctx_mod(i, 0), w_in)
                mix_ctx = _conv_mixers(z_ctx, *conv_w)

        n_lat = bsz * seq
        n_tok = n_lat if last else n_lat + bsz * n_ctx
        bufs = None if last else (jnp.zeros((n_tok, d), BF16), jnp.zeros((n_tok, ROUTER_COLS), F32))
        x_lat, h_all, lg_all = _proj_ln(mix_lat, w_out, x_lat, lat_mod(i, 2), ln1_g[i], ln1_b[i],
                                        lat_mod(i, 4), lat_mod(i, 3), wr[i], br[i], alpha, n_tok, 0, bufs=bufs)
        moe_w = (expert_w_gate, expert_w_up, expert_w_down, i)
        if last:
            ys, dest, gates = _moe(h_all, lg_all, *moe_w)
            x_lat = _moe_ln(x_lat, ys, dest, gates, 0, lat_mod(i, 5), ln2_g[i], ln2_b[i], alpha)
        else:
            x_ctx, h_all, lg_all = _proj_ln(mix_ctx, w_out, x_ctx, ctx_mod(i, 2), ln1_g[i], ln1_b[i],
                                            ctx_mod(i, 4), ctx_mod(i, 3), wr[i], br[i], alpha, n_tok, n_lat,
                                            bufs=(h_all, lg_all))
            ys, dest, gates = _moe(h_all, lg_all, *moe_w)
            x_lat = _moe_ln(x_lat, ys, dest, gates, 0, lat_mod(i, 5), ln2_g[i], ln2_b[i], alpha)
            x_ctx = _moe_ln(x_ctx, ys, dest, gates, n_lat, ctx_mod(i, 5), ln2_g[i], ln2_b[i], alpha)
    return x_lat
```

```python
import functools
import math

import jax
import jax.numpy as jnp
import numpy as np
from jax import lax
from jax.experimental import pallas as pl
from jax.experimental.pallas import tpu as pltpu

F32 = jnp.float32
BF16 = jnp.bfloat16
HIGHEST = lax.Precision.HIGHEST

GRID_W = 64
HEAD_DIM = 64
V_ROWS = HEAD_DIM + 16
ATTN_SCALE = HEAD_DIM ** -0.5
LOG2E = math.log2(math.e)
CONV_A_WIDTH = 31
CONV_B_WIDTH = 3
CONV_HALO = 16
NA_WIN_ROWS = 8
NA_WIN_COLS = 16
NA_Q_ROWS = 4
NA_BAND_ROWS = 12
NA_BLOCKS_PER_STEP = 8
ROPE_THETA = 10000.0
N_GROUPS = 4
EXPERTS_PER_GROUP = 8
N_EXPERTS = N_GROUPS * EXPERTS_PER_GROUP
LN_EPS = 1e-5
RMS_EPS = 1e-6
NEG_BIG = -1e30

VMEM_LIMIT_BYTES = 48 * 1024 * 1024
LANES = 128
ROUTER_COLS = LANES
EXPERT_BLOCK = 256
RANK_BLOCK = 512


def _cparams(*sem):
    return pltpu.CompilerParams(dimension_semantics=sem, vmem_limit_bytes=VMEM_LIMIT_BYTES)


def _layer_norm(x, g, b):
    mu = jnp.mean(x, axis=-1, keepdims=True)
    xc = x - mu
    var = jnp.mean(xc * xc, axis=-1, keepdims=True)
    return xc * lax.rsqrt(var + LN_EPS) * g + b


def _ada_kernel(c_ref, w_ref, b_ref, o_ref):
    c = c_ref[...]
    s = c * jax.nn.sigmoid(c)
    o_ref[0] = jnp.dot(s, w_ref[0], preferred_element_type=F32, precision=HIGHEST) + b_ref[0]


def _ada(cond8, ada_w, ada_b):
    n_layers, d, n = ada_w.shape
    tn = 1536
    return pl.pallas_call(
        _ada_kernel,
        grid=(n_layers, n // tn),
        in_specs=[
            pl.BlockSpec((8, d), lambda l, j: (0, 0)),
            pl.BlockSpec((1, d, tn), lambda l, j: (l, 0, j)),
            pl.BlockSpec((1, 1, tn), lambda l, j: (l, 0, j)),
        ],
        out_specs=pl.BlockSpec((1, 8, tn), lambda l, j: (l, 0, j)),
        out_shape=jax.ShapeDtypeStruct((n_layers, 8, n), F32),
        compiler_params=_cparams("parallel", "parallel"),
        name="ada_mod",
    )(cond8, ada_w, ada_b.reshape(n_layers, 1, n))


def _mod_matmul_kernel(x_ref, sc_ref, sh_ref, w_ref, o_ref):
    h = x_ref[0] * (1.0 + sc_ref[0]) + sh_ref[0]
    o_ref[0] = jnp.dot(h.astype(BF16), w_ref[...], preferred_element_type=F32)


def _mod_matmul(x, sc, sh, w):
    b, s, d = x.shape
    n = w.shape[1]
    tm = min(s, 512)
    return pl.pallas_call(
        _mod_matmul_kernel,
        grid=(b, s // tm),
        in_specs=[
            pl.BlockSpec((1, tm, d), lambda bi, i: (bi, i, 0)),
            pl.BlockSpec((1, 1, d), lambda bi, i: (bi, 0, 0)),
            pl.BlockSpec((1, 1, d), lambda bi, i: (bi, 0, 0)),
            pl.BlockSpec((d, n), lambda bi, i: (0, 0)),
        ],
        out_specs=pl.BlockSpec((1, tm, n), lambda bi, i: (bi, i, 0)),
        out_shape=jax.ShapeDtypeStruct((b, s, n), F32),
        compiler_params=_cparams("parallel", "parallel"),
        name="mod_matmul",
    )(x, sc, sh, w)


def _conv_kernel(z_ref, zp_ref, zn_ref, aw_ref, ab_ref, ag_ref, abb_ref, bw_ref,
                 o_ref, ea_ref, eb_ref, sh_ref, *, tile, ch):
    i = pl.program_id(1)
    n = pl.num_programs(1)
    halo = CONV_HALO

    def glu(zz):
        return zz[:, 0:ch] * jax.nn.sigmoid(zz[:, ch:2 * ch])

    def cv(zz):
        return zz[:, 3 * ch:4 * ch] * zz[:, 4 * ch:5 * ch]

    pmask = (i > 0).astype(F32)
    nmask = (i < n - 1).astype(F32)
    zp = zp_ref[0]
    zn = zn_ref[0]
    ea_ref[0:halo, :] = glu(zp) * pmask
    eb_ref[0:halo, :] = cv(zp) * pmask
    ea_ref[halo + tile:2 * halo + tile, :] = glu(zn) * nmask
    eb_ref[halo + tile:2 * halo + tile, :] = cv(zn) * nmask
    ea_ref[halo:halo + tile, :] = glu(z_ref[0])
    eb_ref[halo:halo + tile, :] = cv(z_ref[0])

    rows = 32
    pad_a = CONV_A_WIDTH // 2
    pad_b = CONV_B_WIDTH // 2
    n_sh = sh_ref.shape[1]
    piece = 40
    for b in range(1, 8):
        for r in range(0, n_sh, piece):
            sh_ref[b - 1, r:r + piece, :] = ea_ref[pl.ds(r + b, piece), :]
    groups = rows // 8
    for r0 in range(0, tile, rows):
        acc = jnp.zeros((groups, 8, ch), F32) + ab_ref[...]
        for k in range(CONV_A_WIDTH):
            a8, b = divmod(halo + k - pad_a, 8)
            if b == 0:
                tap = ea_ref[pl.ds(r0 + 8 * a8, rows), :]
            else:
                tap = sh_ref[b - 1, pl.ds(r0 + 8 * a8, rows), :]
            acc = acc + aw_ref[8 * k:8 * k + 8, :] * tap.reshape(groups, 8, ch)
        ya = _layer_norm(acc.reshape(rows, ch), ag_ref[...], abb_ref[...])
        ya = ya * jax.nn.sigmoid(ya)
        cb = jnp.zeros((rows, ch), F32)
        for k in range(CONV_B_WIDTH):
            cb = cb + bw_ref[k:k + 1, :] * eb_ref[pl.ds(halo + r0 + k - pad_b, rows), :]
        yb = z_ref[0, r0:r0 + rows, 2 * ch:3 * ch] * cb
        o_ref[0, r0:r0 + rows, 0:ch] = ya.astype(BF16)
        o_ref[0, r0:r0 + rows, ch:2 * ch] = yb.astype(BF16)


def _conv_mixers(z, a_dw_w, a_dw_b, a_ln_g, a_ln_b, b_dw_w):
    b, s, n = z.shape
    ch = n // 5
    tile = min(s, 256)
    hb = tile // CONV_HALO
    last_hb = s // CONV_HALO - 1
    kern = functools.partial(_conv_kernel, tile=tile, ch=ch)
    vec = lambda v: v.reshape(1, ch)
    return pl.pallas_call(
        kern,
        grid=(b, s // tile),
        in_specs=[
            pl.BlockSpec((1, tile, n), lambda bi, i: (bi, i, 0)),
            pl.BlockSpec((1, CONV_HALO, n), lambda bi, i: (bi, jnp.maximum(i * hb - 1, 0), 0)),
            pl.BlockSpec((1, CONV_HALO, n), lambda bi, i: (bi, jnp.minimum((i + 1) * hb, last_hb), 0)),
            pl.BlockSpec((8 * CONV_A_WIDTH, ch), lambda bi, i: (0, 0)),
            pl.BlockSpec((1, ch), lambda bi, i: (0, 0)),
            pl.BlockSpec((1, ch), lambda bi, i: (0, 0)),
            pl.BlockSpec((1, ch), lambda bi, i: (0, 0)),
            pl.BlockSpec((CONV_B_WIDTH, ch), lambda bi, i: (0, 0)),
        ],
        out_specs=pl.BlockSpec((1, tile, 2 * ch), lambda bi, i: (bi, i, 0)),
        out_shape=jax.ShapeDtypeStruct((b, s, 2 * ch), BF16),
        scratch_shapes=[pltpu.VMEM((tile + 2 * CONV_HALO, ch), F32),
                        pltpu.VMEM((tile + 2 * CONV_HALO, ch), F32),
                        pltpu.VMEM((7, tile + 2 * CONV_HALO - 8, ch), F32)],
        compiler_params=_cparams("parallel", "parallel"),
        name="conv_mixers",
    )(z, z, z, jnp.repeat(a_dw_w, 8, axis=0), vec(a_dw_b), vec(a_ln_g), vec(a_ln_b), b_dw_w)


def _proj_ln_kernel(*refs, n_mix, n_alias, alpha):
    mix_refs = refs[:n_mix]
    (w_ref, x_ref, g_ref, lng_ref, lnb_ref, sc_ref, sh_ref, wrh_ref, wrl_ref, br_ref) = refs[n_mix:n_mix + 10]
    xo_ref, h_ref, lg_ref = refs[n_mix + 10 + n_alias:]
    if n_mix == 2:
        ut = jnp.concatenate([r[0].reshape(r.shape[1] * r.shape[2], r.shape[3]) for r in mix_refs], axis=0)
        y = lax.dot_general(ut, w_ref[...], (((0,), (0,)), ((), ())), preferred_element_type=F32)
    else:
        y = jnp.dot(mix_refs[0][0], w_ref[...], preferred_element_type=F32)
    xn = _layer_norm(alpha * x_ref[0] + g_ref[0] * y, lng_ref[...], lnb_ref[...])
    xo_ref[0] = xn
    h2 = xn * (1.0 + sc_ref[0]) + sh_ref[0]
    h_ref[...] = h2.astype(BF16)
    h_hi = h2.astype(BF16)
    h_lo = (h2 - h_hi.astype(F32)).astype(BF16)
    lg_ref[...] = (jnp.dot(h_hi, wrh_ref[...], preferred_element_type=F32)
                   + (jnp.dot(h_lo, wrh_ref[...], preferred_element_type=F32)
                      + jnp.dot(h_hi, wrl_ref[...], preferred_element_type=F32))
                   + br_ref[...])


def _proj_ln(mix, w_out, x, g, ln_g, ln_b, sc2, sh2, wr, br, alpha, n_tok, tok_off, bufs=None):
    b, s, d = x.shape
    tm = min(s, 512)
    nt = s // tm
    off = tok_off // tm
    if isinstance(mix, tuple):
        mix_specs = [pl.BlockSpec((1, o.shape[1], HEAD_DIM, tm), lambda bi, i: (bi, 0, 0, i)) for o in mix]
        mix_args = list(mix)
    else:
        mix_specs = [pl.BlockSpec((1, tm, mix.shape[2]), lambda bi, i: (bi, i, 0))]
        mix_args = [mix]
    row = lambda: pl.BlockSpec((1, tm, d), lambda bi, i: (bi, i, 0))
    per_b = lambda: pl.BlockSpec((1, 1, d), lambda bi, i: (bi, 0, 0))
    vec = lambda: pl.BlockSpec((1, d), lambda bi, i: (0, 0))
    alias_args = [] if bufs is None else list(bufs)
    n_in = len(mix_args) + 10
    wr_hi = wr.astype(BF16)
    wr_lo = (wr - wr_hi.astype(F32)).astype(BF16)
    kern = functools.partial(_proj_ln_kernel, n_mix=len(mix_args), n_alias=len(alias_args), alpha=alpha)
    return pl.pallas_call(
        kern,
        grid=(b, nt),
        in_specs=mix_specs + [
            pl.BlockSpec(w_out.shape, lambda bi, i: (0, 0)),
            row(), per_b(), vec(), vec(), per_b(), per_b(),
            pl.BlockSpec((d, ROUTER_COLS), lambda bi, i: (0, 0)),
            pl.BlockSpec((d, ROUTER_COLS), lambda bi, i: (0, 0)),
            pl.BlockSpec((1, ROUTER_COLS), lambda bi, i: (0, 0)),
        ] + [pl.BlockSpec(memory_space=pl.ANY) for _ in alias_args],
        out_specs=[row(),
                   pl.BlockSpec((tm, d), lambda bi, i: (off + bi * nt + i, 0)),
                   pl.BlockSpec((tm, ROUTER_COLS), lambda bi, i: (off + bi * nt + i, 0))],
        out_shape=[jax.ShapeDtypeStruct((b, s, d), F32),
                   jax.ShapeDtypeStruct((n_tok, d), BF16),
                   jax.ShapeDtypeStruct((n_tok, ROUTER_COLS), F32)],
        input_output_aliases={n_in: 1, n_in + 1: 2} if alias_args else {},
        compiler_params=_cparams("parallel", "parallel"),
        name="proj_res_ln",
    )(*mix_args, w_out, x, g, ln_g.reshape(1, d), ln_b.reshape(1, d), sc2, sh2, wr_hi, wr_lo, br, *alias_args)


def _rank_kernel(e_ref, tri_ref, rank_ref, cnt_ref, carry_ref):
    @pl.when(pl.program_id(0) == 0)
    def _():
        carry_ref[...] = jnp.zeros_like(carry_ref)

    e = e_ref[0]
    blk = e.shape[1]
    ids = lax.broadcasted_iota(jnp.int32, (N_EXPERTS, blk), 0)
    oh = jnp.where(ids == e, 1.0, 0.0).astype(F32)
    before = jnp.dot(oh.astype(BF16), tri_ref[...], preferred_element_type=F32)
    carry = carry_ref[...]
    before = before + carry[:, 0:1]
    rank_ref[0] = jnp.sum(oh * before, axis=0, keepdims=True).astype(jnp.int32)
    carry = carry + jnp.sum(oh, axis=1, keepdims=True)
    carry_ref[...] = carry
    cnt_ref[...] = carry.astype(jnp.int32)


def _expert_ranks(flat_e):
    m = flat_e.shape[0]
    blk = RANK_BLOCK
    nblk = m // blk
    tri = (lax.broadcasted_iota(jnp.int32, (blk, blk), 0)
           < lax.broadcasted_iota(jnp.int32, (blk, blk), 1)).astype(BF16)
    rank, cnt = pl.pallas_call(
        _rank_kernel,
        grid=(nblk,),
        in_specs=[pl.BlockSpec((1, 1, blk), lambda i: (i, 0, 0)),
                  pl.BlockSpec((blk, blk), lambda i: (0, 0))],
        out_specs=[pl.BlockSpec((1, 1, blk), lambda i: (i, 0, 0)),
                   pl.BlockSpec((N_EXPERTS, LANES), lambda i: (0, 0))],
        out_shape=[jax.ShapeDtypeStruct((nblk, 1, blk), jnp.int32),
                   jax.ShapeDtypeStruct((N_EXPERTS, LANES), jnp.int32)],
        scratch_shapes=[pltpu.VMEM((N_EXPERTS, LANES), F32)],
        compiler_params=_cparams("arbitrary"),
        name="expert_rank",
    )(flat_e.reshape(nblk, 1, blk), tri)
    return rank.reshape(m), cnt[:, 0]


def _to_row_tiles(ref, x, n_rows, tiles):
    for s in range(tiles):
        ref[pl.ds(s, n_rows, stride=tiles), :] = x[:, s * LANES:(s + 1) * LANES]


def _from_row_tiles(ref, n_rows, tiles):
    return jnp.concatenate([ref[pl.ds(s, n_rows, stride=tiles), :] for s in range(tiles)], axis=1)


def _dispatch_kernel(dest_ref, x_ref, init_ref, xs_ref, buf_ref, sem, *, tm, tiles):
    del init_ref
    i = pl.program_id(0)
    n = pl.num_programs(0)
    slot = i % 2

    def drain(s):
        for _ in range(2):
            pltpu.make_async_copy(buf_ref.at[s], xs_ref.at[pl.ds(0, tm * tiles), :], sem.at[s]).wait()

    @pl.when(i >= 2)
    def _():
        drain(slot)

    _to_row_tiles(buf_ref.at[slot], x_ref[...].astype(F32), tm, tiles)

    def issue(r, c):
        src = buf_ref.at[slot, pl.ds(pl.multiple_of(r * tiles, tiles), tiles), :]
        for k in range(2):
            d = pl.multiple_of(dest_ref[0, 0, 2 * r + k] * tiles, tiles)
            pltpu.make_async_copy(src, xs_ref.at[pl.ds(d, tiles), :], sem.at[slot]).start(priority=k)
        return c

    lax.fori_loop(0, tm, issue, 0, unroll=4)

    @pl.when(i == n - 1)
    def _():
        @pl.when(i >= 1)
        def _():
            drain(1 - slot)

        drain(slot)


def _dispatch(h_all, dest, xs_init):
    n, d = h_all.shape
    tiles = d // LANES
    tm = 512
    nt = n // tm
    return pl.pallas_call(
        functools.partial(_dispatch_kernel, tm=tm, tiles=tiles),
        grid=(nt,),
        in_specs=[
            pl.BlockSpec((1, 1, 2 * tm), lambda i: (i, 0, 0), memory_space=pltpu.SMEM),
            pl.BlockSpec((tm, d), lambda i: (i, 0)),
            pl.BlockSpec(memory_space=pl.ANY),
        ],
        out_specs=pl.BlockSpec(memory_space=pl.ANY),
        out_shape=jax.ShapeDtypeStruct(xs_init.shape, F32),
        scratch_shapes=[pltpu.VMEM((2, tm * tiles, LANES), F32), pltpu.SemaphoreType.DMA((2,))],
        input_output_aliases={2: 0},
        compiler_params=_cparams("arbitrary"),
        name="moe_dispatch",
    )(dest.reshape(nt, 1, 2 * tm), h_all, xs_init)


def _experts_kernel(be_ref, nu_ref, x_ref, wg_ref, wu_ref, wd_ref, o_ref, wgb_ref, wub_ref, wdb_ref,
                    *, tb, tiles):
    i = pl.program_id(0)
    changed = jnp.logical_or(i == 0, be_ref[i] != be_ref[jnp.maximum(i - 1, 0)])

    @pl.when(changed)
    def _():
        wgb_ref[...] = wg_ref[0, 0].astype(BF16)
        wub_ref[...] = wu_ref[0, 0].astype(BF16)
        wdb_ref[...] = wd_ref[0, 0].astype(BF16)

    @pl.when(i < nu_ref[0])
    def _():
        x = _from_row_tiles(x_ref, tb, tiles).astype(BF16)
        g = jnp.dot(x, wgb_ref[...], preferred_element_type=F32)
        u = jnp.dot(x, wub_ref[...], preferred_element_type=F32)
        a = (g * jax.nn.sigmoid(g) * u).astype(BF16)
        _to_row_tiles(o_ref, jnp.dot(a, wdb_ref[...], preferred_element_type=F32), tb, tiles)

    @pl.when(i >= nu_ref[0])
    def _():
        o_ref[...] = jnp.zeros_like(o_ref)


def _experts(xs, block_e, n_used, w_gate, w_up, w_down, layer):
    d, hid = w_gate.shape[2], w_gate.shape[3]
    tiles = d // LANES
    tb = EXPERT_BLOCK
    nb = xs.shape[0] // (tb * tiles)
    grid_spec = pltpu.PrefetchScalarGridSpec(
        num_scalar_prefetch=2,
        grid=(nb,),
        in_specs=[
            pl.BlockSpec((tb * tiles, LANES), lambda i, be, nu: (i, 0)),
            pl.BlockSpec((1, 1, d, hid), lambda i, be, nu: (layer, be[i], 0, 0)),
            pl.BlockSpec((1, 1, d, hid), lambda i, be, nu: (layer, be[i], 0, 0)),
            pl.BlockSpec((1, 1, hid, d), lambda i, be, nu: (layer, be[i], 0, 0)),
        ],
        out_specs=pl.BlockSpec((tb * tiles, LANES), lambda i, be, nu: (i, 0)),
        scratch_shapes=[pltpu.VMEM((d, hid), BF16), pltpu.VMEM((d, hid), BF16), pltpu.VMEM((hid, d), BF16)],
    )
    return pl.pallas_call(
        functools.partial(_experts_kernel, tb=tb, tiles=tiles),
        grid_spec=grid_spec,
        out_shape=jax.ShapeDtypeStruct(xs.shape, F32),
        compiler_params=_cparams("arbitrary"),
        name="experts",
    )(block_e, n_used, xs, w_gate, w_up, w_down)


def _moe_ln_kernel(dest_ref, dest_next_ref, x_ref, gt_ref, g_ref, lng_ref, lnb_ref, ys_ref, o_ref,
                   y0_ref, y1_ref, sem0, sem1, *, tm, tiles, alpha):
    i = pl.program_id(0)
    n = pl.num_programs(0)
    slot = i % 2

    def gather(idx_ref, slot):
        def issue(r, c):
            dst = pl.ds(pl.multiple_of(r * tiles, tiles), tiles)
            d0 = pl.multiple_of(idx_ref[0, 0, 2 * r] * tiles, tiles)
            d1 = pl.multiple_of(idx_ref[0, 0, 2 * r + 1] * tiles, tiles)
            pltpu.make_async_copy(ys_ref.at[pl.ds(d0, tiles), :], y0_ref.at[slot, dst, :],
                                  sem0.at[slot]).start(priority=0)
            pltpu.make_async_copy(ys_ref.at[pl.ds(d1, tiles), :], y1_ref.at[slot, dst, :],
                                  sem1.at[slot]).start(priority=1)
            return c

        lax.fori_loop(0, tm, issue, 0, unroll=4)

    @pl.when(i == 0)
    def _():
        gather(dest_ref, 0)

    @pl.when(i + 1 < n)
    def _():
        gather(dest_next_ref, 1 - slot)

    pltpu.make_async_copy(ys_ref.at[pl.ds(0, tm * tiles), :], y0_ref.at[slot], sem0.at[slot]).wait()
    pltpu.make_async_copy(ys_ref.at[pl.ds(0, tm * tiles), :], y1_ref.at[slot], sem1.at[slot]).wait()
    gt = gt_ref[...]
    y = (gt[:, 0:1] * _from_row_tiles(y0_ref.at[slot], tm, tiles)
         + gt[:, 1:2] * _from_row_tiles(y1_ref.at[slot], tm, tiles))
    o_ref[...] = _layer_norm(alpha * x_ref[...] + g_ref[0] * y, lng_ref[...], lnb_ref[...])


def _moe_ln(x, ys, dest, gates, tok_off, g, ln_g, ln_b, alpha):
    b, s, d = x.shape
    tiles = d // LANES
    n_tok = gates.shape[0]
    tm = min(s, 512)
    nt = s // tm
    off = tok_off // tm
    n_steps = b * nt
    vec = lambda: pl.BlockSpec((1, d), lambda i: (0, 0))
    dest_blocks = dest.reshape(n_tok // tm, 1, 2 * tm)
    out = pl.pallas_call(
        functools.partial(_moe_ln_kernel, tm=tm, tiles=tiles, alpha=alpha),
        grid=(n_steps,),
        in_specs=[
            pl.BlockSpec((1, 1, 2 * tm), lambda i: (off + i, 0, 0), memory_space=pltpu.SMEM),
            pl.BlockSpec((1, 1, 2 * tm), lambda i: (off + jnp.minimum(i + 1, n_steps - 1), 0, 0),
                         memory_space=pltpu.SMEM),
            pl.BlockSpec((tm, d), lambda i: (i, 0)),
            pl.BlockSpec((tm, 2), lambda i: (off + i, 0)),
            pl.BlockSpec((1, 1, d), lambda i: (i // nt, 0, 0)),
            vec(), vec(),
            pl.BlockSpec(memory_space=pl.ANY),
        ],
        out_specs=pl.BlockSpec((tm, d), lambda i: (i, 0)),
        out_shape=jax.ShapeDtypeStruct((b * s, d), F32),
        scratch_shapes=[pltpu.VMEM((2, tm * tiles, LANES), F32), pltpu.VMEM((2, tm * tiles, LANES), F32),
                        pltpu.SemaphoreType.DMA((2,)), pltpu.SemaphoreType.DMA((2,))],
        compiler_params=_cparams("arbitrary"),
        name="moe_res_ln",
    )(dest_blocks, dest_blocks, x.reshape(b * s, d), gates, g, ln_g.reshape(1, d), ln_b.reshape(1, d), ys)
    return out.reshape(b, s, d)


def _route(logits):
    n = logits.shape[0]
    l1 = logits[:, :N_GROUPS]
    grp = jnp.argmax(l1, axis=-1).astype(jnp.int32)
    p_grp = 1.0 / jnp.sum(jnp.exp(l1 - jnp.max(l1, axis=-1, keepdims=True)), axis=-1)
    lg2 = logits[:, N_GROUPS:N_GROUPS + N_EXPERTS].reshape(n, N_GROUPS, EXPERTS_PER_GROUP)
    gsel = grp[:, None] == jnp.arange(N_GROUPS, dtype=jnp.int32)[None, :]
    sel = jnp.sum(jnp.where(gsel[:, :, None], lg2, 0.0), axis=1)
    i1 = jnp.argmax(sel, axis=-1).astype(jnp.int32)
    v1 = jnp.max(sel, axis=-1)
    col = jnp.arange(EXPERTS_PER_GROUP, dtype=jnp.int32)[None, :]
    sel2 = jnp.where(col == i1[:, None], -jnp.inf, sel)
    i2 = jnp.argmax(sel2, axis=-1).astype(jnp.int32)
    v2 = jnp.max(sel2, axis=-1)
    e2 = jnp.exp(v2 - v1)
    w1 = 1.0 / (1.0 + e2)
    gates = p_grp[:, None] * jnp.stack([w1, e2 * w1], axis=-1)
    experts = grp[:, None] * EXPERTS_PER_GROUP + jnp.stack([i1, i2], axis=-1)
    return experts, gates.astype(F32)


def _moe_slots(n_tok, d):
    nb = -(-2 * n_tok // EXPERT_BLOCK) + N_EXPERTS
    return jnp.zeros((nb * EXPERT_BLOCK * (d // LANES), LANES), F32)


def _moe(h_all, logits, xs_buf, w_gate, w_up, w_down, layer):
    n, d = h_all.shape
    m = 2 * n
    tb = EXPERT_BLOCK
    experts, gates = _route(logits)
    flat_e = experts.reshape(m)
    rank, counts = _expert_ranks(flat_e)
    padded = (counts + tb - 1) // tb * tb
    pad_end = jnp.cumsum(padded)
    pad_start = pad_end - padded
    onehot = flat_e[:, None] == jnp.arange(N_EXPERTS, dtype=jnp.int32)[None, :]
    dest = jnp.sum(jnp.where(onehot, pad_start[None, :], 0), axis=-1).astype(jnp.int32) + rank
    nb = xs_buf.shape[0] // (tb * (d // LANES))
    blk_start = jnp.arange(nb, dtype=jnp.int32) * tb
    block_e = jnp.minimum(jnp.sum(pad_end[None, :] <= blk_start[:, None], axis=-1), N_EXPERTS - 1)
    n_used = (pad_end[-1] // tb).reshape(1)

    xs = _dispatch(h_all, dest, xs_buf)
    ys = _experts(xs, block_e.astype(jnp.int32), n_used.astype(jnp.int32), w_gate, w_up, w_down, layer)
    return ys, dest, gates, xs


def _prep_kernel(z_ref, cos_ref, sin_ref, gq_ref, gk_ref, grp_ref,
                 qt_ref, kc_ref, vt_ref, qtd_ref, kd_ref, vtd_ref, *, hq, hkv, hd, ts, dchunk):
    cos = cos_ref[...]
    sin = sin_ref[...]
    lane = lax.broadcasted_iota(jnp.int32, (ts, LANES), 1)
    first_half = (lane % 32) < 16

    def norm_rope(x, g):
        ss = jnp.dot(x * x, grp_ref[...], preferred_element_type=F32, precision=HIGHEST)
        xn = x * lax.rsqrt(ss * (1.0 / HEAD_DIM) + RMS_EPS) * g
        partner = jnp.where(first_half, pltpu.roll(xn, LANES - 16, 1), pltpu.roll(xn, 16, 1))
        return xn * cos + partner * sin

    col = 0
    for s in range(hq // 2):
        x = norm_rope(z_ref[0, :, col:col + LANES], gq_ref[...]) * (ATTN_SCALE * LOG2E)
        t = x.T.astype(BF16)
        qt_ref[0, 2 * s] = t[0:HEAD_DIM]
        qt_ref[0, 2 * s + 1] = t[HEAD_DIM:]
        col += LANES
    for s in range(hkv // 2):
        x = norm_rope(z_ref[0, :, col:col + LANES], gk_ref[...]).astype(BF16)
        kc_ref[0, 2 * s, 0] = x[:, 0:HEAD_DIM]
        kc_ref[0, 2 * s + 1, 0] = x[:, HEAD_DIM:]
        col += LANES
    tail = (lax.broadcasted_iota(jnp.int32, (V_ROWS - HEAD_DIM, ts), 0) == 0).astype(BF16)
    tail_d = (lax.broadcasted_iota(jnp.int32, (V_ROWS - HEAD_DIM, dchunk), 0) == 0).astype(BF16)
    for s in range(hkv // 2):
        t = z_ref[0, :, col:col + LANES].T.astype(BF16)
        for hh in range(2):
            vt_ref[0, 2 * s + hh, 0, 0:HEAD_DIM, :] = t[hh * HEAD_DIM:(hh + 1) * HEAD_DIM]
            vt_ref[0, 2 * s + hh, 0, HEAD_DIM:V_ROWS, :] = tail
        col += LANES
    for s in range(hd // 2):
        t = (z_ref[0, :, col:col + LANES] * (ATTN_SCALE * LOG2E)).T.astype(BF16)
        qtd_ref[0, 2 * s] = t[0:HEAD_DIM]
        qtd_ref[0, 2 * s + 1] = t[HEAD_DIM:]
        col += LANES
    for s in range(hd // 2):
        x = z_ref[0, :, col:col + LANES].astype(BF16)
        for c in range(ts // dchunk):
            kd_ref[0, 2 * s, c] = x[c * dchunk:(c + 1) * dchunk, 0:HEAD_DIM]
            kd_ref[0, 2 * s + 1, c] = x[c * dchunk:(c + 1) * dchunk, HEAD_DIM:]
        col += LANES
    for s in range(hd // 2):
        t = z_ref[0, :, col:col + LANES].T.astype(BF16)
        for c in range(ts // dchunk):
            for hh in range(2):
                vtd_ref[0, 2 * s + hh, c, 0:HEAD_DIM, :] = t[hh * HEAD_DIM:(hh + 1) * HEAD_DIM,
                                                             c * dchunk:(c + 1) * dchunk]
                vtd_ref[0, 2 * s + hh, c, HEAD_DIM:V_ROWS, :] = tail_d
        col += LANES


def _attn_prep(z, cos, sin, q_g, k_g, hq, hkv, hd):
    b, s, n = z.shape
    ts = min(s, 512)
    dchunk = 256
    nt = s // ts
    hdim = HEAD_DIM
    gq = jnp.tile(q_g, 2).reshape(1, LANES)
    gk = jnp.tile(k_g, 2).reshape(1, LANES)
    li = jnp.arange(LANES)
    grp = (li[:, None] // hdim == li[None, :] // hdim).astype(F32)
    kern = functools.partial(_prep_kernel, hq=hq, hkv=hkv, hd=hd, ts=ts, dchunk=dchunk)
    const = lambda shape: pl.BlockSpec(shape, lambda bi, i: (0,) * len(shape))
    return pl.pallas_call(
        kern,
        grid=(b, nt),
        in_specs=[
            pl.BlockSpec((1, ts, n), lambda bi, i: (bi, i, 0)),
            pl.BlockSpec((ts, LANES), lambda bi, i: (i, 0)),
            pl.BlockSpec((ts, LANES), lambda bi, i: (i, 0)),
            const((1, LANES)), const((1, LANES)), const((LANES, LANES)),
        ],
        out_specs=[
            pl.BlockSpec((1, hq, hdim, ts), lambda bi, i: (bi, 0, 0, i)),
            pl.BlockSpec((1, hkv, 1, ts, hdim), lambda bi, i: (bi, 0, i, 0, 0)),
            pl.BlockSpec((1, hkv, 1, V_ROWS, ts), lambda bi, i: (bi, 0, i, 0, 0)),
            pl.BlockSpec((1, hd, hdim, ts), lambda bi, i: (bi, 0, 0, i)),
            pl.BlockSpec((1, hd, ts // dchunk, dchunk, hdim), lambda bi, i: (bi, 0, i, 0, 0)),
            pl.BlockSpec((1, hd, ts // dchunk, V_ROWS, dchunk), lambda bi, i: (bi, 0, i, 0, 0)),
        ],
        out_shape=[
            jax.ShapeDtypeStruct((b, hq, hdim, s), BF16),
            jax.ShapeDtypeStruct((b, hkv, nt, ts, hdim), BF16),
            jax.ShapeDtypeStruct((b, hkv, nt, V_ROWS, ts), BF16),
            jax.ShapeDtypeStruct((b, hd, hdim, s), BF16),
            jax.ShapeDtypeStruct((b, hd, s // dchunk, dchunk, hdim), BF16),
            jax.ShapeDtypeStruct((b, hd, s // dchunk, V_ROWS, dchunk), BF16),
        ],
        compiler_params=_cparams("parallel", "parallel"),
        name="attn_prep",
    )(z, cos, sin, gq, gk, grp)


def _rope_tables(s):
    t = jnp.arange(s)
    row = (t // GRID_W).astype(F32)
    colp = (t % GRID_W).astype(F32)
    axis_dim = HEAD_DIM // 2
    inv = ROPE_THETA ** (-jnp.arange(0, axis_dim, 2, dtype=F32) / axis_dim)
    ar = row[:, None] * inv
    ac = colp[:, None] * inv
    cr, sr, cc, sc = jnp.cos(ar), jnp.sin(ar), jnp.cos(ac), jnp.sin(ac)
    cos = jnp.concatenate([cr, cr, cc, cc], axis=-1)
    sin = jnp.concatenate([-sr, sr, -sc, sc], axis=-1)
    return jnp.tile(cos, (1, 2)), jnp.tile(sin, (1, 2))


def _gqa_kernel(qt_ref, k_ref, vt_ref, kc_ref, vtc_ref, o_ref, acc_ref, s_ref, sc_ref, *, group):
    n_chunks = k_ref.shape[2]
    tq = qt_ref.shape[3]

    def q_all():
        return jnp.concatenate([qt_ref[0, g] for g in range(group)], axis=1)

    def scores(kb, slot):
        st = jnp.dot(kb, q_all(), preferred_element_type=F32)
        s_ref[slot] = st
        return jnp.max(st, axis=0, keepdims=True)

    def absorb(slot, vb, cmax, m):
        m_new = jnp.maximum(m, cmax)
        alpha = jnp.exp2(m - m_new)
        p = jnp.exp2(s_ref[slot] - m_new)
        acc_ref[...] = alpha * acc_ref[...] + jnp.dot(vb, p.astype(BF16), preferred_element_type=F32)
        return m_new

    per_trip = 16 if n_chunks % 16 == 0 else 2
    st = jnp.dot(kc_ref[0, 0], q_all(), preferred_element_type=F32)
    sc_ref[...] = st
    m = jnp.max(st, axis=0, keepdims=True)
    cm = scores(k_ref[0, 0, 0], 0)
    p = jnp.exp2(sc_ref[...] - m)
    acc_ref[...] = jnp.dot(vtc_ref[0, 0], p.astype(BF16), preferred_element_type=F32)

    def body(i, carry):
        m, cm_cur = carry
        j0 = per_trip * i
        for u in range(per_trip):
            cm_next = scores(k_ref[0, 0, jnp.minimum(j0 + u + 1, n_chunks - 1)], (u + 1) % 2)
            m = absorb(u % 2, vt_ref[0, 0, j0 + u], cm_cur, m)
            cm_cur = cm_next
        return m, cm_cur

    lax.fori_loop(0, n_chunks // per_trip, body, (m, cm))
    out = acc_ref[0:HEAD_DIM, :] / acc_ref[HEAD_DIM:HEAD_DIM + 1, :]
    for g in range(group):
        o_ref[0, g] = out[:, g * tq:(g + 1) * tq].astype(BF16)


def _gqa(qt, k, vt, k_ctx, vt_ctx):
    b, hq, hdim, s = qt.shape
    hkv, n_chunks, tk = k.shape[1], k.shape[2], k.shape[3]
    n_ctx = k_ctx.shape[2]
    group = hq // hkv
    tq = min(s, 256)
    return pl.pallas_call(
        functools.partial(_gqa_kernel, group=group),
        grid=(b, hkv, s // tq),
        in_specs=[
            pl.BlockSpec((1, group, hdim, tq), lambda bi, h, i: (bi, h, 0, i)),
            pl.BlockSpec((1, 1, n_chunks, tk, hdim), lambda bi, h, i: (bi, h, 0, 0, 0)),
            pl.BlockSpec((1, 1, n_chunks, V_ROWS, tk), lambda bi, h, i: (bi, h, 0, 0, 0)),
            pl.BlockSpec((1, 1, n_ctx, hdim), lambda bi, h, i: (bi, h, 0, 0)),
            pl.BlockSpec((1, 1, V_ROWS, n_ctx), lambda bi, h, i: (bi, h, 0, 0)),
        ],
        out_specs=pl.BlockSpec((1, group, hdim, tq), lambda bi, h, i: (bi, h, 0, i)),
        out_shape=jax.ShapeDtypeStruct((b, hq, hdim, s), BF16),
        scratch_shapes=[pltpu.VMEM((V_ROWS, group * tq), F32), pltpu.VMEM((2, tk, group * tq), F32),
                        pltpu.VMEM((n_ctx, group * tq), F32)],
        compiler_params=_cparams("parallel", "parallel", "parallel"),
        name="gqa_attention",
    )(qt, k, vt, k_ctx, vt_ctx)


def _ctx_gqa_kernel(qt_ref, k_ref, vt_ref, o_ref):
    st = jnp.dot(k_ref[0, 0], qt_ref[0, 0], preferred_element_type=F32)
    p = jnp.exp2(st - jnp.max(st, axis=0, keepdims=True))
    acc = jnp.dot(vt_ref[0, 0], p.astype(BF16), preferred_element_type=F32)
    o_ref[0, 0] = (acc[0:HEAD_DIM] / acc[HEAD_DIM:HEAD_DIM + 1]).astype(BF16)


def _ctx_gqa(qt, k, vt):
    b, hq, hdim, n = qt.shape
    group = hq // k.shape[1]
    return pl.pallas_call(
        _ctx_gqa_kernel,
        grid=(b, hq),
        in_specs=[
            pl.BlockSpec((1, 1, hdim, n), lambda bi, h: (bi, h, 0, 0)),
            pl.BlockSpec((1, 1, n, hdim), lambda bi, h: (bi, h // group, 0, 0)),
            pl.BlockSpec((1, 1, V_ROWS, n), lambda bi, h: (bi, h // group, 0, 0)),
        ],
        out_specs=pl.BlockSpec((1, 1, hdim, n), lambda bi, h: (bi, h, 0, 0)),
        out_shape=jax.ShapeDtypeStruct((b, hq, hdim, n), BF16),
        compiler_params=_cparams("parallel", "parallel"),
        name="ctx_gqa",
    )(qt, k, vt)


def _na_kernel(qt_ref, k_ref, vt_ref, kc_ref, vtc_ref, bias_ref, o_ref, s_ref, *, n_band, nq, per_step):
    step = pl.program_id(2)
    tq = NA_Q_ROWS * GRID_W
    cw = k_ref.shape[3]
    n_ctx = kc_ref.shape[2]

    def scores(j, slot):
        i = step * per_step + j
        c0 = jnp.clip(i - 1, 0, nq - n_band)
        variant = jnp.where(i == 0, 0, jnp.where(i == nq - 1, 2, 1))
        qt = qt_ref[0, 0, :, j * tq:(j + 1) * tq]
        st = jnp.dot(kc_ref[0, 0], qt, preferred_element_type=F32)
        s_ref[slot, 0:n_ctx, :] = st
        m = jnp.max(st, axis=0, keepdims=True)
        for c in range(n_band):
            b0 = pl.multiple_of((variant * n_band + c) * cw, cw)
            st = (jnp.dot(k_ref[0, 0, c0 + c], qt, preferred_element_type=F32)
                  + bias_ref[0, pl.ds(b0, cw), :])
            s_ref[slot, n_ctx + c * cw:n_ctx + (c + 1) * cw, :] = st
            m = jnp.maximum(m, jnp.max(st, axis=0, keepdims=True))
        return m, c0

    def finish(j, slot, m, c0):
        p = jnp.exp2(s_ref[slot, 0:n_ctx, :] - m).astype(BF16)
        acc = jnp.dot(vtc_ref[0, 0], p, preferred_element_type=F32)
        for c in range(n_band):
            p = jnp.exp2(s_ref[slot, n_ctx + c * cw:n_ctx + (c + 1) * cw, :] - m).astype(BF16)
            acc = acc + jnp.dot(vt_ref[0, 0, c0 + c], p, preferred_element_type=F32)
        o_ref[0, 0, :, j * tq:(j + 1) * tq] = (acc[0:HEAD_DIM] / acc[HEAD_DIM:HEAD_DIM + 1]).astype(BF16)

    nxt = scores(0, 0)
    for j in range(per_step):
        cur = nxt
        if j + 1 < per_step:
            nxt = scores(j + 1, (j + 1) % 2)
        finish(j, j % 2, *cur)


def _na_bias_table(rel_bias, rows):
    w = GRID_W
    h, n_rel_rows, n_rel_cols = rel_bias.shape
    qcol = np.arange(w)
    cstart = np.clip(qcol - NA_WIN_COLS // 2, 0, w - NA_WIN_COLS)
    in_win = (qcol[None, :] >= cstart[:, None]) & (qcol[None, :] < cstart[:, None] + NA_WIN_COLS)
    col_idx = np.clip(qcol[None, :] - qcol[:, None] + NA_WIN_COLS - 1, 0, n_rel_cols - 1)
    onehot = (col_idx.reshape(-1)[None, :] == np.arange(n_rel_cols)[:, None]).astype(np.float32)
    cols = jnp.einsum('hrj,jn->hrn', rel_bias, onehot, precision=HIGHEST).reshape(h, n_rel_rows, w, w)
    cols = jnp.where(in_win[None, None], cols, NEG_BIG)
    ext = jnp.concatenate([cols, jnp.full((h, 1, w, w), NEG_BIG, F32)], axis=1)
    a = np.arange(NA_Q_ROWS)[:, None]
    c = np.arange(NA_BAND_ROWS)[None, :]
    tables = []
    for r0, rs0 in ((0, 0), (NA_Q_ROWS, 0), (rows - NA_Q_ROWS, rows - NA_BAND_ROWS)):
        r = r0 + a
        kr = rs0 + c
        rs = np.clip(r - NA_WIN_ROWS // 2, 0, rows - NA_WIN_ROWS)
        valid = (kr >= rs) & (kr < rs + NA_WIN_ROWS)
        row_idx = np.where(valid, kr - r + NA_WIN_ROWS - 1, n_rel_rows)
        bias = ext[:, row_idx.reshape(-1)].reshape(h, NA_Q_ROWS, NA_BAND_ROWS, w, w)
        tables.append(bias.transpose(0, 2, 4, 1, 3).reshape(h, NA_BAND_ROWS * w, NA_Q_ROWS * w))
    return (jnp.stack(tables, axis=1) * LOG2E).astype(F32)


def _na(qt, k, vt, k_ctx, vt_ctx, bias):
    b, h, hdim, s = qt.shape
    n_chunks, cw = k.shape[2], k.shape[3]
    n_ctx = k_ctx.shape[2]
    tq = NA_Q_ROWS * GRID_W
    nq = s // tq
    n_band = NA_BAND_ROWS * GRID_W // cw
    band_w = NA_BAND_ROWS * GRID_W
    per_step = math.gcd(nq, NA_BLOCKS_PER_STEP)
    return pl.pallas_call(
        functools.partial(_na_kernel, n_band=n_band, nq=nq, per_step=per_step),
        grid=(b, h, nq // per_step),
        in_specs=[
            pl.BlockSpec((1, 1, hdim, per_step * tq), lambda bi, hi, i: (bi, hi, 0, i)),
            pl.BlockSpec((1, 1, n_chunks, cw, hdim), lambda bi, hi, i: (bi, hi, 0, 0, 0)),
            pl.BlockSpec((1, 1, n_chunks, V_ROWS, cw), lambda bi, hi, i: (bi, hi, 0, 0, 0)),
            pl.BlockSpec((1, 1, n_ctx, hdim), lambda bi, hi, i: (bi, hi, 0, 0)),
            pl.BlockSpec((1, 1, V_ROWS, n_ctx), lambda bi, hi, i: (bi, hi, 0, 0)),
            pl.BlockSpec((1, 3 * band_w, tq), lambda bi, hi, i: (hi, 0, 0)),
        ],
        out_specs=pl.BlockSpec((1, 1, hdim, per_step * tq), lambda bi, hi, i: (bi, hi, 0, i)),
        out_shape=jax.ShapeDtypeStruct((b, h, hdim, s), BF16),
        scratch_shapes=[pltpu.VMEM((2, n_ctx + band_w, tq), F32)],
        compiler_params=_cparams("parallel", "parallel", "parallel"),
        name="na_attention",
    )(qt, k, vt, k_ctx, vt_ctx, bias.reshape(h, 3 * band_w, tq))


def kernel(x, c, ctx, c_ctx, ada_w, ada_b, ln1_g, ln1_b, ln2_g, ln2_b, conv_w_in, conv_w_out, conv_a_dw_w, conv_a_dw_b, conv_a_ln_g, conv_a_ln_b, conv_b_dw_w, attn_w_in, attn_w_out, q_norm_g, k_norm_g, na_rel_bias, router_w1, router_b1, router_w2, router_b2, expert_w_gate, expert_w_up, expert_w_down):
    bsz, seq, d = x.shape
    n_ctx = ctx.shape[1]
    depth = ada_w.shape[0]
    alpha = (2 * depth) ** 0.25
    hq = (d // 2) // HEAD_DIM
    hkv = hq // 4
    hd = (d // 2) // HEAD_DIM

    cond8 = jnp.zeros((8, d), F32).at[:bsz].set(c).at[bsz].set(c_ctx)
    mods = _ada(cond8, ada_w, ada_b)

    def lat_mod(i, k):
        return mods[i, :bsz, k * d:(k + 1) * d].reshape(bsz, 1, d)

    def ctx_mod(i, k):
        return jnp.broadcast_to(mods[i, bsz, k * d:(k + 1) * d].reshape(1, 1, d), (bsz, 1, d))

    wr = jnp.zeros((depth, d, ROUTER_COLS), F32)
    wr = wr.at[:, :, :N_GROUPS].set(router_w1).at[:, :, N_GROUPS:N_GROUPS + N_EXPERTS].set(router_w2)
    br = jnp.zeros((depth, 1, ROUTER_COLS), F32)
    br = br.at[:, 0, :N_GROUPS].set(router_b1).at[:, 0, N_GROUPS:N_GROUPS + N_EXPERTS].set(router_b2)

    cos_lat, sin_lat = _rope_tables(seq)
    cos_ctx = jnp.ones((n_ctx, LANES), F32)
    sin_ctx = jnp.zeros((n_ctx, LANES), F32)

    xs_buf = _moe_slots(bsz * seq + bsz * n_ctx, d)

    x_lat, x_ctx = x, ctx
    for i in range(depth):
        last = i == depth - 1
        odd = i % 2 == 1
        j = i // 2
        use_ctx = odd or not last
        if odd:
            w_in = attn_w_in[j].astype(BF16)
            w_out = attn_w_out[j].astype(BF16)
            z_lat = _mod_matmul(x_lat, lat_mod(i, 1), lat_mod(i, 0), w_in)
            z_ctx = _mod_matmul(x_ctx, ctx_mod(i, 1), ctx_mod(i, 0), w_in)
            qt, kc, vtc, qtd, kd, vtd = _attn_prep(z_lat, cos_lat, sin_lat, q_norm_g[j], k_norm_g[j], hq, hkv, hd)
            cqt, ckc, cvtc, cqtd, ckd, cvtd = _attn_prep(z_ctx, cos_ctx, sin_ctx, q_norm_g[j], k_norm_g[j],
                                                         hq, hkv, hd)
            ckc = ckc.reshape(bsz, hkv, n_ctx, HEAD_DIM)
            cvtc = cvtc.reshape(bsz, hkv, V_ROWS, n_ctx)
            ckd = ckd.reshape(bsz, hd, n_ctx, HEAD_DIM)
            cvtd = cvtd.reshape(bsz, hd, V_ROWS, n_ctx)
            ot_c = _gqa(qt, kc, vtc, ckc, cvtc)
            bias = _na_bias_table(na_rel_bias[j], seq // GRID_W)
            ot_d = _na(qtd, kd, vtd, ckd, cvtd, bias)
            mix_lat = (ot_c, ot_d)
            if not last:
                mix_ctx = (_ctx_gqa(cqt, ckc, cvtc), _ctx_gqa(cqtd, ckd, cvtd))
        else:
            w_in = conv_w_in[j].astype(BF16)
            w_out = conv_w_out[j].astype(BF16)
            conv_w = (conv_a_dw_w[j], conv_a_dw_b[j], conv_a_ln_g[j], conv_a_ln_b[j], conv_b_dw_w[j])
            z_lat = _mod_matmul(x_lat, lat_mod(i, 1), lat_mod(i, 0), w_in)
            mix_lat = _conv_mixers(z_lat, *conv_w)
            if not last:
                z_ctx = _mod_matmul(x_ctx, ctx_mod(i, 1), ctx_mod(i, 0), w_in)
                mix_ctx = _conv_mixers(z_ctx, *conv_w)

        n_lat = bsz * seq
        n_tok = n_lat if last else n_lat + bsz * n_ctx
        bufs = None if last else (jnp.zeros((n_tok, d), BF16), jnp.zeros((n_tok, ROUTER_COLS), F32))
        x_lat, h_all, lg_all = _proj_ln(mix_lat, w_out, x_lat, lat_mod(i, 2), ln1_g[i], ln1_b[i],
                                        lat_mod(i, 4), lat_mod(i, 3), wr[i], br[i], alpha, n_tok, 0, bufs=bufs)
        moe_w = (expert_w_gate, expert_w_up, expert_w_down, i)
        if last:
            ys, dest, gates, xs_buf = _moe(h_all, lg_all, xs_buf, *moe_w)
            x_lat = _moe_ln(x_lat, ys, dest, gates, 0, lat_mod(i, 5), ln2_g[i], ln2_b[i], alpha)
        else:
            x_ctx, h_all, lg_all = _proj_ln(mix_ctx, w_out, x_ctx, ctx_mod(i, 2), ln1_g[i], ln1_b[i],
                                            ctx_mod(i, 4), ctx_mod(i, 3), wr[i], br[i], alpha, n_tok, n_lat,
                                            bufs=(h_all, lg_all))
            ys, dest, gates, xs_buf = _moe(h_all, lg_all, xs_buf, *moe_w)
            x_lat = _moe_ln(x_lat, ys, dest, gates, 0, lat_mod(i, 5), ln2_g[i], ln2_b[i], alpha)
            x_ctx = _moe_ln(x_ctx, ys, dest, gates, n_lat, ctx_mod(i, 5), ln2_g[i], ln2_b[i], alpha)
    return x_lat
```

```python
import functools
import math

import jax
import jax.numpy as jnp
import numpy as np
from jax import lax
from jax.experimental import pallas as pl
from jax.experimental.pallas import tpu as pltpu

F32 = jnp.float32
BF16 = jnp.bfloat16
HIGHEST = lax.Precision.HIGHEST

GRID_W = 64
HEAD_DIM = 64
V_ROWS = HEAD_DIM + 16
ATTN_SCALE = HEAD_DIM ** -0.5
LOG2E = math.log2(math.e)
CONV_A_WIDTH = 31
CONV_B_WIDTH = 3
CONV_HALO = 16
NA_WIN_ROWS = 8
NA_WIN_COLS = 16
NA_Q_ROWS = 4
NA_BAND_ROWS = 12
NA_BLOCKS_PER_STEP = 16
ROPE_THETA = 10000.0
N_GROUPS = 4
EXPERTS_PER_GROUP = 8
N_EXPERTS = N_GROUPS * EXPERTS_PER_GROUP
LN_EPS = 1e-5
RMS_EPS = 1e-6
NEG_BIG = -1e30

VMEM_LIMIT_BYTES = 48 * 1024 * 1024
LANES = 128
ROUTER_COLS = LANES
EXPERT_BLOCK = 256
RANK_BLOCK = 512


def _cparams(*sem):
    return pltpu.CompilerParams(dimension_semantics=sem, vmem_limit_bytes=VMEM_LIMIT_BYTES)


def _layer_norm(x, g, b):
    mu = jnp.mean(x, axis=-1, keepdims=True)
    xc = x - mu
    var = jnp.mean(xc * xc, axis=-1, keepdims=True)
    return xc * lax.rsqrt(var + LN_EPS) * g + b


def _ada_kernel(c_ref, w_ref, b_ref, o_ref):
    c = c_ref[...]
    s = c * jax.nn.sigmoid(c)
    o_ref[0] = jnp.dot(s, w_ref[0], preferred_element_type=F32, precision=HIGHEST) + b_ref[0]


def _ada(cond8, ada_w, ada_b):
    n_layers, d, n = ada_w.shape
    tn = 1536
    return pl.pallas_call(
        _ada_kernel,
        grid=(n_layers, n // tn),
        in_specs=[
            pl.BlockSpec((8, d), lambda l, j: (0, 0)),
            pl.BlockSpec((1, d, tn), lambda l, j: (l, 0, j)),
            pl.BlockSpec((1, 1, tn), lambda l, j: (l, 0, j)),
        ],
        out_specs=pl.BlockSpec((1, 8, tn), lambda l, j: (l, 0, j)),
        out_shape=jax.ShapeDtypeStruct((n_layers, 8, n), F32),
        compiler_params=_cparams("parallel", "parallel"),
        name="ada_mod",
    )(cond8, ada_w, ada_b.reshape(n_layers, 1, n))


def _mod_matmul_kernel(x_ref, sc_ref, sh_ref, w_ref, o_ref):
    h = x_ref[0] * (1.0 + sc_ref[0]) + sh_ref[0]
    o_ref[0] = jnp.dot(h.astype(BF16), w_ref[...], preferred_element_type=F32)


def _mod_matmul(x, sc, sh, w):
    b, s, d = x.shape
    n = w.shape[1]
    tm = min(s, 512)
    return pl.pallas_call(
        _mod_matmul_kernel,
        grid=(b, s // tm),
        in_specs=[
            pl.BlockSpec((1, tm, d), lambda bi, i: (bi, i, 0)),
            pl.BlockSpec((1, 1, d), lambda bi, i: (bi, 0, 0)),
            pl.BlockSpec((1, 1, d), lambda bi, i: (bi, 0, 0)),
            pl.BlockSpec((d, n), lambda bi, i: (0, 0)),
        ],
        out_specs=pl.BlockSpec((1, tm, n), lambda bi, i: (bi, i, 0)),
        out_shape=jax.ShapeDtypeStruct((b, s, n), F32),
        compiler_params=_cparams("parallel", "parallel"),
        name="mod_matmul",
    )(x, sc, sh, w)


def _conv_kernel(z_ref, zp_ref, zn_ref, aw_ref, ab_ref, ag_ref, abb_ref, bw_ref,
                 o_ref, ea_ref, eb_ref, sh_ref, *, tile, ch):
    i = pl.program_id(1)
    n = pl.num_programs(1)
    halo = CONV_HALO

    def glu(zz):
        return zz[:, 0:ch] * jax.nn.sigmoid(zz[:, ch:2 * ch])

    def cv(zz):
        return zz[:, 3 * ch:4 * ch] * zz[:, 4 * ch:5 * ch]

    pmask = (i > 0).astype(F32)
    nmask = (i < n - 1).astype(F32)
    zp = zp_ref[0]
    zn = zn_ref[0]
    ea_ref[0:halo, :] = glu(zp) * pmask
    eb_ref[0:halo, :] = cv(zp) * pmask
    ea_ref[halo + tile:2 * halo + tile, :] = glu(zn) * nmask
    eb_ref[halo + tile:2 * halo + tile, :] = cv(zn) * nmask
    ea_ref[halo:halo + tile, :] = glu(z_ref[0])
    eb_ref[halo:halo + tile, :] = cv(z_ref[0])

    rows = 32
    pad_a = CONV_A_WIDTH // 2
    pad_b = CONV_B_WIDTH // 2
    n_sh = sh_ref.shape[1]
    piece = 40
    for b in range(1, 8):
        for r in range(0, n_sh, piece):
            sh_ref[b - 1, r:r + piece, :] = ea_ref[pl.ds(r + b, piece), :]
    groups = rows // 8
    for r0 in range(0, tile, rows):
        acc = jnp.zeros((groups, 8, ch), F32) + ab_ref[...]
        for k in range(CONV_A_WIDTH):
            a8, b = divmod(halo + k - pad_a, 8)
            if b == 0:
                tap = ea_ref[pl.ds(r0 + 8 * a8, rows), :]
            else:
                tap = sh_ref[b - 1, pl.ds(r0 + 8 * a8, rows), :]
            acc = acc + aw_ref[8 * k:8 * k + 8, :] * tap.reshape(groups, 8, ch)
        ya = _layer_norm(acc.reshape(rows, ch), ag_ref[...], abb_ref[...])
        ya = ya * jax.nn.sigmoid(ya)
        cb = jnp.zeros((rows, ch), F32)
        for k in range(CONV_B_WIDTH):
            cb = cb + bw_ref[k:k + 1, :] * eb_ref[pl.ds(halo + r0 + k - pad_b, rows), :]
        yb = z_ref[0, r0:r0 + rows, 2 * ch:3 * ch] * cb
        o_ref[0, r0:r0 + rows, 0:ch] = ya.astype(BF16)
        o_ref[0, r0:r0 + rows, ch:2 * ch] = yb.astype(BF16)


def _conv_mixers(z, a_dw_w, a_dw_b, a_ln_g, a_ln_b, b_dw_w):
    b, s, n = z.shape
    ch = n // 5
    tile = min(s, 256)
    hb = tile // CONV_HALO
    last_hb = s // CONV_HALO - 1
    kern = functools.partial(_conv_kernel, tile=tile, ch=ch)
    vec = lambda v: v.reshape(1, ch)
    return pl.pallas_call(
        kern,
        grid=(b, s // tile),
        in_specs=[
            pl.BlockSpec((1, tile, n), lambda bi, i: (bi, i, 0)),
            pl.BlockSpec((1, CONV_HALO, n), lambda bi, i: (bi, jnp.maximum(i * hb - 1, 0), 0)),
            pl.BlockSpec((1, CONV_HALO, n), lambda bi, i: (bi, jnp.minimum((i + 1) * hb, last_hb), 0)),
            pl.BlockSpec((8 * CONV_A_WIDTH, ch), lambda bi, i: (0, 0)),
            pl.BlockSpec((1, ch), lambda bi, i: (0, 0)),
            pl.BlockSpec((1, ch), lambda bi, i: (0, 0)),
            pl.BlockSpec((1, ch), lambda bi, i: (0, 0)),
            pl.BlockSpec((CONV_B_WIDTH, ch), lambda bi, i: (0, 0)),
        ],
        out_specs=pl.BlockSpec((1, tile, 2 * ch), lambda bi, i: (bi, i, 0)),
        out_shape=jax.ShapeDtypeStruct((b, s, 2 * ch), BF16),
        scratch_shapes=[pltpu.VMEM((tile + 2 * CONV_HALO, ch), F32),
                        pltpu.VMEM((tile + 2 * CONV_HALO, ch), F32),
                        pltpu.VMEM((7, tile + 2 * CONV_HALO - 8, ch), F32)],
        compiler_params=_cparams("parallel", "parallel"),
        name="conv_mixers",
    )(z, z, z, jnp.repeat(a_dw_w, 8, axis=0), vec(a_dw_b), vec(a_ln_g), vec(a_ln_b), b_dw_w)


def _proj_ln_kernel(*refs, n_mix, n_alias, alpha):
    mix_refs = refs[:n_mix]
    (w_ref, x_ref, g_ref, lng_ref, lnb_ref, sc_ref, sh_ref, wrh_ref, wrl_ref, br_ref) = refs[n_mix:n_mix + 10]
    xo_ref, h_ref, lg_ref = refs[n_mix + 10 + n_alias:]
    if n_mix == 2:
        ut = jnp.concatenate([r[0].reshape(r.shape[1] * r.shape[2], r.shape[3]) for r in mix_refs], axis=0)
        y = lax.dot_general(ut, w_ref[...], (((0,), (0,)), ((), ())), preferred_element_type=F32)
    else:
        y = jnp.dot(mix_refs[0][0], w_ref[...], preferred_element_type=F32)
    xn = _layer_norm(alpha * x_ref[0] + g_ref[0] * y, lng_ref[...], lnb_ref[...])
    xo_ref[0] = xn
    h2 = xn * (1.0 + sc_ref[0]) + sh_ref[0]
    h_ref[...] = h2.astype(BF16)
    h_hi = h2.astype(BF16)
    h_lo = (h2 - h_hi.astype(F32)).astype(BF16)
    lg_ref[...] = (jnp.dot(h_hi, wrh_ref[...], preferred_element_type=F32)
                   + (jnp.dot(h_lo, wrh_ref[...], preferred_element_type=F32)
                      + jnp.dot(h_hi, wrl_ref[...], preferred_element_type=F32))
                   + br_ref[...])


def _proj_ln(mix, w_out, x, g, ln_g, ln_b, sc2, sh2, wr, br, alpha, n_tok, tok_off, bufs=None):
    b, s, d = x.shape
    tm = min(s, 512)
    nt = s // tm
    off = tok_off // tm
    if isinstance(mix, tuple):
        mix_specs = [pl.BlockSpec((1, o.shape[1], HEAD_DIM, tm), lambda bi, i: (bi, 0, 0, i)) for o in mix]
        mix_args = list(mix)
    else:
        mix_specs = [pl.BlockSpec((1, tm, mix.shape[2]), lambda bi, i: (bi, i, 0))]
        mix_args = [mix]
    row = lambda: pl.BlockSpec((1, tm, d), lambda bi, i: (bi, i, 0))
    per_b = lambda: pl.BlockSpec((1, 1, d), lambda bi, i: (bi, 0, 0))
    vec = lambda: pl.BlockSpec((1, d), lambda bi, i: (0, 0))
    alias_args = [] if bufs is None else list(bufs)
    n_in = len(mix_args) + 10
    wr_hi = wr.astype(BF16)
    wr_lo = (wr - wr_hi.astype(F32)).astype(BF16)
    kern = functools.partial(_proj_ln_kernel, n_mix=len(mix_args), n_alias=len(alias_args), alpha=alpha)
    return pl.pallas_call(
        kern,
        grid=(b, nt),
        in_specs=mix_specs + [
            pl.BlockSpec(w_out.shape, lambda bi, i: (0, 0)),
            row(), per_b(), vec(), vec(), per_b(), per_b(),
            pl.BlockSpec((d, ROUTER_COLS), lambda bi, i: (0, 0)),
            pl.BlockSpec((d, ROUTER_COLS), lambda bi, i: (0, 0)),
            pl.BlockSpec((1, ROUTER_COLS), lambda bi, i: (0, 0)),
        ] + [pl.BlockSpec(memory_space=pl.ANY) for _ in alias_args],
        out_specs=[row(),
                   pl.BlockSpec((tm, d), lambda bi, i: (off + bi * nt + i, 0)),
                   pl.BlockSpec((tm, ROUTER_COLS), lambda bi, i: (off + bi * nt + i, 0))],
        out_shape=[jax.ShapeDtypeStruct((b, s, d), F32),
                   jax.ShapeDtypeStruct((n_tok, d), BF16),
                   jax.ShapeDtypeStruct((n_tok, ROUTER_COLS), F32)],
        input_output_aliases={n_in: 1, n_in + 1: 2} if alias_args else {},
        compiler_params=_cparams("parallel", "parallel"),
        name="proj_res_ln",
    )(*mix_args, w_out, x, g, ln_g.reshape(1, d), ln_b.reshape(1, d), sc2, sh2, wr_hi, wr_lo, br, *alias_args)


def _rank_kernel(e_ref, tri_ref, rank_ref, cnt_ref, carry_ref):
    @pl.when(pl.program_id(0) == 0)
    def _():
        carry_ref[...] = jnp.zeros_like(carry_ref)

    e = e_ref[0]
    blk = e.shape[1]
    ids = lax.broadcasted_iota(jnp.int32, (N_EXPERTS, blk), 0)
    oh = jnp.where(ids == e, 1.0, 0.0).astype(F32)
    before = jnp.dot(oh.astype(BF16), tri_ref[...], preferred_element_type=F32)
    carry = carry_ref[...]
    before = before + carry[:, 0:1]
    rank_ref[0] = jnp.sum(oh * before, axis=0, keepdims=True).astype(jnp.int32)
    carry = carry + jnp.sum(oh, axis=1, keepdims=True)
    carry_ref[...] = carry
    cnt_ref[...] = carry.astype(jnp.int32)


def _expert_ranks(flat_e):
    m = flat_e.shape[0]
    blk = RANK_BLOCK
    nblk = m // blk
    tri = (lax.broadcasted_iota(jnp.int32, (blk, blk), 0)
           < lax.broadcasted_iota(jnp.int32, (blk, blk), 1)).astype(BF16)
    rank, cnt = pl.pallas_call(
        _rank_kernel,
        grid=(nblk,),
        in_specs=[pl.BlockSpec((1, 1, blk), lambda i: (i, 0, 0)),
                  pl.BlockSpec((blk, blk), lambda i: (0, 0))],
        out_specs=[pl.BlockSpec((1, 1, blk), lambda i: (i, 0, 0)),
                   pl.BlockSpec((N_EXPERTS, LANES), lambda i: (0, 0))],
        out_shape=[jax.ShapeDtypeStruct((nblk, 1, blk), jnp.int32),
                   jax.ShapeDtypeStruct((N_EXPERTS, LANES), jnp.int32)],
        scratch_shapes=[pltpu.VMEM((N_EXPERTS, LANES), F32)],
        compiler_params=_cparams("arbitrary"),
        name="expert_rank",
    )(flat_e.reshape(nblk, 1, blk), tri)
    return rank.reshape(m), cnt[:, 0]


def _to_row_tiles(ref, x, n_rows, tiles):
    for s in range(tiles):
        ref[pl.ds(s, n_rows, stride=tiles), :] = x[:, s * LANES:(s + 1) * LANES]


def _from_row_tiles(ref, n_rows, tiles):
    return jnp.concatenate([ref[pl.ds(s, n_rows, stride=tiles), :] for s in range(tiles)], axis=1)


def _dispatch_kernel(dest_ref, x_ref, init_ref, xs_ref, buf_ref, sem, *, tm, tiles):
    del init_ref
    i = pl.program_id(0)
    n = pl.num_programs(0)
    slot = i % 2

    def drain(s):
        for _ in range(2):
            pltpu.make_async_copy(buf_ref.at[s], xs_ref.at[pl.ds(0, tm * tiles), :], sem.at[s]).wait()

    @pl.when(i >= 2)
    def _():
        drain(slot)

    _to_row_tiles(buf_ref.at[slot], x_ref[...].astype(F32), tm, tiles)

    def issue(r, c):
        src = buf_ref.at[slot, pl.ds(pl.multiple_of(r * tiles, tiles), tiles), :]
        for k in range(2):
            d = pl.multiple_of(dest_ref[0, 0, 2 * r + k] * tiles, tiles)
            pltpu.make_async_copy(src, xs_ref.at[pl.ds(d, tiles), :], sem.at[slot]).start(priority=k)
        return c

    lax.fori_loop(0, tm, issue, 0, unroll=4)

    @pl.when(i == n - 1)
    def _():
        @pl.when(i >= 1)
        def _():
            drain(1 - slot)

        drain(slot)


def _dispatch(h_all, dest, xs_init):
    n, d = h_all.shape
    tiles = d // LANES
    tm = 512
    nt = n // tm
    return pl.pallas_call(
        functools.partial(_dispatch_kernel, tm=tm, tiles=tiles),
        grid=(nt,),
        in_specs=[
            pl.BlockSpec((1, 1, 2 * tm), lambda i: (i, 0, 0), memory_space=pltpu.SMEM),
            pl.BlockSpec((tm, d), lambda i: (i, 0)),
            pl.BlockSpec(memory_space=pl.ANY),
        ],
        out_specs=pl.BlockSpec(memory_space=pl.ANY),
        out_shape=jax.ShapeDtypeStruct(xs_init.shape, F32),
        scratch_shapes=[pltpu.VMEM((2, tm * tiles, LANES), F32), pltpu.SemaphoreType.DMA((2,))],
        input_output_aliases={2: 0},
        compiler_params=_cparams("arbitrary"),
        name="moe_dispatch",
    )(dest.reshape(nt, 1, 2 * tm), h_all, xs_init)


def _experts_kernel(be_ref, nu_ref, x_ref, wg_ref, wu_ref, wd_ref, o_ref, wgb_ref, wub_ref, wdb_ref,
                    *, tb, tiles):
    i = pl.program_id(0)
    changed = jnp.logical_or(i == 0, be_ref[i] != be_ref[jnp.maximum(i - 1, 0)])

    @pl.when(changed)
    def _():
        wgb_ref[...] = wg_ref[0, 0].astype(BF16)
        wub_ref[...] = wu_ref[0, 0].astype(BF16)
        wdb_ref[...] = wd_ref[0, 0].astype(BF16)

    @pl.when(i < nu_ref[0])
    def _():
        x = _from_row_tiles(x_ref, tb, tiles).astype(BF16)
        g = jnp.dot(x, wgb_ref[...], preferred_element_type=F32)
        u = jnp.dot(x, wub_ref[...], preferred_element_type=F32)
        a = (g * jax.nn.sigmoid(g) * u).astype(BF16)
        _to_row_tiles(o_ref, jnp.dot(a, wdb_ref[...], preferred_element_type=F32), tb, tiles)

    @pl.when(i >= nu_ref[0])
    def _():
        o_ref[...] = jnp.zeros_like(o_ref)


def _experts(xs, block_e, n_used, w_gate, w_up, w_down, layer):
    d, hid = w_gate.shape[2], w_gate.shape[3]
    tiles = d // LANES
    tb = EXPERT_BLOCK
    nb = xs.shape[0] // (tb * tiles)
    grid_spec = pltpu.PrefetchScalarGridSpec(
        num_scalar_prefetch=2,
        grid=(nb,),
        in_specs=[
            pl.BlockSpec((tb * tiles, LANES), lambda i, be, nu: (i, 0)),
            pl.BlockSpec((1, 1, d, hid), lambda i, be, nu: (layer, be[i], 0, 0)),
            pl.BlockSpec((1, 1, d, hid), lambda i, be, nu: (layer, be[i], 0, 0)),
            pl.BlockSpec((1, 1, hid, d), lambda i, be, nu: (layer, be[i], 0, 0)),
        ],
        out_specs=pl.BlockSpec((tb * tiles, LANES), lambda i, be, nu: (i, 0)),
        scratch_shapes=[pltpu.VMEM((d, hid), BF16), pltpu.VMEM((d, hid), BF16), pltpu.VMEM((hid, d), BF16)],
    )
    return pl.pallas_call(
        functools.partial(_experts_kernel, tb=tb, tiles=tiles),
        grid_spec=grid_spec,
        out_shape=jax.ShapeDtypeStruct(xs.shape, F32),
        compiler_params=_cparams("arbitrary"),
        name="experts",
    )(block_e, n_used, xs, w_gate, w_up, w_down)


def _moe_ln_kernel(dest_ref, dest_next_ref, x_ref, gt_ref, g_ref, lng_ref, lnb_ref, ys_ref, o_ref,
                   y0_ref, y1_ref, sem0, sem1, *, tm, tiles, alpha):
    i = pl.program_id(0)
    n = pl.num_programs(0)
    slot = i % 2

    def gather(idx_ref, slot):
        def issue(r, c):
            dst = pl.ds(pl.multiple_of(r * tiles, tiles), tiles)
            d0 = pl.multiple_of(idx_ref[0, 0, 2 * r] * tiles, tiles)
            d1 = pl.multiple_of(idx_ref[0, 0, 2 * r + 1] * tiles, tiles)
            pltpu.make_async_copy(ys_ref.at[pl.ds(d0, tiles), :], y0_ref.at[slot, dst, :],
                                  sem0.at[slot]).start(priority=0)
            pltpu.make_async_copy(ys_ref.at[pl.ds(d1, tiles), :], y1_ref.at[slot, dst, :],
                                  sem1.at[slot]).start(priority=1)
            return c

        lax.fori_loop(0, tm, issue, 0, unroll=4)

    @pl.when(i == 0)
    def _():
        gather(dest_ref, 0)

    @pl.when(i + 1 < n)
    def _():
        gather(dest_next_ref, 1 - slot)

    pltpu.make_async_copy(ys_ref.at[pl.ds(0, tm * tiles), :], y0_ref.at[slot], sem0.at[slot]).wait()
    pltpu.make_async_copy(ys_ref.at[pl.ds(0, tm * tiles), :], y1_ref.at[slot], sem1.at[slot]).wait()
    gt = gt_ref[...]
    y = (gt[:, 0:1] * _from_row_tiles(y0_ref.at[slot], tm, tiles)
         + gt[:, 1:2] * _from_row_tiles(y1_ref.at[slot], tm, tiles))
    o_ref[...] = _layer_norm(alpha * x_ref[...] + g_ref[0] * y, lng_ref[...], lnb_ref[...])


def _moe_ln(x, ys, dest, gates, tok_off, g, ln_g, ln_b, alpha):
    b, s, d = x.shape
    tiles = d // LANES
    n_tok = gates.shape[0]
    tm = min(s, 512)
    nt = s // tm
    off = tok_off // tm
    n_steps = b * nt
    vec = lambda: pl.BlockSpec((1, d), lambda i: (0, 0))
    dest_blocks = dest.reshape(n_tok // tm, 1, 2 * tm)
    out = pl.pallas_call(
        functools.partial(_moe_ln_kernel, tm=tm, tiles=tiles, alpha=alpha),
        grid=(n_steps,),
        in_specs=[
            pl.BlockSpec((1, 1, 2 * tm), lambda i: (off + i, 0, 0), memory_space=pltpu.SMEM),
            pl.BlockSpec((1, 1, 2 * tm), lambda i: (off + jnp.minimum(i + 1, n_steps - 1), 0, 0),
                         memory_space=pltpu.SMEM),
            pl.BlockSpec((tm, d), lambda i: (i, 0)),
            pl.BlockSpec((tm, 2), lambda i: (off + i, 0)),
            pl.BlockSpec((1, 1, d), lambda i: (i // nt, 0, 0)),
            vec(), vec(),
            pl.BlockSpec(memory_space=pl.ANY),
        ],
        out_specs=pl.BlockSpec((tm, d), lambda i: (i, 0)),
        out_shape=jax.ShapeDtypeStruct((b * s, d), F32),
        scratch_shapes=[pltpu.VMEM((2, tm * tiles, LANES), F32), pltpu.VMEM((2, tm * tiles, LANES), F32),
                        pltpu.SemaphoreType.DMA((2,)), pltpu.SemaphoreType.DMA((2,))],
        compiler_params=_cparams("arbitrary"),
        name="moe_res_ln",
    )(dest_blocks, dest_blocks, x.reshape(b * s, d), gates, g, ln_g.reshape(1, d), ln_b.reshape(1, d), ys)
    return out.reshape(b, s, d)


def _route(logits):
    n = logits.shape[0]
    l1 = logits[:, :N_GROUPS]
    grp = jnp.argmax(l1, axis=-1).astype(jnp.int32)
    p_grp = 1.0 / jnp.sum(jnp.exp(l1 - jnp.max(l1, axis=-1, keepdims=True)), axis=-1)
    lg2 = logits[:, N_GROUPS:N_GROUPS + N_EXPERTS].reshape(n, N_GROUPS, EXPERTS_PER_GROUP)
    gsel = grp[:, None] == jnp.arange(N_GROUPS, dtype=jnp.int32)[None, :]
    sel = jnp.sum(jnp.where(gsel[:, :, None], lg2, 0.0), axis=1)
    i1 = jnp.argmax(sel, axis=-1).astype(jnp.int32)
    v1 = jnp.max(sel, axis=-1)
    col = jnp.arange(EXPERTS_PER_GROUP, dtype=jnp.int32)[None, :]
    sel2 = jnp.where(col == i1[:, None], -jnp.inf, sel)
    i2 = jnp.argmax(sel2, axis=-1).astype(jnp.int32)
    v2 = jnp.max(sel2, axis=-1)
    e2 = jnp.exp(v2 - v1)
    w1 = 1.0 / (1.0 + e2)
    gates = p_grp[:, None] * jnp.stack([w1, e2 * w1], axis=-1)
    experts = grp[:, None] * EXPERTS_PER_GROUP + jnp.stack([i1, i2], axis=-1)
    return experts, gates.astype(F32)


def _moe_slots(n_tok, d):
    nb = -(-2 * n_tok // EXPERT_BLOCK) + N_EXPERTS
    return jnp.zeros((nb * EXPERT_BLOCK * (d // LANES), LANES), F32)


def _moe(h_all, logits, xs_buf, w_gate, w_up, w_down, layer):
    n, d = h_all.shape
    m = 2 * n
    tb = EXPERT_BLOCK
    experts, gates = _route(logits)
    flat_e = experts.reshape(m)
    rank, counts = _expert_ranks(flat_e)
    padded = (counts + tb - 1) // tb * tb
    pad_end = jnp.cumsum(padded)
    pad_start = pad_end - padded
    onehot = flat_e[:, None] == jnp.arange(N_EXPERTS, dtype=jnp.int32)[None, :]
    dest = jnp.sum(jnp.where(onehot, pad_start[None, :], 0), axis=-1).astype(jnp.int32) + rank
    nb = xs_buf.shape[0] // (tb * (d // LANES))
    blk_start = jnp.arange(nb, dtype=jnp.int32) * tb
    block_e = jnp.minimum(jnp.sum(pad_end[None, :] <= blk_start[:, None], axis=-1), N_EXPERTS - 1)
    n_used = (pad_end[-1] // tb).reshape(1)

    xs = _dispatch(h_all, dest, xs_buf)
    ys = _experts(xs, block_e.astype(jnp.int32), n_used.astype(jnp.int32), w_gate, w_up, w_down, layer)
    return ys, dest, gates, xs


def _prep_kernel(z_ref, cos_ref, sin_ref, gq_ref, gk_ref, grp_ref,
                 qt_ref, kc_ref, vt_ref, qtd_ref, kd_ref, vtd_ref, *, hq, hkv, hd, ts, dchunk):
    cos = cos_ref[...]
    sin = sin_ref[...]
    lane = lax.broadcasted_iota(jnp.int32, (ts, LANES), 1)
    first_half = (lane % 32) < 16

    def norm_rope(x, g):
        ss = jnp.dot(x * x, grp_ref[...], preferred_element_type=F32, precision=HIGHEST)
        xn = x * lax.rsqrt(ss * (1.0 / HEAD_DIM) + RMS_EPS) * g
        partner = jnp.where(first_half, pltpu.roll(xn, LANES - 16, 1), pltpu.roll(xn, 16, 1))
        return xn * cos + partner * sin

    col = 0
    for s in range(hq // 2):
        x = norm_rope(z_ref[0, :, col:col + LANES], gq_ref[...]) * (ATTN_SCALE * LOG2E)
        t = x.T.astype(BF16)
        qt_ref[0, 2 * s] = t[0:HEAD_DIM]
        qt_ref[0, 2 * s + 1] = t[HEAD_DIM:]
        col += LANES
    for s in range(hkv // 2):
        x = norm_rope(z_ref[0, :, col:col + LANES], gk_ref[...]).astype(BF16)
        kc_ref[0, 2 * s, 0] = x[:, 0:HEAD_DIM]
        kc_ref[0, 2 * s + 1, 0] = x[:, HEAD_DIM:]
        col += LANES
    tail = (lax.broadcasted_iota(jnp.int32, (V_ROWS - HEAD_DIM, ts), 0) == 0).astype(BF16)
    tail_d = (lax.broadcasted_iota(jnp.int32, (V_ROWS - HEAD_DIM, dchunk), 0) == 0).astype(BF16)
    for s in range(hkv // 2):
        t = z_ref[0, :, col:col + LANES].T.astype(BF16)
        for hh in range(2):
            vt_ref[0, 2 * s + hh, 0, 0:HEAD_DIM, :] = t[hh * HEAD_DIM:(hh + 1) * HEAD_DIM]
            vt_ref[0, 2 * s + hh, 0, HEAD_DIM:V_ROWS, :] = tail
        col += LANES
    for s in range(hd // 2):
        t = (z_ref[0, :, col:col + LANES] * (ATTN_SCALE * LOG2E)).T.astype(BF16)
        qtd_ref[0, 2 * s] = t[0:HEAD_DIM]
        qtd_ref[0, 2 * s + 1] = t[HEAD_DIM:]
        col += LANES
    for s in range(hd // 2):
        x = z_ref[0, :, col:col + LANES].astype(BF16)
        for c in range(ts // dchunk):
            kd_ref[0, 2 * s, c] = x[c * dchunk:(c + 1) * dchunk, 0:HEAD_DIM]
            kd_ref[0, 2 * s + 1, c] = x[c * dchunk:(c + 1) * dchunk, HEAD_DIM:]
        col += LANES
    for s in range(hd // 2):
        t = z_ref[0, :, col:col + LANES].T.astype(BF16)
        for c in range(ts // dchunk):
            for hh in range(2):
                vtd_ref[0, 2 * s + hh, c, 0:HEAD_DIM, :] = t[hh * HEAD_DIM:(hh + 1) * HEAD_DIM,
                                                             c * dchunk:(c + 1) * dchunk]
                vtd_ref[0, 2 * s + hh, c, HEAD_DIM:V_ROWS, :] = tail_d
        col += LANES


def _attn_prep(z, cos, sin, q_g, k_g, hq, hkv, hd):
    b, s, n = z.shape
    ts = min(s, 512)
    dchunk = 256
    nt = s // ts
    hdim = HEAD_DIM
    gq = jnp.tile(q_g, 2).reshape(1, LANES)
    gk = jnp.tile(k_g, 2).reshape(1, LANES)
    li = jnp.arange(LANES)
    grp = (li[:, None] // hdim == li[None, :] // hdim).astype(F32)
    kern = functools.partial(_prep_kernel, hq=hq, hkv=hkv, hd=hd, ts=ts, dchunk=dchunk)
    const = lambda shape: pl.BlockSpec(shape, lambda bi, i: (0,) * len(shape))
    return pl.pallas_call(
        kern,
        grid=(b, nt),
        in_specs=[
            pl.BlockSpec((1, ts, n), lambda bi, i: (bi, i, 0)),
            pl.BlockSpec((ts, LANES), lambda bi, i: (i, 0)),
            pl.BlockSpec((ts, LANES), lambda bi, i: (i, 0)),
            const((1, LANES)), const((1, LANES)), const((LANES, LANES)),
        ],
        out_specs=[
            pl.BlockSpec((1, hq, hdim, ts), lambda bi, i: (bi, 0, 0, i)),
            pl.BlockSpec((1, hkv, 1, ts, hdim), lambda bi, i: (bi, 0, i, 0, 0)),
            pl.BlockSpec((1, hkv, 1, V_ROWS, ts), lambda bi, i: (bi, 0, i, 0, 0)),
            pl.BlockSpec((1, hd, hdim, ts), lambda bi, i: (bi, 0, 0, i)),
            pl.BlockSpec((1, hd, ts // dchunk, dchunk, hdim), lambda bi, i: (bi, 0, i, 0, 0)),
            pl.BlockSpec((1, hd, ts // dchunk, V_ROWS, dchunk), lambda bi, i: (bi, 0, i, 0, 0)),
        ],
        out_shape=[
            jax.ShapeDtypeStruct((b, hq, hdim, s), BF16),
            jax.ShapeDtypeStruct((b, hkv, nt, ts, hdim), BF16),
            jax.ShapeDtypeStruct((b, hkv, nt, V_ROWS, ts), BF16),
            jax.ShapeDtypeStruct((b, hd, hdim, s), BF16),
            jax.ShapeDtypeStruct((b, hd, s // dchunk, dchunk, hdim), BF16),
            jax.ShapeDtypeStruct((b, hd, s // dchunk, V_ROWS, dchunk), BF16),
        ],
        compiler_params=_cparams("parallel", "parallel"),
        name="attn_prep",
    )(z, cos, sin, gq, gk, grp)


def _rope_tables(s):
    t = jnp.arange(s)
    row = (t // GRID_W).astype(F32)
    colp = (t % GRID_W).astype(F32)
    axis_dim = HEAD_DIM // 2
    inv = ROPE_THETA ** (-jnp.arange(0, axis_dim, 2, dtype=F32) / axis_dim)
    ar = row[:, None] * inv
    ac = colp[:, None] * inv
    cr, sr, cc, sc = jnp.cos(ar), jnp.sin(ar), jnp.cos(ac), jnp.sin(ac)
    cos = jnp.concatenate([cr, cr, cc, cc], axis=-1)
    sin = jnp.concatenate([-sr, sr, -sc, sc], axis=-1)
    return jnp.tile(cos, (1, 2)), jnp.tile(sin, (1, 2))


def _gqa_kernel(qt_ref, k_ref, vt_ref, kc_ref, vtc_ref, o_ref, acc_ref, s_ref, sc_ref, *, group):
    n_chunks = k_ref.shape[2]
    tq = qt_ref.shape[3]

    def q_all():
        return jnp.concatenate([qt_ref[0, g] for g in range(group)], axis=1)

    def scores(kb, slot):
        st = jnp.dot(kb, q_all(), preferred_element_type=F32)
        s_ref[slot] = st
        return jnp.max(st, axis=0, keepdims=True)

    def absorb(slot, vb, cmax, m):
        m_new = jnp.maximum(m, cmax)
        alpha = jnp.exp2(m - m_new)
        p = jnp.exp2(s_ref[slot] - m_new)
        acc_ref[...] = alpha * acc_ref[...] + jnp.dot(vb, p.astype(BF16), preferred_element_type=F32)
        return m_new

    per_trip = 16 if n_chunks % 16 == 0 else 2
    st = jnp.dot(kc_ref[0, 0], q_all(), preferred_element_type=F32)
    sc_ref[...] = st
    m = jnp.max(st, axis=0, keepdims=True)
    cm = scores(k_ref[0, 0, 0], 0)
    p = jnp.exp2(sc_ref[...] - m)
    acc_ref[...] = jnp.dot(vtc_ref[0, 0], p.astype(BF16), preferred_element_type=F32)

    def body(i, carry):
        m, cm_cur = carry
        j0 = per_trip * i
        for u in range(per_trip):
            cm_next = scores(k_ref[0, 0, jnp.minimum(j0 + u + 1, n_chunks - 1)], (u + 1) % 2)
            m = absorb(u % 2, vt_ref[0, 0, j0 + u], cm_cur, m)
            cm_cur = cm_next
        return m, cm_cur

    lax.fori_loop(0, n_chunks // per_trip, body, (m, cm))
    out = acc_ref[0:HEAD_DIM, :] / acc_ref[HEAD_DIM:HEAD_DIM + 1, :]
    for g in range(group):
        o_ref[0, g] = out[:, g * tq:(g + 1) * tq].astype(BF16)


def _gqa(qt, k, vt, k_ctx, vt_ctx):
    b, hq, hdim, s = qt.shape
    hkv, n_chunks, tk = k.shape[1], k.shape[2], k.shape[3]
    n_ctx = k_ctx.shape[2]
    group = hq // hkv
    tq = min(s, 256)
    return pl.pallas_call(
        functools.partial(_gqa_kernel, group=group),
        grid=(b, hkv, s // tq),
        in_specs=[
            pl.BlockSpec((1, group, hdim, tq), lambda bi, h, i: (bi, h, 0, i)),
            pl.BlockSpec((1, 1, n_chunks, tk, hdim), lambda bi, h, i: (bi, h, 0, 0, 0)),
            pl.BlockSpec((1, 1, n_chunks, V_ROWS, tk), lambda bi, h, i: (bi, h, 0, 0, 0)),
            pl.BlockSpec((1, 1, n_ctx, hdim), lambda bi, h, i: (bi, h, 0, 0)),
            pl.BlockSpec((1, 1, V_ROWS, n_ctx), lambda bi, h, i: (bi, h, 0, 0)),
        ],
        out_specs=pl.BlockSpec((1, group, hdim, tq), lambda bi, h, i: (bi, h, 0, i)),
        out_shape=jax.ShapeDtypeStruct((b, hq, hdim, s), BF16),
        scratch_shapes=[pltpu.VMEM((V_ROWS, group * tq), F32), pltpu.VMEM((2, tk, group * tq), F32),
                        pltpu.VMEM((n_ctx, group * tq), F32)],
        compiler_params=_cparams("parallel", "parallel", "parallel"),
        name="gqa_attention",
    )(qt, k, vt, k_ctx, vt_ctx)


def _ctx_gqa_kernel(qt_ref, k_ref, vt_ref, o_ref):
    st = jnp.dot(k_ref[0, 0], qt_ref[0, 0], preferred_element_type=F32)
    p = jnp.exp2(st - jnp.max(st, axis=0, keepdims=True))
    acc = jnp.dot(vt_ref[0, 0], p.astype(BF16), preferred_element_type=F32)
    o_ref[0, 0] = (acc[0:HEAD_DIM] / acc[HEAD_DIM:HEAD_DIM + 1]).astype(BF16)


def _ctx_gqa(qt, k, vt):
    b, hq, hdim, n = qt.shape
    group = hq // k.shape[1]
    return pl.pallas_call(
        _ctx_gqa_kernel,
        grid=(b, hq),
        in_specs=[
            pl.BlockSpec((1, 1, hdim, n), lambda bi, h: (bi, h, 0, 0)),
            pl.BlockSpec((1, 1, n, hdim), lambda bi, h: (bi, h // group, 0, 0)),
            pl.BlockSpec((1, 1, V_ROWS, n), lambda bi, h: (bi, h // group, 0, 0)),
        ],
        out_specs=pl.BlockSpec((1, 1, hdim, n), lambda bi, h: (bi, h, 0, 0)),
        out_shape=jax.ShapeDtypeStruct((b, hq, hdim, n), BF16),
        compiler_params=_cparams("parallel", "parallel"),
        name="ctx_gqa",
    )(qt, k, vt)


def _na_kernel(qt_ref, k_ref, vt_ref, kc_ref, vtc_ref, bias_ref, o_ref, s_ref, *, n_band, nq, per_step):
    step = pl.program_id(2)
    tq = NA_Q_ROWS * GRID_W
    cw = k_ref.shape[3]
    n_ctx = kc_ref.shape[2]

    def scores(j, slot):
        i = step * per_step + j
        c0 = jnp.clip(i - 1, 0, nq - n_band)
        variant = jnp.where(i == 0, 0, jnp.where(i == nq - 1, 2, 1))
        qt = qt_ref[0, 0, :, j * tq:(j + 1) * tq]
        st = jnp.dot(kc_ref[0, 0], qt, preferred_element_type=F32)
        s_ref[slot, 0:n_ctx, :] = st
        m = jnp.max(st, axis=0, keepdims=True)
        for c in range(n_band):
            b0 = pl.multiple_of((variant * n_band + c) * cw, cw)
            st = (jnp.dot(k_ref[0, 0, c0 + c], qt, preferred_element_type=F32)
                  + bias_ref[0, pl.ds(b0, cw), :])
            s_ref[slot, n_ctx + c * cw:n_ctx + (c + 1) * cw, :] = st
            m = jnp.maximum(m, jnp.max(st, axis=0, keepdims=True))
        return m, c0

    def finish(j, slot, m, c0):
        p = jnp.exp2(s_ref[slot, 0:n_ctx, :] - m).astype(BF16)
        acc = jnp.dot(vtc_ref[0, 0], p, preferred_element_type=F32)
        for c in range(n_band):
            p = jnp.exp2(s_ref[slot, n_ctx + c * cw:n_ctx + (c + 1) * cw, :] - m).astype(BF16)
            acc = acc + jnp.dot(vt_ref[0, 0, c0 + c], p, preferred_element_type=F32)
        o_ref[0, 0, :, j * tq:(j + 1) * tq] = (acc[0:HEAD_DIM] / acc[HEAD_DIM:HEAD_DIM + 1]).astype(BF16)

    nxt = scores(0, 0)
    for j in range(per_step):
        cur = nxt
        if j + 1 < per_step:
            nxt = scores(j + 1, (j + 1) % 2)
        finish(j, j % 2, *cur)


def _na_bias_table(rel_bias, rows):
    w = GRID_W
    h, n_rel_rows, n_rel_cols = rel_bias.shape
    qcol = np.arange(w)
    cstart = np.clip(qcol - NA_WIN_COLS // 2, 0, w - NA_WIN_COLS)
    in_win = (qcol[None, :] >= cstart[:, None]) & (qcol[None, :] < cstart[:, None] + NA_WIN_COLS)
    col_idx = np.clip(qcol[None, :] - qcol[:, None] + NA_WIN_COLS - 1, 0, n_rel_cols - 1)
    onehot = (col_idx.reshape(-1)[None, :] == np.arange(n_rel_cols)[:, None]).astype(np.float32)
    cols = jnp.einsum('hrj,jn->hrn', rel_bias, onehot, precision=HIGHEST).reshape(h, n_rel_rows, w, w)
    cols = jnp.where(in_win[None, None], cols, NEG_BIG)
    ext = jnp.concatenate([cols, jnp.full((h, 1, w, w), NEG_BIG, F32)], axis=1)
    a = np.arange(NA_Q_ROWS)[:, None]
    c = np.arange(NA_BAND_ROWS)[None, :]
    tables = []
    for r0, rs0 in ((0, 0), (NA_Q_ROWS, 0), (rows - NA_Q_ROWS, rows - NA_BAND_ROWS)):
        r = r0 + a
        kr = rs0 + c
        rs = np.clip(r - NA_WIN_ROWS // 2, 0, rows - NA_WIN_ROWS)
        valid = (kr >= rs) & (kr < rs + NA_WIN_ROWS)
        row_idx = np.where(valid, kr - r + NA_WIN_ROWS - 1, n_rel_rows)
        bias = ext[:, row_idx.reshape(-1)].reshape(h, NA_Q_ROWS, NA_BAND_ROWS, w, w)
        tables.append(bias.transpose(0, 2, 4, 1, 3).reshape(h, NA_BAND_ROWS * w, NA_Q_ROWS * w))
    return (jnp.stack(tables, axis=1) * LOG2E).astype(F32)


def _na(qt, k, vt, k_ctx, vt_ctx, bias):
    b, h, hdim, s = qt.shape
    n_chunks, cw = k.shape[2], k.shape[3]
    n_ctx = k_ctx.shape[2]
    tq = NA_Q_ROWS * GRID_W
    nq = s // tq
    n_band = NA_BAND_ROWS * GRID_W // cw
    band_w = NA_BAND_ROWS * GRID_W
    per_step = math.gcd(nq, NA_BLOCKS_PER_STEP)
    return pl.pallas_call(
        functools.partial(_na_kernel, n_band=n_band, nq=nq, per_step=per_step),
        grid=(b, h, nq // per_step),
        in_specs=[
            pl.BlockSpec((1, 1, hdim, per_step * tq), lambda bi, hi, i: (bi, hi, 0, i)),
            pl.BlockSpec((1, 1, n_chunks, cw, hdim), lambda bi, hi, i: (bi, hi, 0, 0, 0)),
            pl.BlockSpec((1, 1, n_chunks, V_ROWS, cw), lambda bi, hi, i: (bi, hi, 0, 0, 0)),
            pl.BlockSpec((1, 1, n_ctx, hdim), lambda bi, hi, i: (bi, hi, 0, 0)),
            pl.BlockSpec((1, 1, V_ROWS, n_ctx), lambda bi, hi, i: (bi, hi, 0, 0)),
            pl.BlockSpec((1, 3 * band_w, tq), lambda bi, hi, i: (hi, 0, 0)),
        ],
        out_specs=pl.BlockSpec((1, 1, hdim, per_step * tq), lambda bi, hi, i: (bi, hi, 0, i)),
        out_shape=jax.ShapeDtypeStruct((b, h, hdim, s), BF16),
        scratch_shapes=[pltpu.VMEM((2, n_ctx + band_w, tq), F32)],
        compiler_params=_cparams("parallel", "parallel", "parallel"),
        name="na_attention",
    )(qt, k, vt, k_ctx, vt_ctx, bias.reshape(h, 3 * band_w, tq))


def kernel(x, c, ctx, c_ctx, ada_w, ada_b, ln1_g, ln1_b, ln2_g, ln2_b, conv_w_in, conv_w_out, conv_a_dw_w, conv_a_dw_b, conv_a_ln_g, conv_a_ln_b, conv_b_dw_w, attn_w_in, attn_w_out, q_norm_g, k_norm_g, na_rel_bias, router_w1, router_b1, router_w2, router_b2, expert_w_gate, expert_w_up, expert_w_down):
    bsz, seq, d = x.shape
    n_ctx = ctx.shape[1]
    depth = ada_w.shape[0]
    alpha = (2 * depth) ** 0.25
    hq = (d // 2) // HEAD_DIM
    hkv = hq // 4
    hd = (d // 2) // HEAD_DIM

    cond8 = jnp.zeros((8, d), F32).at[:bsz].set(c).at[bsz].set(c_ctx)
    mods = _ada(cond8, ada_w, ada_b)

    def lat_mod(i, k):
        return mods[i, :bsz, k * d:(k + 1) * d].reshape(bsz, 1, d)

    def ctx_mod(i, k):
        return jnp.broadcast_to(mods[i, bsz, k * d:(k + 1) * d].reshape(1, 1, d), (bsz, 1, d))

    wr = jnp.zeros((depth, d, ROUTER_COLS), F32)
    wr = wr.at[:, :, :N_GROUPS].set(router_w1).at[:, :, N_GROUPS:N_GROUPS + N_EXPERTS].set(router_w2)
    br = jnp.zeros((depth, 1, ROUTER_COLS), F32)
    br = br.at[:, 0, :N_GROUPS].set(router_b1).at[:, 0, N_GROUPS:N_GROUPS + N_EXPERTS].set(router_b2)

    cos_lat, sin_lat = _rope_tables(seq)
    cos_ctx = jnp.ones((n_ctx, LANES), F32)
    sin_ctx = jnp.zeros((n_ctx, LANES), F32)

    xs_buf = _moe_slots(bsz * seq + bsz * n_ctx, d)
    n_all = bsz * seq + bsz * n_ctx
    moe_bufs = (jnp.zeros((n_all, d), BF16), jnp.zeros((n_all, ROUTER_COLS), F32))

    x_lat, x_ctx = x, ctx
    for i in range(depth):
        last = i == depth - 1
        odd = i % 2 == 1
        j = i // 2
        use_ctx = odd or not last
        if odd:
            w_in = attn_w_in[j].astype(BF16)
            w_out = attn_w_out[j].astype(BF16)
            z_lat = _mod_matmul(x_lat, lat_mod(i, 1), lat_mod(i, 0), w_in)
            z_ctx = _mod_matmul(x_ctx, ctx_mod(i, 1), ctx_mod(i, 0), w_in)
            qt, kc, vtc, qtd, kd, vtd = _attn_prep(z_lat, cos_lat, sin_lat, q_norm_g[j], k_norm_g[j], hq, hkv, hd)
            cqt, ckc, cvtc, cqtd, ckd, cvtd = _attn_prep(z_ctx, cos_ctx, sin_ctx, q_norm_g[j], k_norm_g[j],
                                                         hq, hkv, hd)
            ckc = ckc.reshape(bsz, hkv, n_ctx, HEAD_DIM)
            cvtc = cvtc.reshape(bsz, hkv, V_ROWS, n_ctx)
            ckd = ckd.reshape(bsz, hd, n_ctx, HEAD_DIM)
            cvtd = cvtd.reshape(bsz, hd, V_ROWS, n_ctx)
            ot_c = _gqa(qt, kc, vtc, ckc, cvtc)
            bias = _na_bias_table(na_rel_bias[j], seq // GRID_W)
            ot_d = _na(qtd, kd, vtd, ckd, cvtd, bias)
            mix_lat = (ot_c, ot_d)
            if not last:
                mix_ctx = (_ctx_gqa(cqt, ckc, cvtc), _ctx_gqa(cqtd, ckd, cvtd))
        else:
            w_in = conv_w_in[j].astype(BF16)
            w_out = conv_w_out[j].astype(BF16)
            conv_w = (conv_a_dw_w[j], conv_a_dw_b[j], conv_a_ln_g[j], conv_a_ln_b[j], conv_b_dw_w[j])
            z_lat = _mod_matmul(x_lat, lat_mod(i, 1), lat_mod(i, 0), w_in)
            mix_lat = _conv_mixers(z_lat, *conv_w)
            if not last:
                z_ctx = _mod_matmul(x_ctx, ctx_mod(i, 1), ctx_mod(i, 0), w_in)
                mix_ctx = _conv_mixers(z_ctx, *conv_w)

        n_lat = bsz * seq
        n_tok = n_lat if last else n_lat + bsz * n_ctx
        bufs = None if last else moe_bufs
        x_lat, h_all, lg_all = _proj_ln(mix_lat, w_out, x_lat, lat_mod(i, 2), ln1_g[i], ln1_b[i],
                                        lat_mod(i, 4), lat_mod(i, 3), wr[i], br[i], alpha, n_tok, 0, bufs=bufs)
        moe_w = (expert_w_gate, expert_w_up, expert_w_down, i)
        if last:
            ys, dest, gates, xs_buf = _moe(h_all, lg_all, xs_buf, *moe_w)
            x_lat = _moe_ln(x_lat, ys, dest, gates, 0, lat_mod(i, 5), ln2_g[i], ln2_b[i], alpha)
        else:
            x_ctx, h_all, lg_all = _proj_ln(mix_ctx, w_out, x_ctx, ctx_mod(i, 2), ln1_g[i], ln1_b[i],
                                            ctx_mod(i, 4), ctx_mod(i, 3), wr[i], br[i], alpha, n_tok, n_lat,
                                            bufs=(h_all, lg_all))
            ys, dest, gates, xs_buf = _moe(h_all, lg_all, xs_buf, *moe_w)
            moe_bufs = (h_all, lg_all)
            x_lat = _moe_ln(x_lat, ys, dest, gates, 0, lat_mod(i, 5), ln2_g[i], ln2_b[i], alpha)
            x_ctx = _moe_ln(x_ctx, ys, dest, gates, n_lat, ctx_mod(i, 5), ln2_g[i], ln2_b[i], alpha)
    return x_lat
```
